```python
import math
import jax, jax.numpy as jnp
from jax import lax
import numpy as np

D_MODEL = 1024
BATCH = 4
SEQ = 4096
DEPTH = 4

CHUNK = 64
D_CONV = D_MODEL
CONV_WIDTH = 31
SSM_EXPAND = 2
D_INNER = SSM_EXPAND * D_MODEL
SSM_HEAD_DIM = 64
SSM_HEADS = D_INNER // SSM_HEAD_DIM
SSM_GROUPS = 8
HEADS_PER_GROUP = SSM_HEADS // SSM_GROUPS
D_STATE = 128
SSM_CONV_WIDTH = 4
SSD_CHUNK = CHUNK
D_XBC = D_INNER + 2 * SSM_GROUPS * D_STATE
COL_SIZES = (2 * D_CONV, D_INNER, D_XBC, SSM_HEADS, 2 * D_MODEL)
D_IN_PROJ = sum(COL_SIZES)
N_EXPERTS = 16
N_EXPERT_GROUPS = 4
EXPERTS_PER_GROUP = N_EXPERTS // N_EXPERT_GROUPS
TOP_K = 2
D_EXPERT = 512
MOE_BLOCK = 128
PLE_DIM = 256
ALPHA = (2.0 * DEPTH) ** 0.25
BETA = (8.0 * DEPTH) ** -0.25
LN_EPS = 1e-5
RMS_EPS = 1e-5

kernel_name = "hybrid_conformer_ssd_groupmoe_deepnorm"


def layer_norm(x, g, b):
    xf = x.astype(jnp.float32)
    mu = jnp.mean(xf, axis=-1, keepdims=True)
    xc = xf - mu
    var = jnp.mean(xc * xc, axis=-1, keepdims=True)
    return (xc * lax.rsqrt(var + LN_EPS) * g.astype(jnp.float32) + b.astype(jnp.float32)).astype(x.dtype)


def causal_dwconv(x, w, b):
    k = w.shape[0]
    y = lax.conv_general_dilated(
        x, w[:, None, :].astype(x.dtype), window_strides=(1,), padding=[(k - 1, 0)],
        dimension_numbers=("NWC", "WIO", "NWC"), feature_group_count=x.shape[-1])
    return y + b.astype(x.dtype)


def gated_group_rmsnorm(y, z, g):
    yz = (y * jax.nn.silu(z)).astype(jnp.float32)
    shp = yz.shape
    yz = yz.reshape(shp[:-1] + (SSM_GROUPS, D_INNER // SSM_GROUPS))
    yz = yz * lax.rsqrt(jnp.mean(yz * yz, axis=-1, keepdims=True) + RMS_EPS)
    return (yz.reshape(shp) * g.astype(jnp.float32)).astype(y.dtype)


def ssd_scan(xh, dt, a, bm, cm):
    bsz, seqlen = xh.shape[0], xh.shape[1]
    nc = seqlen // SSD_CHUNK
    q, g, r, p, n = SSD_CHUNK, SSM_GROUPS, HEADS_PER_GROUP, SSM_HEAD_DIM, D_STATE
    xdt = (xh.astype(jnp.float32) * dt[..., None]).reshape(bsz, nc, q, g, r, p)
    bc = bm.astype(jnp.float32).reshape(bsz, nc, q, g, n)
    cc = cm.astype(jnp.float32).reshape(bsz, nc, q, g, n)
    adt = (dt * a).reshape(bsz, nc, q, g, r).transpose(0, 1, 3, 4, 2)
    acs = jnp.cumsum(adt, axis=-1)
    diff = acs[..., :, None] - acs[..., None, :]
    causal = jnp.tril(jnp.ones((q, q), dtype=bool))
    lmat = jnp.exp(jnp.where(causal, diff, -jnp.inf))
    cb = jnp.einsum("bclgn,bcsgn->bcgls", cc, bc)
    y_diag = jnp.einsum("bcgrls,bcsgrp->bclgrp", cb[:, :, :, None] * lmat, xdt)
    decay_states = jnp.exp(acs[..., -1:] - acs).transpose(0, 1, 4, 2, 3)
    states = jnp.einsum("bclgn,bclgrp->bcgrpn", bc, xdt * decay_states[..., None])
    chunk_decay = jnp.exp(acs[..., -1])

    def step(h, inp):
        s_c, d_c = inp
        return h * d_c[..., None, None] + s_c, h

    h0 = jnp.zeros((bsz, g, r, p, n), jnp.float32)
    _, prev = lax.scan(step, h0, (states.transpose(1, 0, 2, 3, 4, 5), chunk_decay.transpose(1, 0, 2, 3)))
    prev = prev.transpose(1, 0, 2, 3, 4, 5)
    out_decay = jnp.exp(acs).transpose(0, 1, 4, 2, 3)
    y_off = jnp.einsum("bclgn,bcgrpn->bclgrp", cc, prev) * out_decay[..., None]
    return (y_diag + y_off).reshape(bsz, seqlen, SSM_HEADS, SSM_HEAD_DIM)


def token_mixer(u, w_in, b_glu, b_branch_gate, conv_w, conv_b, conv_ln_g, conv_ln_b, w_conv_out,
                ssm_conv_w, ssm_conv_b, dt_bias, a_log, d_skip, ssm_norm_g, w_ssm_out, w_out):
    bsz, seqlen, _ = u.shape
    proj = jnp.einsum("bld,de->ble", u, w_in)
    splits = np.cumsum(COL_SIZES)[:-1].tolist()
    glu_in, z, xbc, dt_raw, gate_in = jnp.split(proj, splits, axis=-1)
    glu_in = glu_in + b_glu
    ca, cg = jnp.split(glu_in, 2, axis=-1)
    c = ca * jax.nn.sigmoid(cg)
    c = causal_dwconv(c, conv_w, conv_b)
    c = jax.nn.silu(layer_norm(c, conv_ln_g, conv_ln_b))
    y_conv = jnp.einsum("blc,cd->bld", c, w_conv_out)
    xbc = jax.nn.silu(causal_dwconv(xbc, ssm_conv_w, ssm_conv_b))
    xs, bm, cm = jnp.split(xbc, [D_INNER, D_INNER + SSM_GROUPS * D_STATE], axis=-1)
    dt = jax.nn.softplus(dt_raw.astype(jnp.float32) + dt_bias.astype(jnp.float32))
    a = -jnp.exp(a_log.astype(jnp.float32))
    xh = xs.reshape(bsz, seqlen, SSM_HEADS, SSM_HEAD_DIM)
    y = ssd_scan(xh, dt, a,
                 bm.reshape(bsz, seqlen, SSM_GROUPS, D_STATE),
                 cm.reshape(bsz, seqlen, SSM_GROUPS, D_STATE))
    y = y + xh.astype(jnp.float32) * d_skip.astype(jnp.float32)[:, None]
    y = gated_group_rmsnorm(y.reshape(bsz, seqlen, D_INNER).astype(u.dtype), z, ssm_norm_g)
    y_ssm = jnp.einsum("ble,ed->bld", y, w_ssm_out)
    g_conv, g_ssm = jnp.split(jax.nn.sigmoid(gate_in + b_branch_gate), 2, axis=-1)
    merged = g_conv * y_conv + g_ssm * y_ssm
    return jnp.einsum("bld,de->ble", merged, w_out)


def grouped_moe(xf, w_router, router_bias, wg, wu, wd):
    t, d = xf.shape
    logits = jnp.einsum("td,de->te", xf.astype(jnp.float32), w_router.astype(jnp.float32))
    scores = jax.nn.sigmoid(logits)
    sel = scores + router_bias.astype(jnp.float32)
    sel_g = sel.reshape(t, N_EXPERT_GROUPS, EXPERTS_PER_GROUP)
    group_score = lax.top_k(sel_g, 2)[0].sum(-1)
    grp = jnp.argmax(group_score, axis=-1)
    sel_in = jnp.take_along_axis(sel_g, grp[:, None, None], axis=1)[:, 0]
    _, loc = lax.top_k(sel_in, TOP_K)
    eid = grp[:, None] * EXPERTS_PER_GROUP + loc
    wts = jnp.take_along_axis(scores, eid, axis=1)
    wts = wts / jnp.sum(wts, axis=-1, keepdims=True)
    n_assign = t * TOP_K
    flat_e = eid.reshape(-1).astype(jnp.int32)
    flat_t = jnp.repeat(jnp.arange(t, dtype=jnp.int32), TOP_K)
    flat_w = wts.reshape(-1)
    order = jnp.argsort(flat_e)
    se, st, sw = flat_e[order], flat_t[order], flat_w[order]
    counts = jnp.bincount(flat_e, length=N_EXPERTS).astype(jnp.int32)
    padded = (counts + MOE_BLOCK - 1) // MOE_BLOCK * MOE_BLOCK
    start = jnp.cumsum(counts) - counts
    pend = jnp.cumsum(padded)
    pstart = pend - padded
    dest = pstart[se] + jnp.arange(n_assign, dtype=jnp.int32) - start[se]
    n_rows = (n_assign + MOE_BLOCK - 1) // MOE_BLOCK * MOE_BLOCK + N_EXPERTS * MOE_BLOCK
    n_blocks = n_rows // MOE_BLOCK
    row_tok = jnp.full((n_rows,), t, jnp.int32).at[dest].set(st)
    row_w = jnp.zeros((n_rows,), xf.dtype).at[dest].set(sw.astype(xf.dtype))
    blk_e = jnp.minimum(jnp.searchsorted(pend, jnp.arange(n_blocks, dtype=jnp.int32) * MOE_BLOCK,
                                         side="right"), N_EXPERTS - 1).astype(jnp.int32)
    x_pad = jnp.concatenate([xf, jnp.zeros((1, d), xf.dtype)], axis=0)

    def run_block(args):
        tok, e = args
        xb = x_pad[tok]
        h = jax.nn.silu(xb @ wg[e]) * (xb @ wu[e])
        return h @ wd[e]

    y_rows = lax.map(run_block, (row_tok.reshape(n_blocks, MOE_BLOCK), blk_e))
    y_rows = y_rows.reshape(n_rows, d) * row_w[:, None]
    out = jnp.zeros((t + 1, d), xf.dtype).at[row_tok].add(y_rows)
    return out[:t]


def setup_inputs(seed: int = 0) -> dict:
    key = jax.random.key(seed)
    ks = jax.random.split(key, 32)
    f32 = jnp.float32
    nrm = lambda k, shape, scale: (jax.random.normal(k, shape, f32) * scale).astype(f32)
    L = DEPTH
    dt0 = jnp.exp(jax.random.uniform(ks[12], (L, SSM_HEADS), f32) * (math.log(0.1) - math.log(1e-3)) + math.log(1e-3))
    return {
        "x": nrm(ks[0], (BATCH, SEQ, D_MODEL), 1.0),
        "p": nrm(ks[1], (DEPTH, BATCH, SEQ, PLE_DIM), 1.0),
        "w_in": nrm(ks[2], (L, D_MODEL, D_IN_PROJ), D_MODEL ** -0.5),
        "b_glu": nrm(ks[3], (L, 2 * D_CONV), 0.02),
        "b_branch_gate": nrm(ks[4], (L, 2 * D_MODEL), 0.02),
        "conv_w": nrm(ks[5], (L, CONV_WIDTH, D_CONV), CONV_WIDTH ** -0.5),
        "conv_b": nrm(ks[6], (L, D_CONV), 0.02),
        "conv_ln_g": 1.0 + nrm(ks[7], (L, D_CONV), 0.02),
        "conv_ln_b": nrm(ks[8], (L, D_CONV), 0.02),
        "w_conv_out": nrm(ks[9], (L, D_CONV, D_MODEL), D_CONV ** -0.5),
        "ssm_conv_w": nrm(ks[10], (L, SSM_CONV_WIDTH, D_XBC), SSM_CONV_WIDTH ** -0.5),
        "ssm_conv_b": nrm(ks[11], (L, D_XBC), 0.02),
        "dt_bias": dt0 + jnp.log(-jnp.expm1(-dt0)),
        "a_log": jnp.log(jax.random.uniform(ks[13], (L, SSM_HEADS), f32, 1.0, 16.0)),
        "d_skip": 1.0 + nrm(ks[14], (L, SSM_HEADS), 0.02),
        "ssm_norm_g": 1.0 + nrm(ks[15], (L, D_INNER), 0.02),
        "w_ssm_out": nrm(ks[16], (L, D_INNER, D_MODEL), D_INNER ** -0.5),
        "w_out": nrm(ks[17], (L, D_MODEL, D_MODEL), BETA * D_MODEL ** -0.5),
        "ln1_g": 1.0 + nrm(ks[18], (L, D_MODEL), 0.02),
        "ln1_b": nrm(ks[19], (L, D_MODEL), 0.02),
        "w_router": nrm(ks[20], (D_MODEL, N_EXPERTS), D_MODEL ** -0.5),
        "router_bias": nrm(ks[21], (N_EXPERTS,), 0.01),
        "w_exp_gate": nrm(ks[22], (L, N_EXPERTS, D_MODEL, D_EXPERT), D_MODEL ** -0.5),
        "w_exp_up": nrm(ks[23], (L, N_EXPERTS, D_MODEL, D_EXPERT), D_MODEL ** -0.5),
        "w_exp_down": nrm(ks[24], (L, N_EXPERTS, D_EXPERT, D_MODEL), BETA * D_EXPERT ** -0.5),
        "w_ple_up": nrm(ks[25], (L, PLE_DIM, D_MODEL), BETA * PLE_DIM ** -0.5),
        "w_ple_gate": nrm(ks[26], (L, D_MODEL, D_MODEL), D_MODEL ** -0.5),
        "b_ple_gate": nrm(ks[27], (L, D_MODEL), 0.02),
        "ln2_g": 1.0 + nrm(ks[28], (L, D_MODEL), 0.02),
        "ln2_b": nrm(ks[29], (L, D_MODEL), 0.02),
    }


def reference(x, p, w_in, b_glu, b_branch_gate, conv_w, conv_b, conv_ln_g, conv_ln_b, w_conv_out,
              ssm_conv_w, ssm_conv_b, dt_bias, a_log, d_skip, ssm_norm_g, w_ssm_out, w_out,
              ln1_g, ln1_b, w_router, router_bias, w_exp_gate, w_exp_up, w_exp_down,
              w_ple_up, w_ple_gate, b_ple_gate, ln2_g, ln2_b):
    bsz, seqlen, d = x.shape
    for i in range(DEPTH):
        mix = token_mixer(x, w_in[i], b_glu[i], b_branch_gate[i], conv_w[i], conv_b[i],
                          conv_ln_g[i], conv_ln_b[i], w_conv_out[i], ssm_conv_w[i], ssm_conv_b[i],
                          dt_bias[i], a_log[i], d_skip[i], ssm_norm_g[i], w_ssm_out[i], w_out[i])
        x = layer_norm(ALPHA * x + mix, ln1_g[i], ln1_b[i])
        moe = grouped_moe(x.reshape(bsz * seqlen, d), w_router, router_bias,
                          w_exp_gate[i], w_exp_up[i], w_exp_down[i]).reshape(bsz, seqlen, d)
        ple = jnp.einsum("blk,kd->bld", p[i], w_ple_up[i]) * jax.nn.sigmoid(
            jnp.einsum("bld,de->ble", x, w_ple_gate[i]) + b_ple_gate[i])
        x = layer_norm(ALPHA * x + moe + ple, ln2_g[i], ln2_b[i])
    return x
```

```python
import functools

import jax
import jax.numpy as jnp
from jax import lax
from jax.experimental import pallas as pl
from jax.experimental.pallas import tpu as pltpu

F32 = jnp.float32
BF16 = jnp.bfloat16
I32 = jnp.int32

D_MODEL = 1024
D_CONV = 1024
CONV_WIDTH = 31
D_INNER = 2048
SSM_HEAD_DIM = 64
SSM_HEADS = 32
SSM_GROUPS = 8
HEADS_PER_GROUP = 4
D_STATE = 128
SSM_CONV_WIDTH = 4
D_XBC = D_INNER + 2 * SSM_GROUPS * D_STATE
GROUP_CH = HEADS_PER_GROUP * SSM_HEAD_DIM
N_EXPERTS = 16
N_EXPERT_GROUPS = 4
EXPERTS_PER_GROUP = 4
D_EXPERT = 512
PLE_DIM = 256
DEPTH = 4
ALPHA = (2.0 * DEPTH) ** 0.25
LN_EPS = 1e-5
RMS_EPS = 1e-5

LANES = 128
CONV_HALO = 32
SSM_HALO = 8
SSD_Q = 128
MOE_ROWS = 512
VMEM_LIMIT = 48 * 1024 * 1024

OFF_GLU = 0
OFF_Z = 2 * D_CONV
OFF_XBC = OFF_Z + D_INNER
OFF_DT = OFF_XBC + D_XBC
OFF_GATE = OFF_DT + SSM_HEADS


def _sigmoid(x):
    return 1.0 / (1.0 + jnp.exp(-x))


def _layer_norm(x, g, b):
    mu = jnp.mean(x, axis=-1, keepdims=True)
    xc = x - mu
    var = jnp.mean(xc * xc, axis=-1, keepdims=True)
    return xc * lax.rsqrt(var + LN_EPS) * g + b


def _params(n_axes):
    return pltpu.CompilerParams(dimension_semantics=("arbitrary",) * n_axes,
                                vmem_limit_bytes=VMEM_LIMIT)


def _mm_kernel(x_ref, w_ref, *rest, act, has_bias):
    if has_bias:
        b_ref, o_ref, wbf_ref = rest
    else:
        o_ref, wbf_ref = rest

    @pl.when(pl.program_id(1) == 0)
    def _():
        wbf_ref[...] = w_ref[...].astype(BF16)

    acc = jnp.dot(x_ref[...], wbf_ref[...], preferred_element_type=F32)
    if has_bias:
        acc = acc + b_ref[...]
    if act == "sigmoid":
        acc = _sigmoid(acc)
    elif act == "silu":
        acc = acc * _sigmoid(acc)
    elif act == "softplus":
        acc = jnp.maximum(acc, 0.0) + jnp.log1p(jnp.exp(-jnp.abs(acc)))
    o_ref[...] = acc.astype(o_ref.dtype)


def _matmul(x, w, w_index, bias, b_index, n_out, tn, act, out_dtype):
    t, k = x.shape
    tm = min(1024, t)
    w_block = (None,) * (w.ndim - 2) + (k, tn)
    in_specs = [pl.BlockSpec((tm, k), lambda n, m: (m, 0)),
                pl.BlockSpec(w_block, lambda n, m: w_index(n))]
    args = [x, w]
    if bias is not None:
        b_block = (None,) * (bias.ndim - 2) + (1, tn)
        in_specs.append(pl.BlockSpec(b_block, lambda n, m: b_index(n)))
        args.append(bias)
    return pl.pallas_call(
        functools.partial(_mm_kernel, act=act, has_bias=bias is not None),
        grid=(n_out // tn, t // tm),
        in_specs=in_specs,
        out_specs=pl.BlockSpec((tm, tn), lambda n, m: (m, n)),
        out_shape=jax.ShapeDtypeStruct((t, n_out), out_dtype),
        scratch_shapes=[pltpu.VMEM((k, tn), BF16)],
        compiler_params=_params(2),
        name="inproj_" + act,
    )(*args)


def _glu_kernel(x_ref, wa_ref, wg_ref, ba_ref, bg_ref, o_ref, wa_bf, wg_bf):
    @pl.when(pl.program_id(1) == 0)
    def _():
        wa_bf[...] = wa_ref[...].astype(BF16)
        wg_bf[...] = wg_ref[...].astype(BF16)

    x = x_ref[...]
    a = jnp.dot(x, wa_bf[...], preferred_element_type=F32) + ba_ref[...]
    g = jnp.dot(x, wg_bf[...], preferred_element_type=F32) + bg_ref[...]
    o_ref[...] = (a * _sigmoid(g)).astype(o_ref.dtype)


def _glu(x, w_in, b_glu3, layer, tn=512):
    t, k = x.shape
    tm = min(1024, t)
    half = D_CONV // tn
    return pl.pallas_call(
        _glu_kernel,
        grid=(half, t // tm),
        in_specs=[pl.BlockSpec((tm, k), lambda n, m: (m, 0)),
                  pl.BlockSpec((None, k, tn), lambda n, m: (layer, 0, n)),
                  pl.BlockSpec((None, k, tn), lambda n, m: (layer, 0, n + half)),
                  pl.BlockSpec((None, 1, tn), lambda n, m: (layer, 0, n)),
                  pl.BlockSpec((None, 1, tn), lambda n, m: (layer, 0, n + half))],
        out_specs=pl.BlockSpec((tm, tn), lambda n, m: (m, n)),
        out_shape=jax.ShapeDtypeStruct((t, D_CONV), BF16),
        scratch_shapes=[pltpu.VMEM((k, tn), BF16), pltpu.VMEM((k, tn), BF16)],
        compiler_params=_params(2),
        name="inproj_glu",
    )(x, w_in, w_in, b_glu3, b_glu3)


def _convbranch_kernel(c_ref, cw_ref, cb_ref, lg_ref, lb_ref, w_ref, gate_ref, o_ref,
                       ext_ref, wbf_ref, *, tl):
    first = (pl.program_id(0) == 0) & (pl.program_id(1) == 0)

    @pl.when(first)
    def _():
        wbf_ref[...] = w_ref[...].astype(BF16)

    @pl.when(pl.program_id(1) == 0)
    def _():
        ext_ref[0:CONV_HALO, :] = jnp.zeros((CONV_HALO, D_CONV), F32)

    @pl.when(pl.program_id(1) > 0)
    def _():
        ext_ref[0:CONV_HALO, :] = ext_ref[tl:tl + CONV_HALO, :]

    ext_ref[CONV_HALO:CONV_HALO + tl, :] = c_ref[...].astype(F32)

    base = CONV_HALO - (CONV_WIDTH - 1)
    acc = jnp.zeros((tl, D_CONV), F32) + cb_ref[...]
    for k in range(CONV_WIDTH):
        acc = acc + cw_ref[k:k + 1, :] * ext_ref[base + k:base + k + tl, :]
    h = _layer_norm(acc, lg_ref[...], lb_ref[...])
    h = h * _sigmoid(h)
    y = jnp.dot(h.astype(BF16), wbf_ref[...], preferred_element_type=F32)
    o_ref[...] = (y * gate_ref[...].astype(F32)).astype(o_ref.dtype)


def _convbranch(c, gates, conv_w, conv_b3, ln_g3, ln_b3, w_conv_out, layer, bsz, seqlen, tl=256):
    t = c.shape[0]
    nl = seqlen // tl
    vec = pl.BlockSpec((None, 1, D_CONV), lambda b, i: (layer, 0, 0))
    return pl.pallas_call(
        functools.partial(_convbranch_kernel, tl=tl),
        grid=(bsz, nl),
        in_specs=[pl.BlockSpec((tl, D_CONV), lambda b, i: (b * nl + i, 0)),
                  pl.BlockSpec((None, CONV_WIDTH, D_CONV), lambda b, i: (layer, 0, 0)),
                  vec, vec, vec,
                  pl.BlockSpec((None, D_CONV, D_MODEL), lambda b, i: (layer, 0, 0)),
                  pl.BlockSpec((tl, D_MODEL), lambda b, i: (b * nl + i, 0))],
        out_specs=pl.BlockSpec((tl, D_MODEL), lambda b, i: (b * nl + i, 0)),
        out_shape=jax.ShapeDtypeStruct((t, D_MODEL), BF16),
        scratch_shapes=[pltpu.VMEM((tl + CONV_HALO, D_CONV), F32),
                        pltpu.VMEM((D_CONV, D_MODEL), BF16)],
        compiler_params=_params(2),
        name="conv_module",
    )(c, conv_w, conv_b3, ln_g3, ln_b3, w_conv_out, gates)


def _ssd_kernel(xbc_ref, dt_ref, zs_ref, cw_ref, cb_ref, alog_ref, dskip_ref, ng_ref, o_ref,
                ext_ref, state_ref, *, q):
    @pl.when(pl.program_id(1) == 0)
    def _():
        ext_ref[0:SSM_HALO, :] = jnp.zeros((SSM_HALO, D_XBC), F32)
        state_ref[...] = jnp.zeros(state_ref.shape, F32)

    @pl.when(pl.program_id(1) > 0)
    def _():
        ext_ref[0:SSM_HALO, :] = ext_ref[q:q + SSM_HALO, :]

    ext_ref[SSM_HALO:SSM_HALO + q, :] = xbc_ref[...].astype(F32)

    base = SSM_HALO - (SSM_CONV_WIDTH - 1)
    conv = jnp.zeros((q, D_XBC), F32) + cb_ref[...]
    for k in range(SSM_CONV_WIDTH):
        conv = conv + cw_ref[k:k + 1, :] * ext_ref[base + k:base + k + q, :]
    xbc = conv * _sigmoid(conv)
    xs = xbc[:, :D_INNER]
    bm = xbc[:, D_INNER:D_INNER + SSM_GROUPS * D_STATE]
    cm = xbc[:, D_INNER + SSM_GROUPS * D_STATE:]

    dt = dt_ref[...]
    adt = dt * (-jnp.exp(alog_ref[...]))
    row = lax.broadcasted_iota(I32, (q, q), 0)
    col = lax.broadcasted_iota(I32, (q, q), 1)
    causal = row >= col
    tril = jnp.where(causal, 1.0, 0.0).astype(F32)
    acs = jnp.dot(tril, adt, preferred_element_type=F32, precision=lax.Precision.HIGHEST)
    acs_t = acs.T
    last = acs[q - 1:q, :]
    exp_acs = jnp.exp(acs)
    decay_to_end = jnp.exp(last - acs)
    chunk_decay = jnp.exp(last)

    for g in range(SSM_GROUPS):
        cg = cm[:, g * D_STATE:(g + 1) * D_STATE].astype(BF16)
        bg = bm[:, g * D_STATE:(g + 1) * D_STATE]
        cb = lax.dot_general(cg, bg.astype(BF16), (((1,), (1,)), ((), ())),
                             preferred_element_type=F32)
        s_prev = state_ref[g]
        y_off = jnp.dot(cg, s_prev.astype(BF16), preferred_element_type=F32)
        ys, xws, decs = [], [], []
        for r in range(HEADS_PER_GROUP):
            h = g * HEADS_PER_GROUP + r
            a_col = acs[:, h:h + 1]
            a_row = acs_t[h:h + 1, :]
            lmat = jnp.exp(jnp.where(causal, a_col - a_row, -jnp.inf))
            m = (cb * lmat).astype(BF16)
            xh = xs[:, h * SSM_HEAD_DIM:(h + 1) * SSM_HEAD_DIM]
            xdt = xh * dt[:, h:h + 1]
            y = jnp.dot(m, xdt.astype(BF16), preferred_element_type=F32)
            y = y + y_off[:, r * SSM_HEAD_DIM:(r + 1) * SSM_HEAD_DIM] * exp_acs[:, h:h + 1]
            y = y + xh * dskip_ref[:, h * SSM_HEAD_DIM:(h + 1) * SSM_HEAD_DIM]
            ys.append(y)
            xws.append(xdt * decay_to_end[:, h:h + 1])
            decs.append(jnp.broadcast_to(chunk_decay[:, h:h + 1], (1, SSM_HEAD_DIM)))
        yg = jnp.concatenate(ys, axis=1)
        xw = jnp.concatenate(xws, axis=1).astype(BF16)
        dec = jnp.concatenate(decs, axis=1)
        state_ref[g] = s_prev * dec + jnp.dot(bg.T.astype(BF16), xw, preferred_element_type=F32)

        yz = yg * zs_ref[:, g * GROUP_CH:(g + 1) * GROUP_CH].astype(F32)
        ms = jnp.mean(yz * yz, axis=-1, keepdims=True)
        yn = yz * lax.rsqrt(ms + RMS_EPS) * ng_ref[:, g * GROUP_CH:(g + 1) * GROUP_CH]
        o_ref[:, g * GROUP_CH:(g + 1) * GROUP_CH] = yn.astype(o_ref.dtype)


def _ssd(xbc, dt, zs, ssm_conv_w, ssm_conv_b3, alog3, dskip3, ng3, layer, bsz, seqlen):
    t = xbc.shape[0]
    q = SSD_Q
    nq = seqlen // q
    tile = lambda width: pl.BlockSpec((q, width), lambda b, i: (b * nq + i, 0))
    vec = lambda width: pl.BlockSpec((None, 1, width), lambda b, i: (layer, 0, 0))
    return pl.pallas_call(
        functools.partial(_ssd_kernel, q=q),
        grid=(bsz, nq),
        in_specs=[tile(D_XBC), tile(LANES), tile(D_INNER),
                  pl.BlockSpec((None, SSM_CONV_WIDTH, D_XBC), lambda b, i: (layer, 0, 0)),
                  vec(D_XBC), vec(LANES), vec(D_INNER), vec(D_INNER)],
        out_specs=tile(D_INNER),
        out_shape=jax.ShapeDtypeStruct((t, D_INNER), BF16),
        scratch_shapes=[pltpu.VMEM((q + SSM_HALO, D_XBC), F32),
                        pltpu.VMEM((SSM_GROUPS, D_STATE, GROUP_CH), F32)],
        compiler_params=_params(2),
        name="ssd_mixer",
    )(xbc, dt, zs, ssm_conv_w, ssm_conv_b3, alog3, dskip3, ng3)


def _first_argmax(vals):
    best, idx = vals[0], jnp.zeros(vals[0].shape, I32)
    for j in range(1, len(vals)):
        gt = vals[j] > best
        idx = jnp.where(gt, j, idx)
        best = jnp.where(gt, vals[j], best)
    return idx, best


def _select(idx, vals):
    out = vals[len(vals) - 1]
    for j in range(len(vals) - 2, -1, -1):
        out = jnp.where(idx == j, vals[j], out)
    return out


def _outproj_kernel(yn_ref, y1_ref, gs_ref, x_ref, wso_ref, wo_ref, lg_ref, lb_ref, wr_ref, rb_ref,
                    x1_ref, x1b_ref, eid_ref, rank_ref, wts_ref, cnt_ref,
                    wso_bf, wo_bf, base_ref, *, tm):
    @pl.when(pl.program_id(0) == 0)
    def _():
        wso_bf[...] = wso_ref[...].astype(BF16)
        wo_bf[...] = wo_ref[...].astype(BF16)
        base_ref[...] = jnp.zeros(base_ref.shape, F32)

    y_ssm = jnp.dot(yn_ref[...], wso_bf[...], preferred_element_type=F32)
    merged = y1_ref[...].astype(F32) + gs_ref[...].astype(F32) * y_ssm
    mix = jnp.dot(merged.astype(BF16), wo_bf[...], preferred_element_type=F32)
    x1 = _layer_norm(ALPHA * x_ref[...] + mix, lg_ref[...], lb_ref[...])
    x1_ref[...] = x1
    x1b_ref[...] = x1.astype(BF16)

    logits = lax.dot_general(wr_ref[...], x1, (((1,), (1,)), ((), ())),
                             preferred_element_type=F32, precision=lax.Precision.HIGHEST)
    scores = _sigmoid(logits)
    sel = scores + rb_ref[...]
    sel_rows = [sel[e:e + 1, :] for e in range(N_EXPERTS)]
    sc_rows = [scores[e:e + 1, :] for e in range(N_EXPERTS)]

    gscores = []
    for gi in range(N_EXPERT_GROUPS):
        v = sel_rows[gi * EXPERTS_PER_GROUP:(gi + 1) * EXPERTS_PER_GROUP]
        best = None
        for a in range(EXPERTS_PER_GROUP):
            for b in range(a + 1, EXPERTS_PER_GROUP):
                s = v[a] + v[b]
                best = s if best is None else jnp.maximum(best, s)
        gscores.append(best)
    grp, _ = _first_argmax(gscores)

    sel_in = [_select(grp, [sel_rows[gi * EXPERTS_PER_GROUP + j] for gi in range(N_EXPERT_GROUPS)])
              for j in range(EXPERTS_PER_GROUP)]
    sc_in = [_select(grp, [sc_rows[gi * EXPERTS_PER_GROUP + j] for gi in range(N_EXPERT_GROUPS)])
             for j in range(EXPERTS_PER_GROUP)]
    i1, _ = _first_argmax(sel_in)
    neg = jnp.full(sel_in[0].shape, -jnp.inf, F32)
    i2, _ = _first_argmax([jnp.where(i1 == j, neg, sel_in[j]) for j in range(EXPERTS_PER_GROUP)])
    s1 = _select(i1, sc_in)
    s2 = _select(i2, sc_in)
    tot = s1 + s2
    e1 = grp * EXPERTS_PER_GROUP + i1
    e2 = grp * EXPERTS_PER_GROUP + i2
    eid_ref[0:1, :] = e1
    eid_ref[1:2, :] = e2
    wts_ref[0:1, :] = s1 / tot
    wts_ref[1:2, :] = s2 / tot

    eio = lax.broadcasted_iota(I32, (N_EXPERTS, tm), 0)
    oh1 = jnp.where(eio == e1, 1.0, 0.0).astype(F32)
    oh2 = jnp.where(eio == e2, 1.0, 0.0).astype(F32)
    both = oh1 + oh2
    srow = lax.broadcasted_iota(I32, (tm, tm), 0)
    scol = lax.broadcasted_iota(I32, (tm, tm), 1)
    before = jnp.where(srow < scol, 1.0, 0.0).astype(BF16)
    cum = jnp.dot(both.astype(BF16), before, preferred_element_type=F32)
    pos = base_ref[:, 0:1] + cum
    rank_ref[0:1, :] = jnp.sum(oh1 * pos, axis=0, keepdims=True).astype(I32)
    rank_ref[1:2, :] = jnp.sum(oh2 * pos, axis=0, keepdims=True).astype(I32)
    base_ref[...] = base_ref[...] + jnp.sum(both, axis=1, keepdims=True)
    cnt_ref[...] = base_ref[...].astype(I32)


def _outproj(yn, y1g, gates, x, w_ssm_out, w_out, ln_g3, ln_b3, wr_t, rbias, layer, tm=512):
    t = x.shape[0]
    tm = min(tm, t)
    tile = lambda width: pl.BlockSpec((tm, width), lambda m: (m, 0))
    vec = pl.BlockSpec((None, 1, D_MODEL), lambda m: (layer, 0, 0))
    pair = pl.BlockSpec((2, tm), lambda m: (0, m))
    return pl.pallas_call(
        functools.partial(_outproj_kernel, tm=tm),
        grid=(t // tm,),
        in_specs=[tile(D_INNER), tile(D_MODEL),
                  pl.BlockSpec((tm, D_MODEL), lambda m: (m, 1)),
                  tile(D_MODEL),
                  pl.BlockSpec((None, D_INNER, D_MODEL), lambda m: (layer, 0, 0)),
                  pl.BlockSpec((None, D_MODEL, D_MODEL), lambda m: (layer, 0, 0)),
                  vec, vec,
                  pl.BlockSpec((N_EXPERTS, D_MODEL), lambda m: (0, 0)),
                  pl.BlockSpec((N_EXPERTS, 1), lambda m: (0, 0))],
        out_specs=[tile(D_MODEL), tile(D_MODEL), pair, pair, pair,
                   pl.BlockSpec((N_EXPERTS, LANES), lambda m: (0, 0))],
        out_shape=[jax.ShapeDtypeStruct((t, D_MODEL), F32),
                   jax.ShapeDtypeStruct((t, D_MODEL), BF16),
                   jax.ShapeDtypeStruct((2, t), I32),
                   jax.ShapeDtypeStruct((2, t), I32),
                   jax.ShapeDtypeStruct((2, t), F32),
                   jax.ShapeDtypeStruct((N_EXPERTS, LANES), I32)],
        scratch_shapes=[pltpu.VMEM((D_INNER, D_MODEL), BF16),
                        pltpu.VMEM((D_MODEL, D_MODEL), BF16),
                        pltpu.VMEM((N_EXPERTS, LANES), F32)],
        compiler_params=_params(1),
        name="outproj_ln_router",
    )(yn, y1g, gates, x, w_ssm_out, w_out, ln_g3, ln_b3, wr_t, rbias)


def _row_copy(src_ref, src_row, dst_ref, dst_row, sem):
    return pltpu.make_async_copy(src_ref.at[pl.ds(src_row, 1)], dst_ref.at[pl.ds(dst_row, 1)], sem)


def _dispatch_kernel(fill_ref, dest_ref, x_hbm, xs_hbm, zeros_ref, sem, blk_sem, *, tm, n_blk):
    t0 = pl.program_id(0) * tm

    def zero_row(row):
        return _row_copy(zeros_ref, 0, xs_hbm, row, sem)

    def zero_block(b):
        return pltpu.make_async_copy(
            zeros_ref, xs_hbm.at[pl.ds(pl.multiple_of(b * MOE_ROWS, MOE_ROWS), MOE_ROWS)], blk_sem)

    @pl.when(pl.program_id(0) == 0)
    def _():
        zeros_ref[...] = jnp.zeros(zeros_ref.shape, F32)
        n_active = fill_ref[2 * N_EXPERTS]
        for e in range(N_EXPERTS):
            pad_start = fill_ref[e]
            lax.fori_loop(0, fill_ref[N_EXPERTS + e],
                          lambda j, c: (zero_row(pad_start + j).start(), c)[1], 0)
        lax.fori_loop(n_active, n_blk, lambda b, c: (zero_block(b).start(), c)[1], 0)
        lax.fori_loop(0, fill_ref[2 * N_EXPERTS + 1], lambda j, c: (zero_row(0).wait(), c)[1], 0)
        lax.fori_loop(n_active, n_blk, lambda b, c: (zero_block(b).wait(), c)[1], 0)

    def issue(j, carry):
        _row_copy(x_hbm, t0 + j, xs_hbm, dest_ref[0, 0, 2 * j], sem).start()
        _row_copy(x_hbm, t0 + j, xs_hbm, dest_ref[0, 0, 2 * j + 1], sem).start()
        return carry

    lax.fori_loop(0, tm, issue, 0)

    def drain(j, carry):
        _row_copy(x_hbm, 0, xs_hbm, 0, sem).wait()
        return carry

    lax.fori_loop(0, 2 * tm, drain, 0)


def _dispatch(fill, dest_tiles, x1, n_blk, tm):
    t = x1.shape[0]
    grid_spec = pltpu.PrefetchScalarGridSpec(
        num_scalar_prefetch=1,
        grid=(t // tm,),
        in_specs=[pl.BlockSpec((1, 1, 2 * tm), lambda m, fill: (m, 0, 0), memory_space=pltpu.SMEM),
                  pl.BlockSpec(memory_space=pl.ANY)],
        out_specs=pl.BlockSpec(memory_space=pl.ANY),
        scratch_shapes=[pltpu.VMEM((MOE_ROWS, D_MODEL), F32),
                        pltpu.SemaphoreType.DMA(()), pltpu.SemaphoreType.DMA(())],
    )
    return pl.pallas_call(
        functools.partial(_dispatch_kernel, tm=tm, n_blk=n_blk),
        grid_spec=grid_spec,
        out_shape=jax.ShapeDtypeStruct((n_blk * MOE_ROWS, D_MODEL), F32),
        compiler_params=_params(1),
        name="moe_dispatch",
    )(fill, dest_tiles, x1)


def _expert_kernel(src_ref, exp_ref, nvalid_ref, xs_ref, wg_ref, wu_ref, wd_ref, y_ref,
                   wgu_bf, wd_bf):
    i = pl.program_id(0)
    nvalid = nvalid_ref[i]
    changed = (i == 0) | (exp_ref[i] != exp_ref[jnp.maximum(i - 1, 0)])

    @pl.when((nvalid > 0) & changed)
    def _():
        wgu_bf[:, :D_EXPERT] = wg_ref[...].astype(BF16)
        wgu_bf[:, D_EXPERT:] = wu_ref[...].astype(BF16)
        wd_bf[...] = wd_ref[...].astype(BF16)

    @pl.when(nvalid > 0)
    def _():
        rows = lax.broadcasted_iota(I32, (MOE_ROWS, 1), 0)
        x = jnp.where(rows < nvalid, xs_ref[...], 0.0).astype(BF16)
        gu = jnp.dot(x, wgu_bf[...], preferred_element_type=F32)
        hg = gu[:, :D_EXPERT]
        h = hg * _sigmoid(hg) * gu[:, D_EXPERT:]
        y_ref[...] = jnp.dot(h.astype(BF16), wd_bf[...], preferred_element_type=F32)

    @pl.when(nvalid == 0)
    def _():
        y_ref[...] = jnp.zeros(y_ref.shape, F32)


def _experts(blk_src, blk_exp, blk_nvalid, xs, wg, wu, wd, layer):
    n_rows = xs.shape[0]
    n_blk = n_rows // MOE_ROWS
    grid_spec = pltpu.PrefetchScalarGridSpec(
        num_scalar_prefetch=3,
        grid=(n_blk,),
        in_specs=[pl.BlockSpec((MOE_ROWS, D_MODEL), lambda i, src, exp, nv: (src[i], 0)),
                  pl.BlockSpec((None, None, D_MODEL, D_EXPERT),
                               lambda i, src, exp, nv: (layer, exp[i], 0, 0)),
                  pl.BlockSpec((None, None, D_MODEL, D_EXPERT),
                               lambda i, src, exp, nv: (layer, exp[i], 0, 0)),
                  pl.BlockSpec((None, None, D_EXPERT, D_MODEL),
                               lambda i, src, exp, nv: (layer, exp[i], 0, 0))],
        out_specs=pl.BlockSpec((MOE_ROWS, D_MODEL), lambda i, src, exp, nv: (i, 0)),
        scratch_shapes=[pltpu.VMEM((D_MODEL, 2 * D_EXPERT), BF16),
                        pltpu.VMEM((D_EXPERT, D_MODEL), BF16)],
    )
    return pl.pallas_call(
        _expert_kernel,
        grid_spec=grid_spec,
        out_shape=jax.ShapeDtypeStruct((n_rows, D_MODEL), F32),
        compiler_params=_params(1),
        name="moe_experts",
    )(blk_src, blk_exp, blk_nvalid, xs, wg, wu, wd)


def _moe_plan(eid, rank, counts, t):
    cnt = counts[:, 0]
    nblk_e = (cnt + MOE_ROWS - 1) // MOE_ROWS
    blk_end = jnp.cumsum(nblk_e)
    blk_start = blk_end - nblk_e
    n_active = blk_end[N_EXPERTS - 1]
    dest = (blk_start * MOE_ROWS)[eid] + rank
    n_blk = (2 * t) // MOE_ROWS + N_EXPERTS
    ids = jnp.arange(n_blk, dtype=I32)
    src = jnp.minimum(ids, n_active - 1)
    exp = jnp.minimum(jnp.searchsorted(blk_end, src, side="right"), N_EXPERTS - 1).astype(I32)
    left = cnt[exp] - (src - blk_start[exp]) * MOE_ROWS
    nvalid = jnp.where(ids < n_active, jnp.clip(left, 0, MOE_ROWS), 0)
    pad_len = nblk_e * MOE_ROWS - cnt
    fill = jnp.concatenate([blk_start * MOE_ROWS + cnt, pad_len,
                            jnp.stack([n_active, jnp.sum(pad_len)])]).astype(I32)
    return dest.astype(I32), src.astype(I32), exp, nvalid.astype(I32), fill, n_blk


def _final_kernel(dest_ref, x1_ref, x1b_ref, p_ref, wt_ref, wpu_ref, wpg_ref, bpg_ref, lg_ref, lb_ref,
                  y_hbm, x2_ref, x2b_ref, g0_ref, g1_ref, wpu_bf, wpg_bf, sem, *, tm):
    @pl.when(pl.program_id(0) == 0)
    def _():
        wpu_bf[...] = wpu_ref[...].astype(BF16)
        wpg_bf[...] = wpg_ref[...].astype(BF16)

    def issue(j, carry):
        _row_copy(y_hbm, dest_ref[0, 0, 2 * j], g0_ref, j, sem).start()
        _row_copy(y_hbm, dest_ref[0, 0, 2 * j + 1], g1_ref, j, sem).start()
        return carry

    lax.fori_loop(0, tm, issue, 0)

    up = jnp.dot(p_ref[...].astype(BF16), wpu_bf[...], preferred_element_type=F32)
    gate = _sigmoid(jnp.dot(x1b_ref[...], wpg_bf[...], preferred_element_type=F32) + bpg_ref[...])
    resid = ALPHA * x1_ref[...] + up * gate

    def drain(j, carry):
        _row_copy(y_hbm, 0, g0_ref, 0, sem).wait()
        return carry

    lax.fori_loop(0, 2 * tm, drain, 0)

    wt = wt_ref[...]
    moe = wt[:, 0:1] * g0_ref[...] + wt[:, 1:2] * g1_ref[...]
    x2 = _layer_norm(resid + moe, lg_ref[...], lb_ref[...])
    x2_ref[...] = x2
    x2b_ref[...] = x2.astype(BF16)


def _final(dest_tiles, x1, x1b, p, wt_tok, w_ple_up, w_ple_gate, b_pg3, ln_g3, ln_b3, y_rows, layer, tm):
    t = x1.shape[0]
    tile = lambda width: pl.BlockSpec((tm, width), lambda m: (m, 0))
    vec = pl.BlockSpec((None, 1, D_MODEL), lambda m: (layer, 0, 0))
    return pl.pallas_call(
        functools.partial(_final_kernel, tm=tm),
        grid=(t // tm,),
        in_specs=[pl.BlockSpec((1, 1, 2 * tm), lambda m: (m, 0, 0), memory_space=pltpu.SMEM),
                  tile(D_MODEL), tile(D_MODEL),
                  pl.BlockSpec((None, tm, PLE_DIM), lambda m: (layer, m, 0)),
                  tile(2),
                  pl.BlockSpec((None, PLE_DIM, D_MODEL), lambda m: (layer, 0, 0)),
                  pl.BlockSpec((None, D_MODEL, D_MODEL), lambda m: (layer, 0, 0)),
                  vec, vec, vec,
                  pl.BlockSpec(memory_space=pl.ANY)],
        out_specs=[tile(D_MODEL), tile(D_MODEL)],
        out_shape=[jax.ShapeDtypeStruct((t, D_MODEL), F32),
                   jax.ShapeDtypeStruct((t, D_MODEL), BF16)],
        scratch_shapes=[pltpu.VMEM((tm, D_MODEL), F32), pltpu.VMEM((tm, D_MODEL), F32),
                        pltpu.VMEM((PLE_DIM, D_MODEL), BF16), pltpu.VMEM((D_MODEL, D_MODEL), BF16),
                        pltpu.SemaphoreType.DMA(())],
        compiler_params=_params(1),
        name="combine_ple_ln",
    )(dest_tiles, x1, x1b, p, wt_tok, w_ple_up, w_ple_gate, b_pg3, ln_g3, ln_b3, y_rows)


def kernel(x, p, w_in, b_glu, b_branch_gate, conv_w, conv_b, conv_ln_g, conv_ln_b, w_conv_out,
           ssm_conv_w, ssm_conv_b, dt_bias, a_log, d_skip, ssm_norm_g, w_ssm_out, w_out,
           ln1_g, ln1_b, w_router, router_bias, w_exp_gate, w_exp_up, w_exp_down,
           w_ple_up, w_ple_gate, b_ple_gate, ln2_g, ln2_b):
    bsz, seqlen, d = x.shape
    depth = w_in.shape[0]
    t = bsz * seqlen
    tm_rows = min(256, t)

    row3 = lambda a: a.reshape(a.shape[0], 1, a.shape[1])
    pad_lanes = lambda a: jnp.pad(a, ((0, 0), (0, LANES - a.shape[1])))
    b_glu3, b_gate3 = row3(b_glu), row3(b_branch_gate)
    conv_b3, cln_g3, cln_b3 = row3(conv_b), row3(conv_ln_g), row3(conv_ln_b)
    ssm_conv_b3, ng3 = row3(ssm_conv_b), row3(ssm_norm_g)
    dtb3, alog3 = row3(pad_lanes(dt_bias)), row3(pad_lanes(a_log))
    dskip3 = row3(jnp.repeat(d_skip, SSM_HEAD_DIM, axis=1))
    ln1_g3, ln1_b3, ln2_g3, ln2_b3 = row3(ln1_g), row3(ln1_b), row3(ln2_g), row3(ln2_b)
    b_pg3 = row3(b_ple_gate)
    w_dt = jnp.pad(w_in[:, :, OFF_DT:OFF_DT + SSM_HEADS], ((0, 0), (0, 0), (0, LANES - SSM_HEADS)))
    w_gate = w_in[:, :, OFF_GATE:]
    wr_t = w_router.T
    rbias = router_bias.reshape(N_EXPERTS, 1)
    p2 = p.reshape(depth, t, PLE_DIM)

    xf = x.reshape(t, d)
    xb = xf.astype(BF16)
    for i in range(depth):
        c = _glu(xb, w_in, b_glu3, i)
        zs = _matmul(xb, w_in, lambda n: (i, 0, OFF_Z // 1024 + n), None, None,
                     D_INNER, 1024, "silu", BF16)
        xbc = _matmul(xb, w_in, lambda n: (i, 0, OFF_XBC // 1024 + n), None, None,
                      D_XBC, 1024, "none", BF16)
        dt = _matmul(xb, w_dt, lambda n: (i, 0, n), dtb3, lambda n: (i, 0, n),
                     LANES, LANES, "softplus", F32)
        gates = _matmul(xb, w_gate, lambda n: (i, 0, n), b_gate3, lambda n: (i, 0, n),
                        2 * D_MODEL, 1024, "sigmoid", BF16)
        y1g = _convbranch(c, gates, conv_w, conv_b3, cln_g3, cln_b3, w_conv_out, i, bsz, seqlen)
        yn = _ssd(xbc, dt, zs, ssm_conv_w, ssm_conv_b3, alog3, dskip3, ng3, i, bsz, seqlen)
        x1, x1b, eid, rank, wts, counts = _outproj(yn, y1g, gates, xf, w_ssm_out, w_out,
                                                   ln1_g3, ln1_b3, wr_t, rbias, i)
        dest, blk_src, blk_exp, blk_nvalid, fill, n_blk = _moe_plan(eid, rank, counts, t)
        dest_tiles = dest.T.reshape(t // tm_rows, 1, 2 * tm_rows)
        xs = _dispatch(fill, dest_tiles, x1, n_blk, tm_rows)
        y_rows = _experts(blk_src, blk_exp, blk_nvalid, xs, w_exp_gate, w_exp_up, w_exp_down, i)
        xf, xb = _final(dest_tiles, x1, x1b, p2, wts.T, w_ple_up, w_ple_gate, b_pg3,
                        ln2_g3, ln2_b3, y_rows, i, tm_rows)
    return xf.reshape(bsz, seqlen, d)
```

```python
import functools

import jax
import jax.numpy as jnp
from jax import lax
from jax.experimental import pallas as pl
from jax.experimental.pallas import tpu as pltpu

F32 = jnp.float32
BF16 = jnp.bfloat16
I32 = jnp.int32

D_MODEL = 1024
D_CONV = 1024
CONV_WIDTH = 31
D_INNER = 2048
SSM_HEAD_DIM = 64
SSM_HEADS = 32
SSM_GROUPS = 8
HEADS_PER_GROUP = 4
D_STATE = 128
SSM_CONV_WIDTH = 4
D_XBC = D_INNER + 2 * SSM_GROUPS * D_STATE
GROUP_CH = HEADS_PER_GROUP * SSM_HEAD_DIM
N_EXPERTS = 16
N_EXPERT_GROUPS = 4
EXPERTS_PER_GROUP = 4
D_EXPERT = 512
PLE_DIM = 256
DEPTH = 4
ALPHA = (2.0 * DEPTH) ** 0.25
LN_EPS = 1e-5
RMS_EPS = 1e-5

LANES = 128
CONV_HALO = 32
CONV_RC = 64
CONV_CW = 256
SSM_HALO = 8
SSD_Q = 128
MOE_ROWS = 512
VMEM_LIMIT = 48 * 1024 * 1024

OFF_GLU = 0
OFF_Z = 2 * D_CONV
OFF_XBC = OFF_Z + D_INNER
OFF_DT = OFF_XBC + D_XBC
OFF_GATE = OFF_DT + SSM_HEADS


def _sigmoid(x):
    return 1.0 / (1.0 + jnp.exp(-x))


def _layer_norm(x, g, b):
    mu = jnp.mean(x, axis=-1, keepdims=True)
    xc = x - mu
    var = jnp.mean(xc * xc, axis=-1, keepdims=True)
    return xc * lax.rsqrt(var + LN_EPS) * g + b


def _params(n_axes):
    return pltpu.CompilerParams(dimension_semantics=("arbitrary",) * n_axes,
                                vmem_limit_bytes=VMEM_LIMIT)


def _mm_kernel(x_ref, w_ref, *rest, act, has_bias):
    if has_bias:
        b_ref, o_ref, wbf_ref = rest
    else:
        o_ref, wbf_ref = rest

    @pl.when(pl.program_id(1) == 0)
    def _():
        wbf_ref[...] = w_ref[...].astype(BF16)

    acc = jnp.dot(x_ref[...], wbf_ref[...], preferred_element_type=F32)
    if has_bias:
        acc = acc + b_ref[...]
    if act == "sigmoid":
        acc = _sigmoid(acc)
    elif act == "silu":
        acc = acc * _sigmoid(acc)
    elif act == "softplus":
        acc = jnp.maximum(acc, 0.0) + jnp.log1p(jnp.exp(-jnp.abs(acc)))
    o_ref[...] = acc.astype(o_ref.dtype)


def _matmul(x, w, w_index, bias, b_index, n_out, tn, act, out_dtype):
    t, k = x.shape
    tm = min(1024, t)
    w_block = (None,) * (w.ndim - 2) + (k, tn)
    in_specs = [pl.BlockSpec((tm, k), lambda n, m: (m, 0)),
                pl.BlockSpec(w_block, lambda n, m: w_index(n))]
    args = [x, w]
    if bias is not None:
        b_block = (None,) * (bias.ndim - 2) + (1, tn)
        in_specs.append(pl.BlockSpec(b_block, lambda n, m: b_index(n)))
        args.append(bias)
    return pl.pallas_call(
        functools.partial(_mm_kernel, act=act, has_bias=bias is not None),
        grid=(n_out // tn, t // tm),
        in_specs=in_specs,
        out_specs=pl.BlockSpec((tm, tn), lambda n, m: (m, n)),
        out_shape=jax.ShapeDtypeStruct((t, n_out), out_dtype),
        scratch_shapes=[pltpu.VMEM((k, tn), BF16)],
        compiler_params=_params(2),
        name="inproj_" + act,
    )(*args)


def _tail_weights_kernel(wa_ref, wb_ref, wgate_ref, wdt_ref):
    a = wa_ref[...]
    b = wb_ref[...]
    lane = lax.broadcasted_iota(I32, (D_MODEL, LANES), 1)
    wdt_ref[...] = jnp.where(lane < SSM_HEADS, a[:, :LANES], 0.0).astype(BF16)
    n = 2 * D_MODEL
    shifted = pltpu.roll(a, n - SSM_HEADS, axis=1)
    tail = pltpu.roll(b, LANES - SSM_HEADS, axis=1)
    wgate_ref[:, :n - LANES] = shifted[:, :n - LANES].astype(BF16)
    wgate_ref[:, n - LANES:] = jnp.where(lane < LANES - SSM_HEADS, shifted[:, n - LANES:], tail).astype(BF16)


def _tail_weights(w_in):
    depth, k, _ = w_in.shape
    n = 2 * D_MODEL
    return pl.pallas_call(
        _tail_weights_kernel,
        grid=(depth,),
        in_specs=[pl.BlockSpec((None, k, n), lambda l: (l, 0, OFF_DT // n)),
                  pl.BlockSpec((None, k, LANES), lambda l: (l, 0, (OFF_DT + n) // LANES))],
        out_specs=[pl.BlockSpec((None, k, n), lambda l: (l, 0, 0)),
                   pl.BlockSpec((None, k, LANES), lambda l: (l, 0, 0))],
        out_shape=[jax.ShapeDtypeStruct((depth, k, n), BF16),
                   jax.ShapeDtypeStruct((depth, k, LANES), BF16)],
        compiler_params=_params(1),
        name="tail_weights",
    )(w_in, w_in)


def _glu_kernel(x_ref, wa_ref, wg_ref, ba_ref, bg_ref, o_ref, wa_bf, wg_bf):
    @pl.when(pl.program_id(1) == 0)
    def _():
        wa_bf[...] = wa_ref[...].astype(BF16)
        wg_bf[...] = wg_ref[...].astype(BF16)

    x = x_ref[...]
    a = jnp.dot(x, wa_bf[...], preferred_element_type=F32) + ba_ref[...]
    g = jnp.dot(x, wg_bf[...], preferred_element_type=F32) + bg_ref[...]
    o_ref[...] = (a * _sigmoid(g)).astype(o_ref.dtype)


def _glu(x, w_in, b_glu3, layer, tn=512):
    t, k = x.shape
    tm = min(1024, t)
    half = D_CONV // tn
    return pl.pallas_call(
        _glu_kernel,
        grid=(half, t // tm),
        in_specs=[pl.BlockSpec((tm, k), lambda n, m: (m, 0)),
                  pl.BlockSpec((None, k, tn), lambda n, m: (layer, 0, n)),
                  pl.BlockSpec((None, k, tn), lambda n, m: (layer, 0, n + half)),
                  pl.BlockSpec((None, 1, tn), lambda n, m: (layer, 0, n)),
                  pl.BlockSpec((None, 1, tn), lambda n, m: (layer, 0, n + half))],
        out_specs=pl.BlockSpec((tm, tn), lambda n, m: (m, n)),
        out_shape=jax.ShapeDtypeStruct((t, D_CONV), BF16),
        scratch_shapes=[pltpu.VMEM((k, tn), BF16), pltpu.VMEM((k, tn), BF16)],
        compiler_params=_params(2),
        name="inproj_glu",
    )(x, w_in, w_in, b_glu3, b_glu3)


def _convbranch_kernel(c_ref, cw_ref, cb_ref, lg_ref, lb_ref, w_ref, gate_ref, o_ref,
                       ext_ref, sh_ref, conv_ref, wbf_ref, *, tl):
    first = (pl.program_id(0) == 0) & (pl.program_id(1) == 0)

    @pl.when(first)
    def _():
        wbf_ref[...] = w_ref[...].astype(BF16)

    @pl.when(pl.program_id(1) == 0)
    def _():
        ext_ref[0:CONV_HALO, :] = jnp.zeros((CONV_HALO, D_CONV), F32)

    @pl.when(pl.program_id(1) > 0)
    def _():
        ext_ref[0:CONV_HALO, :] = ext_ref[tl:tl + CONV_HALO, :]

    ext_ref[CONV_HALO:CONV_HALO + tl, :] = c_ref[...].astype(F32)

    sh_rows = tl + CONV_HALO - 8
    for s in range(1, 8):
        sh_ref[s - 1] = ext_ref[s:s + sh_rows, :]

    base = CONV_HALO - (CONV_WIDTH - 1)

    def conv_rows(rc, carry):
        r0 = pl.multiple_of(rc * CONV_RC, CONV_RC)
        for lo in range(0, D_CONV, CONV_CW):
            acc = jnp.broadcast_to(cb_ref[:, lo:lo + CONV_CW], (CONV_RC, CONV_CW))
            for k in range(CONV_WIDTH):
                a, s = divmod(base + k, 8)
                rows = pl.ds(r0 + 8 * a, CONV_RC)
                if s == 0:
                    xk = ext_ref[rows, lo:lo + CONV_CW]
                else:
                    xk = sh_ref[s - 1, rows, lo:lo + CONV_CW]
                acc = acc + cw_ref[k:k + 1, lo:lo + CONV_CW] * xk
            conv_ref[pl.ds(r0, CONV_RC), lo:lo + CONV_CW] = acc
        return carry

    lax.fori_loop(0, tl // CONV_RC, conv_rows, 0)
    h = _layer_norm(conv_ref[...], lg_ref[...], lb_ref[...])
    h = h * _sigmoid(h)
    y = jnp.dot(h.astype(BF16), wbf_ref[...], preferred_element_type=F32)
    o_ref[...] = (y * gate_ref[...].astype(F32)).astype(o_ref.dtype)


def _convbranch(c, gates, conv_w, conv_b3, ln_g3, ln_b3, w_conv_out, layer, bsz, seqlen, tl=256):
    t = c.shape[0]
    nl = seqlen // tl
    vec = pl.BlockSpec((None, 1, D_CONV), lambda b, i: (layer, 0, 0))
    return pl.pallas_call(
        functools.partial(_convbranch_kernel, tl=tl),
        grid=(bsz, nl),
        in_specs=[pl.BlockSpec((tl, D_CONV), lambda b, i: (b * nl + i, 0)),
                  pl.BlockSpec((None, CONV_WIDTH, D_CONV), lambda b, i: (layer, 0, 0)),
                  vec, vec, vec,
                  pl.BlockSpec((None, D_CONV, D_MODEL), lambda b, i: (layer, 0, 0)),
                  pl.BlockSpec((tl, D_MODEL), lambda b, i: (b * nl + i, 0))],
        out_specs=pl.BlockSpec((tl, D_MODEL), lambda b, i: (b * nl + i, 0)),
        out_shape=jax.ShapeDtypeStruct((t, D_MODEL), BF16),
        scratch_shapes=[pltpu.VMEM((tl + CONV_HALO, D_CONV), F32),
                        pltpu.VMEM((7, tl + CONV_HALO - 8, D_CONV), F32),
                        pltpu.VMEM((tl, D_CONV), F32),
                        pltpu.VMEM((D_CONV, D_MODEL), BF16)],
        compiler_params=_params(2),
        name="conv_module",
    )(c, conv_w, conv_b3, ln_g3, ln_b3, w_conv_out, gates)


def _ssd_kernel(xbc_ref, dt_ref, zs_ref, cw_ref, cb_ref, alog_ref, dskip_ref, ng_ref, o_ref,
                ext_ref, state_ref, *, q):
    @pl.when(pl.program_id(1) == 0)
    def _():
        ext_ref[0:SSM_HALO, :] = jnp.zeros((SSM_HALO, D_XBC), F32)
        state_ref[...] = jnp.zeros(state_ref.shape, F32)

    @pl.when(pl.program_id(1) > 0)
    def _():
        ext_ref[0:SSM_HALO, :] = ext_ref[q:q + SSM_HALO, :]

    ext_ref[SSM_HALO:SSM_HALO + q, :] = xbc_ref[...].astype(F32)

    base = SSM_HALO - (SSM_CONV_WIDTH - 1)
    conv = jnp.zeros((q, D_XBC), F32) + cb_ref[...]
    for k in range(SSM_CONV_WIDTH):
        conv = conv + cw_ref[k:k + 1, :] * ext_ref[base + k:base + k + q, :]
    xbc = conv * _sigmoid(conv)
    xs = xbc[:, :D_INNER]
    bm = xbc[:, D_INNER:D_INNER + SSM_GROUPS * D_STATE]
    cm = xbc[:, D_INNER + SSM_GROUPS * D_STATE:]

    dt = dt_ref[...]
    adt = dt * (-jnp.exp(alog_ref[...]))
    row = lax.broadcasted_iota(I32, (q, q), 0)
    col = lax.broadcasted_iota(I32, (q, q), 1)
    causal = row >= col
    tril = jnp.where(causal, 1.0, 0.0).astype(F32)
    acs = jnp.dot(tril, adt, preferred_element_type=F32, precision=lax.Precision.HIGHEST)
    acs_t = acs.T
    last = acs[q - 1:q, :]
    exp_acs = jnp.exp(acs)
    decay_to_end = jnp.exp(last - acs)
    chunk_decay = jnp.exp(last)

    for g in range(SSM_GROUPS):
        cg = cm[:, g * D_STATE:(g + 1) * D_STATE].astype(BF16)
        bg = bm[:, g * D_STATE:(g + 1) * D_STATE]
        cb = lax.dot_general(cg, bg.astype(BF16), (((1,), (1,)), ((), ())),
                             preferred_element_type=F32)
        s_prev = state_ref[g]
        y_off = jnp.dot(cg, s_prev.astype(BF16), preferred_element_type=F32)
        ys, xws, decs = [], [], []
        for r in range(HEADS_PER_GROUP):
            h = g * HEADS_PER_GROUP + r
            a_col = acs[:, h:h + 1]
            a_row = acs_t[h:h + 1, :]
            lmat = jnp.exp(jnp.where(causal, a_col - a_row, -jnp.inf))
            m = (cb * lmat).astype(BF16)
            xh = xs[:, h * SSM_HEAD_DIM:(h + 1) * SSM_HEAD_DIM]
            xdt = xh * dt[:, h:h + 1]
            y = jnp.dot(m, xdt.astype(BF16), preferred_element_type=F32)
            y = y + y_off[:, r * SSM_HEAD_DIM:(r + 1) * SSM_HEAD_DIM] * exp_acs[:, h:h + 1]
            y = y + xh * dskip_ref[:, h * SSM_HEAD_DIM:(h + 1) * SSM_HEAD_DIM]
            ys.append(y)
            xws.append(xdt * decay_to_end[:, h:h + 1])
            decs.append(jnp.broadcast_to(chunk_decay[:, h:h + 1], (1, SSM_HEAD_DIM)))
        yg = jnp.concatenate(ys, axis=1)
        xw = jnp.concatenate(xws, axis=1).astype(BF16)
        dec = jnp.concatenate(decs, axis=1)
        state_ref[g] = s_prev * dec + jnp.dot(bg.T.astype(BF16), xw, preferred_element_type=F32)

        yz = yg * zs_ref[:, g * GROUP_CH:(g + 1) * GROUP_CH].astype(F32)
        ms = jnp.mean(yz * yz, axis=-1, keepdims=True)
        yn = yz * lax.rsqrt(ms + RMS_EPS) * ng_ref[:, g * GROUP_CH:(g + 1) * GROUP_CH]
        o_ref[:, g * GROUP_CH:(g + 1) * GROUP_CH] = yn.astype(o_ref.dtype)


def _ssd(xbc, dt, zs, ssm_conv_w, ssm_conv_b3, alog3, dskip3, ng3, layer, bsz, seqlen):
    t = xbc.shape[0]
    q = SSD_Q
    nq = seqlen // q
    tile = lambda width: pl.BlockSpec((q, width), lambda b, i: (b * nq + i, 0))
    vec = lambda width: pl.BlockSpec((None, 1, width), lambda b, i: (layer, 0, 0))
    return pl.pallas_call(
        functools.partial(_ssd_kernel, q=q),
        grid=(bsz, nq),
        in_specs=[tile(D_XBC), tile(LANES), tile(D_INNER),
                  pl.BlockSpec((None, SSM_CONV_WIDTH, D_XBC), lambda b, i: (layer, 0, 0)),
                  vec(D_XBC), vec(LANES), vec(D_INNER), vec(D_INNER)],
        out_specs=tile(D_INNER),
        out_shape=jax.ShapeDtypeStruct((t, D_INNER), BF16),
        scratch_shapes=[pltpu.VMEM((q + SSM_HALO, D_XBC), F32),
                        pltpu.VMEM((SSM_GROUPS, D_STATE, GROUP_CH), F32)],
        compiler_params=_params(2),
        name="ssd_mixer",
    )(xbc, dt, zs, ssm_conv_w, ssm_conv_b3, alog3, dskip3, ng3)


def _first_argmax(vals):
    best, idx = vals[0], jnp.zeros(vals[0].shape, I32)
    for j in range(1, len(vals)):
        gt = vals[j] > best
        idx = jnp.where(gt, j, idx)
        best = jnp.where(gt, vals[j], best)
    return idx, best


def _select(idx, vals):
    out = vals[len(vals) - 1]
    for j in range(len(vals) - 2, -1, -1):
        out = jnp.where(idx == j, vals[j], out)
    return out


def _outproj_kernel(yn_ref, y1_ref, gs_ref, x_ref, wso_ref, wo_ref, lg_ref, lb_ref, wr_ref, rb_ref,
                    x1_ref, x1b_ref, eid_ref, rank_ref, wts_ref, cnt_ref,
                    wso_bf, wo_bf, base_ref, *, tm):
    @pl.when(pl.program_id(0) == 0)
    def _():
        wso_bf[...] = wso_ref[...].astype(BF16)
        wo_bf[...] = wo_ref[...].astype(BF16)
        base_ref[...] = jnp.zeros(base_ref.shape, F32)

    y_ssm = jnp.dot(yn_ref[...], wso_bf[...], preferred_element_type=F32)
    merged = y1_ref[...].astype(F32) + gs_ref[...].astype(F32) * y_ssm
    mix = jnp.dot(merged.astype(BF16), wo_bf[...], preferred_element_type=F32)
    x1 = _layer_norm(ALPHA * x_ref[...] + mix, lg_ref[...], lb_ref[...])
    x1_ref[...] = x1
    x1b_ref[...] = x1.astype(BF16)

    logits = lax.dot_general(wr_ref[...], x1, (((1,), (1,)), ((), ())),
                             preferred_element_type=F32, precision=lax.Precision.HIGHEST)
    scores = _sigmoid(logits)
    sel = scores + rb_ref[...]
    sel_rows = [sel[e:e + 1, :] for e in range(N_EXPERTS)]
    sc_rows = [scores[e:e + 1, :] for e in range(N_EXPERTS)]

    gscores = []
    for gi in range(N_EXPERT_GROUPS):
        v = sel_rows[gi * EXPERTS_PER_GROUP:(gi + 1) * EXPERTS_PER_GROUP]
        best = None
        for a in range(EXPERTS_PER_GROUP):
            for b in range(a + 1, EXPERTS_PER_GROUP):
                s = v[a] + v[b]
                best = s if best is None else jnp.maximum(best, s)
        gscores.append(best)
    grp, _ = _first_argmax(gscores)

    sel_in = [_select(grp, [sel_rows[gi * EXPERTS_PER_GROUP + j] for gi in range(N_EXPERT_GROUPS)])
              for j in range(EXPERTS_PER_GROUP)]
    sc_in = [_select(grp, [sc_rows[gi * EXPERTS_PER_GROUP + j] for gi in range(N_EXPERT_GROUPS)])
             for j in range(EXPERTS_PER_GROUP)]
    i1, _ = _first_argmax(sel_in)
    neg = jnp.full(sel_in[0].shape, -jnp.inf, F32)
    i2, _ = _first_argmax([jnp.where(i1 == j, neg, sel_in[j]) for j in range(EXPERTS_PER_GROUP)])
    s1 = _select(i1, sc_in)
    s2 = _select(i2, sc_in)
    tot = s1 + s2
    e1 = grp * EXPERTS_PER_GROUP + i1
    e2 = grp * EXPERTS_PER_GROUP + i2
    eid_ref[0:1, :] = e1
    eid_ref[1:2, :] = e2
    wts_ref[0:1, :] = s1 / tot
    wts_ref[1:2, :] = s2 / tot

    eio = lax.broadcasted_iota(I32, (N_EXPERTS, tm), 0)
    oh1 = jnp.where(eio == e1, 1.0, 0.0).astype(F32)
    oh2 = jnp.where(eio == e2, 1.0, 0.0).astype(F32)
    both = oh1 + oh2
    srow = lax.broadcasted_iota(I32, (tm, tm), 0)
    scol = lax.broadcasted_iota(I32, (tm, tm), 1)
    before = jnp.where(srow < scol, 1.0, 0.0).astype(BF16)
    cum = jnp.dot(both.astype(BF16), before, preferred_element_type=F32)
    pos = base_ref[:, 0:1] + cum
    rank_ref[0:1, :] = jnp.sum(oh1 * pos, axis=0, keepdims=True).astype(I32)
    rank_ref[1:2, :] = jnp.sum(oh2 * pos, axis=0, keepdims=True).astype(I32)
    base_ref[...] = base_ref[...] + jnp.sum(both, axis=1, keepdims=True)
    cnt_ref[...] = base_ref[...].astype(I32)


def _outproj(yn, y1g, gates, x, w_ssm_out, w_out, ln_g3, ln_b3, wr_t, rbias, layer, tm=512):
    t = x.shape[0]
    tm = min(tm, t)
    tile = lambda width: pl.BlockSpec((tm, width), lambda m: (m, 0))
    vec = pl.BlockSpec((None, 1, D_MODEL), lambda m: (layer, 0, 0))
    pair = pl.BlockSpec((2, tm), lambda m: (0, m))
    return pl.pallas_call(
        functools.partial(_outproj_kernel, tm=tm),
        grid=(t // tm,),
        in_specs=[tile(D_INNER), tile(D_MODEL),
                  pl.BlockSpec((tm, D_MODEL), lambda m: (m, 1)),
                  tile(D_MODEL),
                  pl.BlockSpec((None, D_INNER, D_MODEL), lambda m: (layer, 0, 0)),
                  pl.BlockSpec((None, D_MODEL, D_MODEL), lambda m: (layer, 0, 0)),
                  vec, vec,
                  pl.BlockSpec((N_EXPERTS, D_MODEL), lambda m: (0, 0)),
                  pl.BlockSpec((N_EXPERTS, 1), lambda m: (0, 0))],
        out_specs=[tile(D_MODEL), tile(D_MODEL), pair, pair, pair,
                   pl.BlockSpec((N_EXPERTS, LANES), lambda m: (0, 0))],
        out_shape=[jax.ShapeDtypeStruct((t, D_MODEL), F32),
                   jax.ShapeDtypeStruct((t, D_MODEL), BF16),
                   jax.ShapeDtypeStruct((2, t), I32),
                   jax.ShapeDtypeStruct((2, t), I32),
                   jax.ShapeDtypeStruct((2, t), F32),
                   jax.ShapeDtypeStruct((N_EXPERTS, LANES), I32)],
        scratch_shapes=[pltpu.VMEM((D_INNER, D_MODEL), BF16),
                        pltpu.VMEM((D_MODEL, D_MODEL), BF16),
                        pltpu.VMEM((N_EXPERTS, LANES), F32)],
        compiler_params=_params(1),
        name="outproj_ln_router",
    )(yn, y1g, gates, x, w_ssm_out, w_out, ln_g3, ln_b3, wr_t, rbias)


def _row_copy(src_ref, src_row, dst_ref, dst_row, sem):
    return pltpu.make_async_copy(src_ref.at[pl.ds(src_row, 1)], dst_ref.at[pl.ds(dst_row, 1)], sem)


def _dispatch_kernel(fill_ref, dest_ref, x_ref, xs_hbm, zeros_ref, sem, blk_sem, *, tm, n_blk):
    def zero_row(row):
        return _row_copy(zeros_ref, 0, xs_hbm, row, sem)

    def zero_block(b):
        return pltpu.make_async_copy(
            zeros_ref, xs_hbm.at[pl.ds(pl.multiple_of(b * MOE_ROWS, MOE_ROWS), MOE_ROWS)], blk_sem)

    @pl.when(pl.program_id(0) == 0)
    def _():
        zeros_ref[...] = jnp.zeros(zeros_ref.shape, F32)
        n_active = fill_ref[2 * N_EXPERTS]
        for e in range(N_EXPERTS):
            pad_start = fill_ref[e]
            lax.fori_loop(0, fill_ref[N_EXPERTS + e],
                          lambda j, c: (zero_row(pad_start + j).start(), c)[1], 0)
        lax.fori_loop(n_active, n_blk, lambda b, c: (zero_block(b).start(), c)[1], 0)
        lax.fori_loop(0, fill_ref[2 * N_EXPERTS + 1], lambda j, c: (zero_row(0).wait(), c)[1], 0)
        lax.fori_loop(n_active, n_blk, lambda b, c: (zero_block(b).wait(), c)[1], 0)

    def issue(j, carry):
        _row_copy(x_ref, j, xs_hbm, dest_ref[0, 0, 2 * j], sem).start()
        _row_copy(x_ref, j, xs_hbm, dest_ref[0, 0, 2 * j + 1], sem).start()
        return carry

    lax.fori_loop(0, tm, issue, 0, unroll=8)
    for _ in range(2):
        pltpu.make_async_copy(x_ref, xs_hbm.at[pl.ds(0, tm)], sem).wait()


def _dispatch(fill, dest_tiles, x1, n_blk, tm):
    t = x1.shape[0]
    grid_spec = pltpu.PrefetchScalarGridSpec(
        num_scalar_prefetch=1,
        grid=(t // tm,),
        in_specs=[pl.BlockSpec((1, 1, 2 * tm), lambda m, fill: (m, 0, 0), memory_space=pltpu.SMEM),
                  pl.BlockSpec((tm, D_MODEL), lambda m, fill: (m, 0))],
        out_specs=pl.BlockSpec(memory_space=pl.ANY),
        scratch_shapes=[pltpu.VMEM((MOE_ROWS, D_MODEL), F32),
                        pltpu.SemaphoreType.DMA(()), pltpu.SemaphoreType.DMA(())],
    )
    return pl.pallas_call(
        functools.partial(_dispatch_kernel, tm=tm, n_blk=n_blk),
        grid_spec=grid_spec,
        out_shape=jax.ShapeDtypeStruct((n_blk * MOE_ROWS, D_MODEL), F32),
        compiler_params=_params(1),
        name="moe_dispatch",
    )(fill, dest_tiles, x1)


def _expert_kernel(src_ref, exp_ref, nvalid_ref, xs_ref, wg_ref, wu_ref, wd_ref, y_ref,
                   wgu_bf, wd_bf):
    i = pl.program_id(0)
    nvalid = nvalid_ref[i]
    changed = (i == 0) | (exp_ref[i] != exp_ref[jnp.maximum(i - 1, 0)])

    @pl.when((nvalid > 0) & changed)
    def _():
        wgu_bf[:, :D_EXPERT] = wg_ref[...].astype(BF16)
        wgu_bf[:, D_EXPERT:] = wu_ref[...].astype(BF16)
        wd_bf[...] = wd_ref[...].astype(BF16)

    @pl.when(nvalid > 0)
    def _():
        rows = lax.broadcasted_iota(I32, (MOE_ROWS, 1), 0)
        x = jnp.where(rows < nvalid, xs_ref[...], 0.0).astype(BF16)
        gu = jnp.dot(x, wgu_bf[...], preferred_element_type=F32)
        hg = gu[:, :D_EXPERT]
        h = hg * _sigmoid(hg) * gu[:, D_EXPERT:]
        y_ref[...] = jnp.dot(h.astype(BF16), wd_bf[...], preferred_element_type=F32)

    @pl.when(nvalid == 0)
    def _():
        y_ref[...] = jnp.zeros(y_ref.shape, F32)


def _experts(blk_src, blk_exp, blk_nvalid, xs, wg, wu, wd, layer):
    n_rows = xs.shape[0]
    n_blk = n_rows // MOE_ROWS
    grid_spec = pltpu.PrefetchScalarGridSpec(
        num_scalar_prefetch=3,
        grid=(n_blk,),
        in_specs=[pl.BlockSpec((MOE_ROWS, D_MODEL), lambda i, src, exp, nv: (src[i], 0)),
                  pl.BlockSpec((None, None, D_MODEL, D_EXPERT),
                               lambda i, src, exp, nv: (layer, exp[i], 0, 0)),
                  pl.BlockSpec((None, None, D_MODEL, D_EXPERT),
                               lambda i, src, exp, nv: (layer, exp[i], 0, 0)),
                  pl.BlockSpec((None, None, D_EXPERT, D_MODEL),
                               lambda i, src, exp, nv: (layer, exp[i], 0, 0))],
        out_specs=pl.BlockSpec((MOE_ROWS, D_MODEL), lambda i, src, exp, nv: (i, 0)),
        scratch_shapes=[pltpu.VMEM((D_MODEL, 2 * D_EXPERT), BF16),
                        pltpu.VMEM((D_EXPERT, D_MODEL), BF16)],
    )
    return pl.pallas_call(
        _expert_kernel,
        grid_spec=grid_spec,
        out_shape=jax.ShapeDtypeStruct((n_rows, D_MODEL), F32),
        compiler_params=_params(1),
        name="moe_experts",
    )(blk_src, blk_exp, blk_nvalid, xs, wg, wu, wd)


def _moe_plan(eid, rank, counts, t):
    cnt = counts[:, 0]
    nblk_e = (cnt + MOE_ROWS - 1) // MOE_ROWS
    blk_end = jnp.cumsum(nblk_e)
    blk_start = blk_end - nblk_e
    n_active = blk_end[N_EXPERTS - 1]
    experts = jnp.arange(N_EXPERTS, dtype=I32)
    row_start = blk_start * MOE_ROWS
    dest = rank + jnp.sum(jnp.where(eid[None] == experts[:, None, None],
                                    row_start[:, None, None], 0), axis=0)
    n_blk = (2 * t) // MOE_ROWS + N_EXPERTS
    ids = jnp.arange(n_blk, dtype=I32)
    src = jnp.minimum(ids, n_active - 1)
    exp = jnp.minimum(jnp.sum((blk_end[None, :] <= src[:, None]).astype(I32), axis=1), N_EXPERTS - 1)
    onehot = (exp[:, None] == experts[None, :]).astype(I32)
    pick = lambda table: jnp.sum(onehot * table[None, :], axis=1)
    left = pick(cnt) - (src - pick(blk_start)) * MOE_ROWS
    nvalid = jnp.where(ids < n_active, jnp.clip(left, 0, MOE_ROWS), 0)
    pad_len = nblk_e * MOE_ROWS - cnt
    fill = jnp.concatenate([blk_start * MOE_ROWS + cnt, pad_len,
                            jnp.stack([n_active, jnp.sum(pad_len)])]).astype(I32)
    return dest.astype(I32), src.astype(I32), exp, nvalid.astype(I32), fill, n_blk


def _final_kernel(dest_ref, dnext_ref, x1_ref, x1b_ref, p_ref, wt_ref, wpu_ref, wpg_ref, bpg_ref,
                  lg_ref, lb_ref, y_hbm, x2_ref, x2b_ref, g_ref, wpu_bf, wpg_bf, sems, *, tm, n_tiles):
    i = pl.program_id(0)
    slot = lax.rem(i, 2)

    def gather(d_ref, s):
        def issue(j, carry):
            _row_copy(y_hbm, d_ref[0, 0, 2 * j], g_ref.at[s, 0], j, sems.at[s]).start()
            _row_copy(y_hbm, d_ref[0, 0, 2 * j + 1], g_ref.at[s, 1], j, sems.at[s]).start()
            return carry

        lax.fori_loop(0, tm, issue, 0, unroll=8)

    @pl.when(i == 0)
    def _():
        wpu_bf[...] = wpu_ref[...].astype(BF16)
        wpg_bf[...] = wpg_ref[...].astype(BF16)
        gather(dest_ref, 0)

    @pl.when(i + 1 < n_tiles)
    def _():
        gather(dnext_ref, 1 - slot)

    up = jnp.dot(p_ref[...].astype(BF16), wpu_bf[...], preferred_element_type=F32)
    gate = _sigmoid(jnp.dot(x1b_ref[...], wpg_bf[...], preferred_element_type=F32) + bpg_ref[...])
    resid = ALPHA * x1_ref[...] + up * gate

    for k in range(2):
        pltpu.make_async_copy(y_hbm.at[pl.ds(0, tm)], g_ref.at[slot, k], sems.at[slot]).wait()

    wt = wt_ref[...]
    moe = wt[:, 0:1] * g_ref[slot, 0] + wt[:, 1:2] * g_ref[slot, 1]
    x2 = _layer_norm(resid + moe, lg_ref[...], lb_ref[...])
    x2_ref[...] = x2
    x2b_ref[...] = x2.astype(BF16)


def _final(dest_tiles, x1, x1b, p, wt_tok, w_ple_up, w_ple_gate, b_pg3, ln_g3, ln_b3, y_rows, layer, tm):
    t = x1.shape[0]
    n_tiles = t // tm
    tile = lambda width: pl.BlockSpec((tm, width), lambda m: (m, 0))
    vec = pl.BlockSpec((None, 1, D_MODEL), lambda m: (layer, 0, 0))
    return pl.pallas_call(
        functools.partial(_final_kernel, tm=tm, n_tiles=n_tiles),
        grid=(n_tiles,),
        in_specs=[pl.BlockSpec((1, 1, 2 * tm), lambda m: (m, 0, 0), memory_space=pltpu.SMEM),
                  pl.BlockSpec((1, 1, 2 * tm), lambda m: (jnp.minimum(m + 1, n_tiles - 1), 0, 0),
                               memory_space=pltpu.SMEM),
                  tile(D_MODEL), tile(D_MODEL),
                  pl.BlockSpec((None, tm, PLE_DIM), lambda m: (layer, m, 0)),
                  tile(2),
                  pl.BlockSpec((None, PLE_DIM, D_MODEL), lambda m: (layer, 0, 0)),
                  pl.BlockSpec((None, D_MODEL, D_MODEL), lambda m: (layer, 0, 0)),
                  vec, vec, vec,
                  pl.BlockSpec(memory_space=pl.ANY)],
        out_specs=[tile(D_MODEL), tile(D_MODEL)],
        out_shape=[jax.ShapeDtypeStruct((t, D_MODEL), F32),
                   jax.ShapeDtypeStruct((t, D_MODEL), BF16)],
        scratch_shapes=[pltpu.VMEM((2, 2, tm, D_MODEL), F32),
                        pltpu.VMEM((PLE_DIM, D_MODEL), BF16), pltpu.VMEM((D_MODEL, D_MODEL), BF16),
                        pltpu.SemaphoreType.DMA((2,))],
        compiler_params=_params(1),
        name="combine_ple_ln",
    )(dest_tiles, dest_tiles, x1, x1b, p, wt_tok, w_ple_up, w_ple_gate, b_pg3, ln_g3, ln_b3, y_rows)


def kernel(x, p, w_in, b_glu, b_branch_gate, conv_w, conv_b, conv_ln_g, conv_ln_b, w_conv_out,
           ssm_conv_w, ssm_conv_b, dt_bias, a_log, d_skip, ssm_norm_g, w_ssm_out, w_out,
           ln1_g, ln1_b, w_router, router_bias, w_exp_gate, w_exp_up, w_exp_down,
           w_ple_up, w_ple_gate, b_ple_gate, ln2_g, ln2_b):
    bsz, seqlen, d = x.shape
    depth = w_in.shape[0]
    t = bsz * seqlen
    tm_disp = min(512, t)
    tm_comb = min(256, t)

    row3 = lambda a: a.reshape(a.shape[0], 1, a.shape[1])
    pad_lanes = lambda a: jnp.pad(a, ((0, 0), (0, LANES - a.shape[1])))
    b_glu3, b_gate3 = row3(b_glu), row3(b_branch_gate)
    conv_b3, cln_g3, cln_b3 = row3(conv_b), row3(conv_ln_g), row3(conv_ln_b)
    ssm_conv_b3, ng3 = row3(ssm_conv_b), row3(ssm_norm_g)
    dtb3, alog3 = row3(pad_lanes(dt_bias)), row3(pad_lanes(a_log))
    dskip3 = row3(jnp.repeat(d_skip, SSM_HEAD_DIM, axis=1))
    ln1_g3, ln1_b3, ln2_g3, ln2_b3 = row3(ln1_g), row3(ln1_b), row3(ln2_g), row3(ln2_b)
    b_pg3 = row3(b_ple_gate)
    w_gate, w_dt = _tail_weights(w_in)
    wr_t = w_router.T
    rbias = router_bias.reshape(N_EXPERTS, 1)
    p2 = p.reshape(depth, t, PLE_DIM)

    xf = x.reshape(t, d)
    xb = xf.astype(BF16)
    for i in range(depth):
        c = _glu(xb, w_in, b_glu3, i)
        zs = _matmul(xb, w_in, lambda n: (i, 0, OFF_Z // 1024 + n), None, None,
                     D_INNER, 1024, "silu", BF16)
        xbc = _matmul(xb, w_in, lambda n: (i, 0, OFF_XBC // 1024 + n), None, None,
                      D_XBC, 1024, "none", BF16)
        dt = _matmul(xb, w_dt, lambda n: (i, 0, n), dtb3, lambda n: (i, 0, n),
                     LANES, LANES, "softplus", F32)
        gates = _matmul(xb, w_gate, lambda n: (i, 0, n), b_gate3, lambda n: (i, 0, n),
                        2 * D_MODEL, 1024, "sigmoid", BF16)
        y1g = _convbranch(c, gates, conv_w, conv_b3, cln_g3, cln_b3, w_conv_out, i, bsz, seqlen)
        yn = _ssd(xbc, dt, zs, ssm_conv_w, ssm_conv_b3, alog3, dskip3, ng3, i, bsz, seqlen)
        x1, x1b, eid, rank, wts, counts = _outproj(yn, y1g, gates, xf, w_ssm_out, w_out,
                                                   ln1_g3, ln1_b3, wr_t, rbias, i)
        dest, blk_src, blk_exp, blk_nvalid, fill, n_blk = _moe_plan(eid, rank, counts, t)
        dest_tok = dest.T
        xs = _dispatch(fill, dest_tok.reshape(t // tm_disp, 1, 2 * tm_disp), x1, n_blk, tm_disp)
        y_rows = _experts(blk_src, blk_exp, blk_nvalid, xs, w_exp_gate, w_exp_up, w_exp_down, i)
        xf, xb = _final(dest_tok.reshape(t // tm_comb, 1, 2 * tm_comb), x1, x1b, p2, wts.T,
                        w_ple_up, w_ple_gate, b_pg3, ln2_g3, ln2_b3, y_rows, i, tm_comb)
    return xf.reshape(bsz, seqlen, d)
```

```python
import functools

import jax
import jax.numpy as jnp
from jax import lax
from jax.experimental import pallas as pl
from jax.experimental.pallas import tpu as pltpu

F32 = jnp.float32
BF16 = jnp.bfloat16
I32 = jnp.int32

D_MODEL = 1024
D_CONV = 1024
CONV_WIDTH = 31
D_INNER = 2048
SSM_HEAD_DIM = 64
SSM_HEADS = 32
SSM_GROUPS = 8
HEADS_PER_GROUP = 4
D_STATE = 128
SSM_CONV_WIDTH = 4
D_XBC = D_INNER + 2 * SSM_GROUPS * D_STATE
GROUP_CH = HEADS_PER_GROUP * SSM_HEAD_DIM
N_EXPERTS = 16
N_EXPERT_GROUPS = 4
EXPERTS_PER_GROUP = 4
D_EXPERT = 512
PLE_DIM = 256
DEPTH = 4
ALPHA = (2.0 * DEPTH) ** 0.25
LN_EPS = 1e-5
RMS_EPS = 1e-5

LANES = 128
CONV_HALO = 32
CONV_RC = 64
CONV_CW = 256
SSM_HALO = 8
SSD_Q = 128
MOE_ROWS = 512
VMEM_LIMIT = 48 * 1024 * 1024

OFF_GLU = 0
OFF_Z = 2 * D_CONV
OFF_XBC = OFF_Z + D_INNER
OFF_DT = OFF_XBC + D_XBC
OFF_GATE = OFF_DT + SSM_HEADS


def _sigmoid(x):
    return 1.0 / (1.0 + jnp.exp(-x))


def _layer_norm(x, g, b):
    mu = jnp.mean(x, axis=-1, keepdims=True)
    xc = x - mu
    var = jnp.mean(xc * xc, axis=-1, keepdims=True)
    return xc * lax.rsqrt(var + LN_EPS) * g + b


def _params(n_axes):
    return pltpu.CompilerParams(dimension_semantics=("arbitrary",) * n_axes,
                                vmem_limit_bytes=VMEM_LIMIT)


def _mm_kernel(x_ref, w_ref, *rest, act, has_bias):
    if has_bias:
        b_ref, o_ref, wbf_ref = rest
    else:
        o_ref, wbf_ref = rest

    @pl.when(pl.program_id(1) == 0)
    def _():
        wbf_ref[...] = w_ref[...].astype(BF16)

    acc = jnp.dot(x_ref[...], wbf_ref[...], preferred_element_type=F32)
    if has_bias:
        acc = acc + b_ref[...]
    if act == "sigmoid":
        acc = _sigmoid(acc)
    elif act == "silu":
        acc = acc * _sigmoid(acc)
    elif act == "softplus":
        acc = jnp.maximum(acc, 0.0) + jnp.log1p(jnp.exp(-jnp.abs(acc)))
    o_ref[...] = acc.astype(o_ref.dtype)


def _matmul(x, w, w_index, bias, b_index, n_out, tn, act, out_dtype):
    t, k = x.shape
    tm = min(1024, t)
    w_block = (None,) * (w.ndim - 2) + (k, tn)
    in_specs = [pl.BlockSpec((tm, k), lambda n, m: (m, 0)),
                pl.BlockSpec(w_block, lambda n, m: w_index(n))]
    args = [x, w]
    if bias is not None:
        b_block = (None,) * (bias.ndim - 2) + (1, tn)
        in_specs.append(pl.BlockSpec(b_block, lambda n, m: b_index(n)))
        args.append(bias)
    return pl.pallas_call(
        functools.partial(_mm_kernel, act=act, has_bias=bias is not None),
        grid=(n_out // tn, t // tm),
        in_specs=in_specs,
        out_specs=pl.BlockSpec((tm, tn), lambda n, m: (m, n)),
        out_shape=jax.ShapeDtypeStruct((t, n_out), out_dtype),
        scratch_shapes=[pltpu.VMEM((k, tn), BF16)],
        compiler_params=_params(2),
        name="inproj_" + act,
    )(*args)


def _tail_weights_kernel(wa_ref, wb_ref, wgate_ref, wdt_ref):
    a = wa_ref[...]
    b = wb_ref[...]
    lane = lax.broadcasted_iota(I32, (D_MODEL, LANES), 1)
    wdt_ref[...] = jnp.where(lane < SSM_HEADS, a[:, :LANES], 0.0).astype(BF16)
    n = 2 * D_MODEL
    shifted = pltpu.roll(a, n - SSM_HEADS, axis=1)
    tail = pltpu.roll(b, LANES - SSM_HEADS, axis=1)
    wgate_ref[:, :n - LANES] = shifted[:, :n - LANES].astype(BF16)
    wgate_ref[:, n - LANES:] = jnp.where(lane < LANES - SSM_HEADS, shifted[:, n - LANES:], tail).astype(BF16)


def _tail_weights(w_in):
    depth, k, _ = w_in.shape
    n = 2 * D_MODEL
    return pl.pallas_call(
        _tail_weights_kernel,
        grid=(depth,),
        in_specs=[pl.BlockSpec((None, k, n), lambda l: (l, 0, OFF_DT // n)),
                  pl.BlockSpec((None, k, LANES), lambda l: (l, 0, (OFF_DT + n) // LANES))],
        out_specs=[pl.BlockSpec((None, k, n), lambda l: (l, 0, 0)),
                   pl.BlockSpec((None, k, LANES), lambda l: (l, 0, 0))],
        out_shape=[jax.ShapeDtypeStruct((depth, k, n), BF16),
                   jax.ShapeDtypeStruct((depth, k, LANES), BF16)],
        compiler_params=_params(1),
        name="tail_weights",
    )(w_in, w_in)


def _glu_kernel(x_ref, wa_ref, wg_ref, ba_ref, bg_ref, o_ref, wa_bf, wg_bf):
    @pl.when(pl.program_id(1) == 0)
    def _():
        wa_bf[...] = wa_ref[...].astype(BF16)
        wg_bf[...] = wg_ref[...].astype(BF16)

    x = x_ref[...]
    a = jnp.dot(x, wa_bf[...], preferred_element_type=F32) + ba_ref[...]
    g = jnp.dot(x, wg_bf[...], preferred_element_type=F32) + bg_ref[...]
    o_ref[...] = (a * _sigmoid(g)).astype(o_ref.dtype)


def _glu(x, w_in, b_glu3, layer, tn=512):
    t, k = x.shape
    tm = min(1024, t)
    half = D_CONV // tn
    return pl.pallas_call(
        _glu_kernel,
        grid=(half, t // tm),
        in_specs=[pl.BlockSpec((tm, k), lambda n, m: (m, 0)),
                  pl.BlockSpec((None, k, tn), lambda n, m: (layer, 0, n)),
                  pl.BlockSpec((None, k, tn), lambda n, m: (layer, 0, n + half)),
                  pl.BlockSpec((None, 1, tn), lambda n, m: (layer, 0, n)),
                  pl.BlockSpec((None, 1, tn), lambda n, m: (layer, 0, n + half))],
        out_specs=pl.BlockSpec((tm, tn), lambda n, m: (m, n)),
        out_shape=jax.ShapeDtypeStruct((t, D_CONV), BF16),
        scratch_shapes=[pltpu.VMEM((k, tn), BF16), pltpu.VMEM((k, tn), BF16)],
        compiler_params=_params(2),
        name="inproj_glu",
    )(x, w_in, w_in, b_glu3, b_glu3)


def _convbranch_kernel(c_ref, cw_ref, cb_ref, lg_ref, lb_ref, w_ref, gate_ref, o_ref,
                       ext_ref, sh_ref, conv_ref, wbf_ref, *, tl):
    first = (pl.program_id(0) == 0) & (pl.program_id(1) == 0)

    @pl.when(first)
    def _():
        wbf_ref[...] = w_ref[...].astype(BF16)

    @pl.when(pl.program_id(1) == 0)
    def _():
        ext_ref[0:CONV_HALO, :] = jnp.zeros((CONV_HALO, D_CONV), F32)

    @pl.when(pl.program_id(1) > 0)
    def _():
        ext_ref[0:CONV_HALO, :] = ext_ref[tl:tl + CONV_HALO, :]

    ext_ref[CONV_HALO:CONV_HALO + tl, :] = c_ref[...].astype(F32)

    sh_rows = tl + CONV_HALO - 8
    for s in range(1, 8):
        sh_ref[s - 1] = ext_ref[s:s + sh_rows, :]

    base = CONV_HALO - (CONV_WIDTH - 1)

    n_groups = CONV_RC // 8
    taps_of_shift = [[(a, 8 * a + s - base) for a in range(5) if 0 <= 8 * a + s - base < CONV_WIDTH]
                     for s in range(8)]

    def conv_rows(rc, carry):
        r0 = pl.multiple_of(rc * CONV_RC, CONV_RC)
        for lo in range(0, D_CONV, LANES):
            cols = slice(lo, lo + LANES)
            w = [jnp.broadcast_to(cw_ref[k:k + 1, cols], (8, LANES)) for k in range(CONV_WIDTH)]
            acc = [jnp.broadcast_to(cb_ref[:, cols], (8, LANES))] * n_groups
            for s in range(8):
                taps = taps_of_shift[s]
                for j in range(n_groups + max(a for a, _ in taps)):
                    used = [(a, k) for a, k in taps if 0 <= j - a < n_groups]
                    if not used:
                        continue
                    rows = pl.ds(r0 + 8 * j, 8)
                    x = ext_ref[rows, cols] if s == 0 else sh_ref[s - 1, rows, cols]
                    for a, k in used:
                        acc[j - a] = acc[j - a] + w[k] * x
            for i in range(n_groups):
                conv_ref[pl.ds(r0 + 8 * i, 8), cols] = acc[i]
        return carry

    lax.fori_loop(0, tl // CONV_RC, conv_rows, 0)
    h = _layer_norm(conv_ref[...], lg_ref[...], lb_ref[...])
    h = h * _sigmoid(h)
    y = jnp.dot(h.astype(BF16), wbf_ref[...], preferred_element_type=F32)
    o_ref[...] = (y * gate_ref[...].astype(F32)).astype(o_ref.dtype)


def _convbranch(c, gates, conv_w, conv_b3, ln_g3, ln_b3, w_conv_out, layer, bsz, seqlen, tl=256):
    t = c.shape[0]
    nl = seqlen // tl
    vec = pl.BlockSpec((None, 1, D_CONV), lambda b, i: (layer, 0, 0))
    return pl.pallas_call(
        functools.partial(_convbranch_kernel, tl=tl),
        grid=(bsz, nl),
        in_specs=[pl.BlockSpec((tl, D_CONV), lambda b, i: (b * nl + i, 0)),
                  pl.BlockSpec((None, CONV_WIDTH, D_CONV), lambda b, i: (layer, 0, 0)),
                  vec, vec, vec,
                  pl.BlockSpec((None, D_CONV, D_MODEL), lambda b, i: (layer, 0, 0)),
                  pl.BlockSpec((tl, D_MODEL), lambda b, i: (b * nl + i, 0))],
        out_specs=pl.BlockSpec((tl, D_MODEL), lambda b, i: (b * nl + i, 0)),
        out_shape=jax.ShapeDtypeStruct((t, D_MODEL), BF16),
        scratch_shapes=[pltpu.VMEM((tl + CONV_HALO, D_CONV), F32),
                        pltpu.VMEM((7, tl + CONV_HALO - 8, D_CONV), F32),
                        pltpu.VMEM((tl, D_CONV), F32),
                        pltpu.VMEM((D_CONV, D_MODEL), BF16)],
        compiler_params=_params(2),
        name="conv_module",
    )(c, conv_w, conv_b3, ln_g3, ln_b3, w_conv_out, gates)


def _expand_heads(v, g, lane_in_pair):
    rows = v.shape[0]
    b = [jnp.broadcast_to(v[:, g * HEADS_PER_GROUP + r:g * HEADS_PER_GROUP + r + 1], (rows, LANES))
         for r in range(HEADS_PER_GROUP)]
    first = lane_in_pair[:rows] < SSM_HEAD_DIM
    return jnp.concatenate([jnp.where(first, b[0], b[1]), jnp.where(first, b[2], b[3])], axis=1)


def _ssd_kernel(xbc_ref, dt_ref, zs_ref, cw_ref, cb_ref, alog_ref, dskip_ref, ng_ref, o_ref,
                ext_ref, sh_ref, act_ref, state_ref, *, q):
    @pl.when(pl.program_id(1) == 0)
    def _():
        ext_ref[0:SSM_HALO, :] = jnp.zeros((SSM_HALO, D_XBC), F32)
        state_ref[...] = jnp.zeros(state_ref.shape, F32)

    @pl.when(pl.program_id(1) > 0)
    def _():
        ext_ref[0:SSM_HALO, :] = ext_ref[q:q + SSM_HALO, :]

    ext_ref[SSM_HALO:SSM_HALO + q, :] = xbc_ref[...].astype(F32)

    base = SSM_HALO - (SSM_CONV_WIDTH - 1)
    for k in range(SSM_CONV_WIDTH - 1):
        sh_ref[k] = ext_ref[base + k:base + k + q, :]
    for lo in range(0, D_XBC, CONV_CW):
        for r0 in range(0, q, CONV_RC):
            cols = slice(lo, lo + CONV_CW)
            acc = cb_ref[:, cols] + cw_ref[SSM_CONV_WIDTH - 1:SSM_CONV_WIDTH, cols] * \
                ext_ref[SSM_HALO + r0:SSM_HALO + r0 + CONV_RC, cols]
            for k in range(SSM_CONV_WIDTH - 1):
                acc = acc + cw_ref[k:k + 1, cols] * sh_ref[k, r0:r0 + CONV_RC, cols]
            act_ref[r0:r0 + CONV_RC, cols] = acc * _sigmoid(acc)

    dt = dt_ref[...]
    adt = dt * (-jnp.exp(alog_ref[...]))
    row = lax.broadcasted_iota(I32, (q, q), 0)
    col = lax.broadcasted_iota(I32, (q, q), 1)
    causal = row >= col
    tril = jnp.where(causal, 1.0, 0.0).astype(F32)
    acs = jnp.dot(tril, adt, preferred_element_type=F32, precision=lax.Precision.HIGHEST)
    acs_t = acs.T
    dt_t = dt.T
    last = acs[q - 1:q, :]
    exp_acs = jnp.exp(acs)
    dt_decay = dt * jnp.exp(last - acs)
    chunk_decay = jnp.exp(last)
    lane_in_pair = lax.broadcasted_iota(I32, (q, LANES), 1)
    head_of_lane = lax.broadcasted_iota(I32, (1, GROUP_CH), 1) // SSM_HEAD_DIM

    b_off = D_INNER
    c_off = D_INNER + SSM_GROUPS * D_STATE
    for g in range(SSM_GROUPS):
        ch = slice(g * GROUP_CH, (g + 1) * GROUP_CH)
        xg = act_ref[:, ch]
        xg_bf = xg.astype(BF16)
        bg = act_ref[:, b_off + g * D_STATE:b_off + (g + 1) * D_STATE]
        cg = act_ref[:, c_off + g * D_STATE:c_off + (g + 1) * D_STATE].astype(BF16)
        cb = lax.dot_general(cg, bg.astype(BF16), (((1,), (1,)), ((), ())),
                             preferred_element_type=F32)
        ms, xblocks = [], []
        for r in range(HEADS_PER_GROUP):
            h = g * HEADS_PER_GROUP + r
            lmat = jnp.exp(jnp.where(causal, acs[:, h:h + 1] - acs_t[h:h + 1, :], -jnp.inf))
            ms.append((cb * lmat * dt_t[h:h + 1, :]).astype(BF16))
            head_mask = jnp.where(head_of_lane == r, 1.0, 0.0).astype(BF16)
            xblocks.append(xg_bf * head_mask)
        y_diag = jnp.dot(jnp.concatenate(ms, axis=1), jnp.concatenate(xblocks, axis=0),
                         preferred_element_type=F32)
        s_prev = state_ref[g]
        y_off = jnp.dot(cg, s_prev.astype(BF16), preferred_element_type=F32)
        yg = y_diag + y_off * _expand_heads(exp_acs, g, lane_in_pair) + xg * dskip_ref[:, ch]
        xw = (xg * _expand_heads(dt_decay, g, lane_in_pair)).astype(BF16)
        dec = _expand_heads(chunk_decay, g, lane_in_pair)
        state_ref[g] = s_prev * dec + jnp.dot(bg.T.astype(BF16), xw, preferred_element_type=F32)

        yz = yg * zs_ref[:, g * GROUP_CH:(g + 1) * GROUP_CH].astype(F32)
        ms = jnp.mean(yz * yz, axis=-1, keepdims=True)
        yn = yz * lax.rsqrt(ms + RMS_EPS) * ng_ref[:, g * GROUP_CH:(g + 1) * GROUP_CH]
        o_ref[:, g * GROUP_CH:(g + 1) * GROUP_CH] = yn.astype(o_ref.dtype)


def _ssd(xbc, dt, zs, ssm_conv_w, ssm_conv_b3, alog3, dskip3, ng3, layer, bsz, seqlen):
    t = xbc.shape[0]
    q = SSD_Q
    nq = seqlen // q
    tile = lambda width: pl.BlockSpec((q, width), lambda b, i: (b * nq + i, 0))
    vec = lambda width: pl.BlockSpec((None, 1, width), lambda b, i: (layer, 0, 0))
    return pl.pallas_call(
        functools.partial(_ssd_kernel, q=q),
        grid=(bsz, nq),
        in_specs=[tile(D_XBC), tile(LANES), tile(D_INNER),
                  pl.BlockSpec((None, SSM_CONV_WIDTH, D_XBC), lambda b, i: (layer, 0, 0)),
                  vec(D_XBC), vec(LANES), vec(D_INNER), vec(D_INNER)],
        out_specs=tile(D_INNER),
        out_shape=jax.ShapeDtypeStruct((t, D_INNER), BF16),
        scratch_shapes=[pltpu.VMEM((q + SSM_HALO, D_XBC), F32),
                        pltpu.VMEM((SSM_CONV_WIDTH - 1, q, D_XBC), F32),
                        pltpu.VMEM((q, D_XBC), F32),
                        pltpu.VMEM((SSM_GROUPS, D_STATE, GROUP_CH), F32)],
        compiler_params=_params(2),
        name="ssd_mixer",
    )(xbc, dt, zs, ssm_conv_w, ssm_conv_b3, alog3, dskip3, ng3)


def _first_argmax(vals):
    best, idx = vals[0], jnp.zeros(vals[0].shape, I32)
    for j in range(1, len(vals)):
        gt = vals[j] > best
        idx = jnp.where(gt, j, idx)
        best = jnp.where(gt, vals[j], best)
    return idx, best


def _select(idx, vals):
    out = vals[len(vals) - 1]
    for j in range(len(vals) - 2, -1, -1):
        out = jnp.where(idx == j, vals[j], out)
    return out


def _outproj_kernel(yn_ref, y1_ref, gs_ref, x_ref, wso_ref, wo_ref, lg_ref, lb_ref, wr_ref, rb_ref,
                    x1_ref, x1b_ref, eid_ref, rank_ref, wts_ref, cnt_ref,
                    wso_bf, wo_bf, base_ref, *, tm):
    @pl.when(pl.program_id(0) == 0)
    def _():
        wso_bf[...] = wso_ref[...].astype(BF16)
        wo_bf[...] = wo_ref[...].astype(BF16)
        base_ref[...] = jnp.zeros(base_ref.shape, F32)

    y_ssm = jnp.dot(yn_ref[...], wso_bf[...], preferred_element_type=F32)
    merged = y1_ref[...].astype(F32) + gs_ref[...].astype(F32) * y_ssm
    mix = jnp.dot(merged.astype(BF16), wo_bf[...], preferred_element_type=F32)
    x1 = _layer_norm(ALPHA * x_ref[...] + mix, lg_ref[...], lb_ref[...])
    x1_ref[...] = x1
    x1b_ref[...] = x1.astype(BF16)

    logits = lax.dot_general(wr_ref[...], x1, (((1,), (1,)), ((), ())),
                             preferred_element_type=F32, precision=lax.Precision.HIGHEST)
    scores = _sigmoid(logits)
    sel = scores + rb_ref[...]
    sel_rows = [sel[e:e + 1, :] for e in range(N_EXPERTS)]
    sc_rows = [scores[e:e + 1, :] for e in range(N_EXPERTS)]

    gscores = []
    for gi in range(N_EXPERT_GROUPS):
        v = sel_rows[gi * EXPERTS_PER_GROUP:(gi + 1) * EXPERTS_PER_GROUP]
        best = None
        for a in range(EXPERTS_PER_GROUP):
            for b in range(a + 1, EXPERTS_PER_GROUP):
                s = v[a] + v[b]
                best = s if best is None else jnp.maximum(best, s)
        gscores.append(best)
    grp, _ = _first_argmax(gscores)

    sel_in = [_select(grp, [sel_rows[gi * EXPERTS_PER_GROUP + j] for gi in range(N_EXPERT_GROUPS)])
              for j in range(EXPERTS_PER_GROUP)]
    sc_in = [_select(grp, [sc_rows[gi * EXPERTS_PER_GROUP + j] for gi in range(N_EXPERT_GROUPS)])
             for j in range(EXPERTS_PER_GROUP)]
    i1, _ = _first_argmax(sel_in)
    neg = jnp.full(sel_in[0].shape, -jnp.inf, F32)
    i2, _ = _first_argmax([jnp.where(i1 == j, neg, sel_in[j]) for j in range(EXPERTS_PER_GROUP)])
    s1 = _select(i1, sc_in)
    s2 = _select(i2, sc_in)
    tot = s1 + s2
    e1 = grp * EXPERTS_PER_GROUP + i1
    e2 = grp * EXPERTS_PER_GROUP + i2
    eid_ref[0:1, :] = e1
    eid_ref[1:2, :] = e2
    wts_ref[0:1, :] = s1 / tot
    wts_ref[1:2, :] = s2 / tot

    eio = lax.broadcasted_iota(I32, (N_EXPERTS, tm), 0)
    oh1 = jnp.where(eio == e1, 1.0, 0.0).astype(F32)
    oh2 = jnp.where(eio == e2, 1.0, 0.0).astype(F32)
    both = oh1 + oh2
    srow = lax.broadcasted_iota(I32, (tm, tm), 0)
    scol = lax.broadcasted_iota(I32, (tm, tm), 1)
    before = jnp.where(srow < scol, 1.0, 0.0).astype(BF16)
    cum = jnp.dot(both.astype(BF16), before, preferred_element_type=F32)
    pos = base_ref[:, 0:1] + cum
    rank_ref[0:1, :] = jnp.sum(oh1 * pos, axis=0, keepdims=True).astype(I32)
    rank_ref[1:2, :] = jnp.sum(oh2 * pos, axis=0, keepdims=True).astype(I32)
    base_ref[...] = base_ref[...] + jnp.sum(both, axis=1, keepdims=True)
    cnt_ref[...] = base_ref[...].astype(I32)


def _outproj(yn, y1g, gates, x, w_ssm_out, w_out, ln_g3, ln_b3, wr_t, rbias, layer, tm=512):
    t = x.shape[0]
    tm = min(tm, t)
    tile = lambda width: pl.BlockSpec((tm, width), lambda m: (m, 0))
    vec = pl.BlockSpec((None, 1, D_MODEL), lambda m: (layer, 0, 0))
    pair = pl.BlockSpec((2, tm), lambda m: (0, m))
    return pl.pallas_call(
        functools.partial(_outproj_kernel, tm=tm),
        grid=(t // tm,),
        in_specs=[tile(D_INNER), tile(D_MODEL),
                  pl.BlockSpec((tm, D_MODEL), lambda m: (m, 1)),
                  tile(D_MODEL),
                  pl.BlockSpec((None, D_INNER, D_MODEL), lambda m: (layer, 0, 0)),
                  pl.BlockSpec((None, D_MODEL, D_MODEL), lambda m: (layer, 0, 0)),
                  vec, vec,
                  pl.BlockSpec((N_EXPERTS, D_MODEL), lambda m: (0, 0)),
                  pl.BlockSpec((N_EXPERTS, 1), lambda m: (0, 0))],
        out_specs=[tile(D_MODEL), tile(D_MODEL), pair, pair, pair,
                   pl.BlockSpec((N_EXPERTS, LANES), lambda m: (0, 0))],
        out_shape=[jax.ShapeDtypeStruct((t, D_MODEL), F32),
                   jax.ShapeDtypeStruct((t, D_MODEL), BF16),
                   jax.ShapeDtypeStruct((2, t), I32),
                   jax.ShapeDtypeStruct((2, t), I32),
                   jax.ShapeDtypeStruct((2, t), F32),
                   jax.ShapeDtypeStruct((N_EXPERTS, LANES), I32)],
        scratch_shapes=[pltpu.VMEM((D_INNER, D_MODEL), BF16),
                        pltpu.VMEM((D_MODEL, D_MODEL), BF16),
                        pltpu.VMEM((N_EXPERTS, LANES), F32)],
        compiler_params=_params(1),
        name="outproj_ln_router",
    )(yn, y1g, gates, x, w_ssm_out, w_out, ln_g3, ln_b3, wr_t, rbias)


def _row_copy(src_ref, src_row, dst_ref, dst_row, sem):
    return pltpu.make_async_copy(src_ref.at[pl.ds(src_row, 1)], dst_ref.at[pl.ds(dst_row, 1)], sem)


def _dispatch_kernel(fill_ref, dest_ref, x_ref, xs_hbm, zeros_ref, sem, blk_sem, *, tm, n_blk):
    def zero_row(row):
        return _row_copy(zeros_ref, 0, xs_hbm, row, sem)

    def zero_block(b):
        return pltpu.make_async_copy(
            zeros_ref, xs_hbm.at[pl.ds(pl.multiple_of(b * MOE_ROWS, MOE_ROWS), MOE_ROWS)], blk_sem)

    @pl.when(pl.program_id(0) == 0)
    def _():
        zeros_ref[...] = jnp.zeros(zeros_ref.shape, F32)
        n_active = fill_ref[2 * N_EXPERTS]
        for e in range(N_EXPERTS):
            pad_start = fill_ref[e]
            lax.fori_loop(0, fill_ref[N_EXPERTS + e],
                          lambda j, c: (zero_row(pad_start + j).start(), c)[1], 0)
        lax.fori_loop(n_active, n_blk, lambda b, c: (zero_block(b).start(), c)[1], 0)
        lax.fori_loop(0, fill_ref[2 * N_EXPERTS + 1], lambda j, c: (zero_row(0).wait(), c)[1], 0)
        lax.fori_loop(n_active, n_blk, lambda b, c: (zero_block(b).wait(), c)[1], 0)

    def issue(j, carry):
        _row_copy(x_ref, j, xs_hbm, dest_ref[0, 0, 2 * j], sem).start()
        _row_copy(x_ref, j, xs_hbm, dest_ref[0, 0, 2 * j + 1], sem).start()
        return carry

    lax.fori_loop(0, tm, issue, 0, unroll=8)
    for _ in range(2):
        pltpu.make_async_copy(x_ref, xs_hbm.at[pl.ds(0, tm)], sem).wait()


def _dispatch(fill, dest_tiles, x1, n_blk, tm):
    t = x1.shape[0]
    grid_spec = pltpu.PrefetchScalarGridSpec(
        num_scalar_prefetch=1,
        grid=(t // tm,),
        in_specs=[pl.BlockSpec((1, 1, 2 * tm), lambda m, fill: (m, 0, 0), memory_space=pltpu.SMEM),
                  pl.BlockSpec((tm, D_MODEL), lambda m, fill: (m, 0))],
        out_specs=pl.BlockSpec(memory_space=pl.ANY),
        scratch_shapes=[pltpu.VMEM((MOE_ROWS, D_MODEL), F32),
                        pltpu.SemaphoreType.DMA(()), pltpu.SemaphoreType.DMA(())],
    )
    return pl.pallas_call(
        functools.partial(_dispatch_kernel, tm=tm, n_blk=n_blk),
        grid_spec=grid_spec,
        out_shape=jax.ShapeDtypeStruct((n_blk * MOE_ROWS, D_MODEL), F32),
        compiler_params=_params(1),
        name="moe_dispatch",
    )(fill, dest_tiles, x1)


def _expert_kernel(src_ref, exp_ref, nvalid_ref, xs_ref, wg_ref, wu_ref, wd_ref, y_ref,
                   wgu_bf, wd_bf):
    i = pl.program_id(0)
    nvalid = nvalid_ref[i]
    changed = (i == 0) | (exp_ref[i] != exp_ref[jnp.maximum(i - 1, 0)])

    @pl.when((nvalid > 0) & changed)
    def _():
        wgu_bf[:, :D_EXPERT] = wg_ref[...].astype(BF16)
        wgu_bf[:, D_EXPERT:] = wu_ref[...].astype(BF16)
        wd_bf[...] = wd_ref[...].astype(BF16)

    @pl.when(nvalid > 0)
    def _():
        rows = lax.broadcasted_iota(I32, (MOE_ROWS, 1), 0)
        x = jnp.where(rows < nvalid, xs_ref[...], 0.0).astype(BF16)
        gu = jnp.dot(x, wgu_bf[...], preferred_element_type=F32)
        hg = gu[:, :D_EXPERT]
        h = hg * _sigmoid(hg) * gu[:, D_EXPERT:]
        y_ref[...] = jnp.dot(h.astype(BF16), wd_bf[...], preferred_element_type=F32)

    @pl.when(nvalid == 0)
    def _():
        y_ref[...] = jnp.zeros(y_ref.shape, F32)


def _experts(blk_src, blk_exp, blk_nvalid, xs, wg, wu, wd, layer):
    n_rows = xs.shape[0]
    n_blk = n_rows // MOE_ROWS
    grid_spec = pltpu.PrefetchScalarGridSpec(
        num_scalar_prefetch=3,
        grid=(n_blk,),
        in_specs=[pl.BlockSpec((MOE_ROWS, D_MODEL), lambda i, src, exp, nv: (src[i], 0)),
                  pl.BlockSpec((None, None, D_MODEL, D_EXPERT),
                               lambda i, src, exp, nv: (layer, exp[i], 0, 0)),
                  pl.BlockSpec((None, None, D_MODEL, D_EXPERT),
                               lambda i, src, exp, nv: (layer, exp[i], 0, 0)),
                  pl.BlockSpec((None, None, D_EXPERT, D_MODEL),
                               lambda i, src, exp, nv: (layer, exp[i], 0, 0))],
        out_specs=pl.BlockSpec((MOE_ROWS, D_MODEL), lambda i, src, exp, nv: (i, 0)),
        scratch_shapes=[pltpu.VMEM((D_MODEL, 2 * D_EXPERT), BF16),
                        pltpu.VMEM((D_EXPERT, D_MODEL), BF16)],
    )
    return pl.pallas_call(
        _expert_kernel,
        grid_spec=grid_spec,
        out_shape=jax.ShapeDtypeStruct((n_rows, D_MODEL), F32),
        compiler_params=_params(1),
        name="moe_experts",
    )(blk_src, blk_exp, blk_nvalid, xs, wg, wu, wd)


def _moe_plan(eid, rank, counts, t):
    cnt = counts[:, 0]
    nblk_e = (cnt + MOE_ROWS - 1) // MOE_ROWS
    blk_end = jnp.cumsum(nblk_e)
    blk_start = blk_end - nblk_e
    n_active = blk_end[N_EXPERTS - 1]
    experts = jnp.arange(N_EXPERTS, dtype=I32)
    row_start = blk_start * MOE_ROWS
    dest = rank + jnp.sum(jnp.where(eid[None] == experts[:, None, None],
                                    row_start[:, None, None], 0), axis=0)
    n_blk = (2 * t) // MOE_ROWS + N_EXPERTS
    ids = jnp.arange(n_blk, dtype=I32)
    src = jnp.minimum(ids, n_active - 1)
    exp = jnp.minimum(jnp.sum((blk_end[None, :] <= src[:, None]).astype(I32), axis=1), N_EXPERTS - 1)
    onehot = (exp[:, None] == experts[None, :]).astype(I32)
    pick = lambda table: jnp.sum(onehot * table[None, :], axis=1)
    left = pick(cnt) - (src - pick(blk_start)) * MOE_ROWS
    nvalid = jnp.where(ids < n_active, jnp.clip(left, 0, MOE_ROWS), 0)
    pad_len = nblk_e * MOE_ROWS - cnt
    fill = jnp.concatenate([blk_start * MOE_ROWS + cnt, pad_len,
                            jnp.stack([n_active, jnp.sum(pad_len)])]).astype(I32)
    return dest.astype(I32), src.astype(I32), exp, nvalid.astype(I32), fill, n_blk


def _final_kernel(dest_ref, dnext_ref, x1_ref, x1b_ref, p_ref, wt_ref, wpu_ref, wpg_ref, bpg_ref,
                  lg_ref, lb_ref, y_hbm, x2_ref, x2b_ref, g_ref, wpu_bf, wpg_bf, sems, *, tm, n_tiles):
    i = pl.program_id(0)
    slot = lax.rem(i, 2)

    def gather(d_ref, s):
        def issue(j, carry):
            _row_copy(y_hbm, d_ref[0, 0, 2 * j], g_ref.at[s, 0], j, sems.at[s]).start()
            _row_copy(y_hbm, d_ref[0, 0, 2 * j + 1], g_ref.at[s, 1], j, sems.at[s]).start()
            return carry

        lax.fori_loop(0, tm, issue, 0, unroll=8)

    @pl.when(i == 0)
    def _():
        wpu_bf[...] = wpu_ref[...].astype(BF16)
        wpg_bf[...] = wpg_ref[...].astype(BF16)
        gather(dest_ref, 0)

    @pl.when(i + 1 < n_tiles)
    def _():
        gather(dnext_ref, 1 - slot)

    up = jnp.dot(p_ref[...].astype(BF16), wpu_bf[...], preferred_element_type=F32)
    gate = _sigmoid(jnp.dot(x1b_ref[...], wpg_bf[...], preferred_element_type=F32) + bpg_ref[...])
    resid = ALPHA * x1_ref[...] + up * gate

    for k in range(2):
        pltpu.make_async_copy(y_hbm.at[pl.ds(0, tm)], g_ref.at[slot, k], sems.at[slot]).wait()

    wt = wt_ref[...]
    moe = wt[:, 0:1] * g_ref[slot, 0] + wt[:, 1:2] * g_ref[slot, 1]
    x2 = _layer_norm(resid + moe, lg_ref[...], lb_ref[...])
    x2_ref[...] = x2
    x2b_ref[...] = x2.astype(BF16)


def _final(dest_tiles, x1, x1b, p, wt_tok, w_ple_up, w_ple_gate, b_pg3, ln_g3, ln_b3, y_rows, layer, tm):
    t = x1.shape[0]
    n_tiles = t // tm
    tile = lambda width: pl.BlockSpec((tm, width), lambda m: (m, 0))
    vec = pl.BlockSpec((None, 1, D_MODEL), lambda m: (layer, 0, 0))
    return pl.pallas_call(
        functools.partial(_final_kernel, tm=tm, n_tiles=n_tiles),
        grid=(n_tiles,),
        in_specs=[pl.BlockSpec((1, 1, 2 * tm), lambda m: (m, 0, 0), memory_space=pltpu.SMEM),
                  pl.BlockSpec((1, 1, 2 * tm), lambda m: (jnp.minimum(m + 1, n_tiles - 1), 0, 0),
                               memory_space=pltpu.SMEM),
                  tile(D_MODEL), tile(D_MODEL),
                  pl.BlockSpec((None, tm, PLE_DIM), lambda m: (layer, m, 0)),
                  tile(2),
                  pl.BlockSpec((None, PLE_DIM, D_MODEL), lambda m: (layer, 0, 0)),
                  pl.BlockSpec((None, D_MODEL, D_MODEL), lambda m: (layer, 0, 0)),
                  vec, vec, vec,
                  pl.BlockSpec(memory_space=pl.ANY)],
        out_specs=[tile(D_MODEL), tile(D_MODEL)],
        out_shape=[jax.ShapeDtypeStruct((t, D_MODEL), F32),
                   jax.ShapeDtypeStruct((t, D_MODEL), BF16)],
        scratch_shapes=[pltpu.VMEM((2, 2, tm, D_MODEL), F32),
                        pltpu.VMEM((PLE_DIM, D_MODEL), BF16), pltpu.VMEM((D_MODEL, D_MODEL), BF16),
                        pltpu.SemaphoreType.DMA((2,))],
        compiler_params=_params(1),
        name="combine_ple_ln",
    )(dest_tiles, dest_tiles, x1, x1b, p, wt_tok, w_ple_up, w_ple_gate, b_pg3, ln_g3, ln_b3, y_rows)


def kernel(x, p, w_in, b_glu, b_branch_gate, conv_w, conv_b, conv_ln_g, conv_ln_b, w_conv_out,
           ssm_conv_w, ssm_conv_b, dt_bias, a_log, d_skip, ssm_norm_g, w_ssm_out, w_out,
           ln1_g, ln1_b, w_router, router_bias, w_exp_gate, w_exp_up, w_exp_down,
           w_ple_up, w_ple_gate, b_ple_gate, ln2_g, ln2_b):
    bsz, seqlen, d = x.shape
    depth = w_in.shape[0]
    t = bsz * seqlen
    tm_disp = min(512, t)
    tm_comb = min(256, t)

    row3 = lambda a: a.reshape(a.shape[0], 1, a.shape[1])
    pad_lanes = lambda a: jnp.pad(a, ((0, 0), (0, LANES - a.shape[1])))
    b_glu3, b_gate3 = row3(b_glu), row3(b_branch_gate)
    conv_b3, cln_g3, cln_b3 = row3(conv_b), row3(conv_ln_g), row3(conv_ln_b)
    ssm_conv_b3, ng3 = row3(ssm_conv_b), row3(ssm_norm_g)
    dtb3, alog3 = row3(pad_lanes(dt_bias)), row3(pad_lanes(a_log))
    dskip3 = row3(jnp.repeat(d_skip, SSM_HEAD_DIM, axis=1))
    ln1_g3, ln1_b3, ln2_g3, ln2_b3 = row3(ln1_g), row3(ln1_b), row3(ln2_g), row3(ln2_b)
    b_pg3 = row3(b_ple_gate)
    w_gate, w_dt = _tail_weights(w_in)
    wr_t = w_router.T
    rbias = router_bias.reshape(N_EXPERTS, 1)
    p2 = p.reshape(depth, t, PLE_DIM)

    xf = x.reshape(t, d)
    xb = xf.astype(BF16)
    for i in range(depth):
        c = _glu(xb, w_in, b_glu3, i)
        zs = _matmul(xb, w_in, lambda n: (i, 0, OFF_Z // 1024 + n), None, None,
                     D_INNER, 1024, "silu", BF16)
        xbc = _matmul(xb, w_in, lambda n: (i, 0, OFF_XBC // 1024 + n), None, None,
                      D_XBC, 1024, "none", BF16)
        dt = _matmul(xb, w_dt, lambda n: (i, 0, n), dtb3, lambda n: (i, 0, n),
                     LANES, LANES, "softplus", F32)
        gates = _matmul(xb, w_gate, lambda n: (i, 0, n), b_gate3, lambda n: (i, 0, n),
                        2 * D_MODEL, 1024, "sigmoid", BF16)
        y1g = _convbranch(c, gates, conv_w, conv_b3, cln_g3, cln_b3, w_conv_out, i, bsz, seqlen)
        yn = _ssd(xbc, dt, zs, ssm_conv_w, ssm_conv_b3, alog3, dskip3, ng3, i, bsz, seqlen)
        x1, x1b, eid, rank, wts, counts = _outproj(yn, y1g, gates, xf, w_ssm_out, w_out,
                                                   ln1_g3, ln1_b3, wr_t, rbias, i)
        dest, blk_src, blk_exp, blk_nvalid, fill, n_blk = _moe_plan(eid, rank, counts, t)
        dest_tok = dest.T
        xs = _dispatch(fill, dest_tok.reshape(t // tm_disp, 1, 2 * tm_disp), x1, n_blk, tm_disp)
        y_rows = _experts(blk_src, blk_exp, blk_nvalid, xs, w_exp_gate, w_exp_up, w_exp_down, i)
        xf, xb = _final(dest_tok.reshape(t // tm_comb, 1, 2 * tm_comb), x1, x1b, p2, wts.T,
                        w_ple_up, w_ple_gate, b_pg3, ln2_g3, ln2_b3, y_rows, i, tm_comb)
    return xf.reshape(bsz, seqlen, d)
```

```python
import functools

import jax
import jax.numpy as jnp
from jax import lax
from jax.experimental import pallas as pl
from jax.experimental.pallas import tpu as pltpu

F32 = jnp.float32
BF16 = jnp.bfloat16
I32 = jnp.int32

D_MODEL = 1024
D_CONV = 1024
CONV_WIDTH = 31
D_INNER = 2048
SSM_HEAD_DIM = 64
SSM_HEADS = 32
SSM_GROUPS = 8
HEADS_PER_GROUP = 4
D_STATE = 128
SSM_CONV_WIDTH = 4
D_XBC = D_INNER + 2 * SSM_GROUPS * D_STATE
GROUP_CH = HEADS_PER_GROUP * SSM_HEAD_DIM
N_EXPERTS = 16
N_EXPERT_GROUPS = 4
EXPERTS_PER_GROUP = 4
D_EXPERT = 512
PLE_DIM = 256
DEPTH = 4
ALPHA = (2.0 * DEPTH) ** 0.25
LN_EPS = 1e-5
RMS_EPS = 1e-5

LANES = 128
CONV_HALO = 32
CONV_RC = 64
CONV_CW = 256
SSM_HALO = 8
SSD_Q = 128
MOE_ROWS = 512
VMEM_LIMIT = 48 * 1024 * 1024
INPROJ_TM = 2048
INPROJ_VMEM_LIMIT = 56 * 1024 * 1024

OFF_GLU = 0
OFF_Z = 2 * D_CONV
OFF_XBC = OFF_Z + D_INNER
OFF_DT = OFF_XBC + D_XBC
OFF_GATE = OFF_DT + SSM_HEADS


def _sigmoid(x):
    return 1.0 / (1.0 + jnp.exp(-x))


def _layer_norm(x, g, b):
    mu = jnp.mean(x, axis=-1, keepdims=True)
    xc = x - mu
    var = jnp.mean(xc * xc, axis=-1, keepdims=True)
    return xc * lax.rsqrt(var + LN_EPS) * g + b


def _params(n_axes, vmem_limit=VMEM_LIMIT):
    return pltpu.CompilerParams(dimension_semantics=("arbitrary",) * n_axes,
                                vmem_limit_bytes=vmem_limit)


def _mm_kernel(x_ref, w_ref, *rest, act, has_bias, w_rows_are_outputs):
    if has_bias:
        b_ref, o_ref, wbf_ref = rest
    else:
        o_ref, wbf_ref = rest

    @pl.when(pl.program_id(1) == 0)
    def _():
        w = w_ref[...]
        wbf_ref[...] = (w.T if w_rows_are_outputs else w).astype(BF16)

    acc = jnp.dot(x_ref[...], wbf_ref[...], preferred_element_type=F32)
    if has_bias:
        acc = acc + b_ref[...]
    if act == "sigmoid":
        acc = _sigmoid(acc)
    elif act == "silu":
        acc = acc * _sigmoid(acc)
    o_ref[...] = acc.astype(o_ref.dtype)


def _matmul(x, w, w_index, bias, b_index, n_out, tn, act, out_dtype, w_rows_are_outputs=False):
    t, k = x.shape
    tm = min(INPROJ_TM, t)
    w_block = (None,) * (w.ndim - 2) + ((tn, k) if w_rows_are_outputs else (k, tn))
    in_specs = [pl.BlockSpec((tm, k), lambda n, m: (m, 0)),
                pl.BlockSpec(w_block, lambda n, m: w_index(n))]
    args = [x, w]
    if bias is not None:
        b_block = (None,) * (bias.ndim - 2) + (1, tn)
        in_specs.append(pl.BlockSpec(b_block, lambda n, m: b_index(n)))
        args.append(bias)
    return pl.pallas_call(
        functools.partial(_mm_kernel, act=act, has_bias=bias is not None,
                          w_rows_are_outputs=w_rows_are_outputs),
        grid=(n_out // tn, t // tm),
        in_specs=in_specs,
        out_specs=pl.BlockSpec((tm, tn), lambda n, m: (m, n)),
        out_shape=jax.ShapeDtypeStruct((t, n_out), out_dtype),
        scratch_shapes=[pltpu.VMEM((k, tn), BF16)],
        compiler_params=_params(2, INPROJ_VMEM_LIMIT),
        name="inproj_" + act,
    )(*args)


def _tail_weights_kernel(wt_hbm, wgate_ref, wdt_ref, buf_ref, sem):
    n_tail = buf_ref.shape[0]
    copy = pltpu.make_async_copy(wt_hbm.at[pl.program_id(0), pl.ds(OFF_DT, n_tail)], buf_ref, sem)
    copy.start()
    copy.wait()
    lane = lax.broadcasted_iota(I32, (D_MODEL, LANES), 1)
    wdt_ref[...] = jnp.where(lane < SSM_HEADS, buf_ref[0:LANES, :].T, 0.0).astype(BF16)
    step = 4 * LANES
    for lo in range(0, 2 * D_MODEL, step):
        rows = slice(SSM_HEADS + lo, SSM_HEADS + lo + step)
        wgate_ref[:, lo:lo + step] = buf_ref[rows, :].T.astype(BF16)


def _tail_weights(w_in_t):
    depth, n_all, k = w_in_t.shape
    n = 2 * D_MODEL
    return pl.pallas_call(
        _tail_weights_kernel,
        grid=(depth,),
        in_specs=[pl.BlockSpec(memory_space=pl.ANY)],
        out_specs=[pl.BlockSpec((None, k, n), lambda l: (l, 0, 0)),
                   pl.BlockSpec((None, k, LANES), lambda l: (l, 0, 0))],
        out_shape=[jax.ShapeDtypeStruct((depth, k, n), BF16),
                   jax.ShapeDtypeStruct((depth, k, LANES), BF16)],
        scratch_shapes=[pltpu.VMEM((n_all - OFF_DT, k), F32), pltpu.SemaphoreType.DMA(())],
        compiler_params=_params(1),
        name="tail_weights",
    )(w_in_t)


def _glu_kernel(x_ref, wa_ref, wg_ref, ba_ref, bg_ref, o_ref, wa_bf, wg_bf):
    @pl.when(pl.program_id(1) == 0)
    def _():
        wa_bf[...] = wa_ref[...].T.astype(BF16)
        wg_bf[...] = wg_ref[...].T.astype(BF16)

    x = x_ref[...]
    a = jnp.dot(x, wa_bf[...], preferred_element_type=F32) + ba_ref[...]
    g = jnp.dot(x, wg_bf[...], preferred_element_type=F32) + bg_ref[...]
    o_ref[...] = (a * _sigmoid(g)).astype(o_ref.dtype)


def _glu(x, w_in_t, b_glu3, layer, tn=512):
    t, k = x.shape
    tm = min(INPROJ_TM, t)
    half = D_CONV // tn
    return pl.pallas_call(
        _glu_kernel,
        grid=(half, t // tm),
        in_specs=[pl.BlockSpec((tm, k), lambda n, m: (m, 0)),
                  pl.BlockSpec((None, tn, k), lambda n, m: (layer, n, 0)),
                  pl.BlockSpec((None, tn, k), lambda n, m: (layer, n + half, 0)),
                  pl.BlockSpec((None, 1, tn), lambda n, m: (layer, 0, n)),
                  pl.BlockSpec((None, 1, tn), lambda n, m: (layer, 0, n + half))],
        out_specs=pl.BlockSpec((tm, tn), lambda n, m: (m, n)),
        out_shape=jax.ShapeDtypeStruct((t, D_CONV), BF16),
        scratch_shapes=[pltpu.VMEM((k, tn), BF16), pltpu.VMEM((k, tn), BF16)],
        compiler_params=_params(2, INPROJ_VMEM_LIMIT),
        name="inproj_glu",
    )(x, w_in_t, w_in_t, b_glu3, b_glu3)


def _convbranch_kernel(c_ref, cw_ref, cb_ref, lg_ref, lb_ref, w_ref, gate_ref, o_ref,
                       ext_ref, sh_ref, conv_ref, wbf_ref, *, tl):
    first = (pl.program_id(0) == 0) & (pl.program_id(1) == 0)

    @pl.when(first)
    def _():
        wbf_ref[...] = w_ref[...].astype(BF16)

    @pl.when(pl.program_id(1) == 0)
    def _():
        ext_ref[0:CONV_HALO, :] = jnp.zeros((CONV_HALO, D_CONV), F32)

    @pl.when(pl.program_id(1) > 0)
    def _():
        ext_ref[0:CONV_HALO, :] = ext_ref[tl:tl + CONV_HALO, :]

    ext_ref[CONV_HALO:CONV_HALO + tl, :] = c_ref[...].astype(F32)

    sh_rows = tl + CONV_HALO - 8
    for s in range(1, 8):
        sh_ref[s - 1] = ext_ref[s:s + sh_rows, :]

    base = CONV_HALO - (CONV_WIDTH - 1)

    n_groups = CONV_RC // 8
    taps_of_shift = [[(a, 8 * a + s - base) for a in range(5) if 0 <= 8 * a + s - base < CONV_WIDTH]
                     for s in range(8)]

    def conv_rows(rc, carry):
        r0 = pl.multiple_of(rc * CONV_RC, CONV_RC)
        for lo in range(0, D_CONV, LANES):
            cols = slice(lo, lo + LANES)
            w = [jnp.broadcast_to(cw_ref[k:k + 1, cols], (8, LANES)) for k in range(CONV_WIDTH)]
            acc = [jnp.broadcast_to(cb_ref[:, cols], (8, LANES))] * n_groups
            for s in range(8):
                taps = taps_of_shift[s]
                for j in range(n_groups + max(a for a, _ in taps)):
                    used = [(a, k) for a, k in taps if 0 <= j - a < n_groups]
                    if not used:
                        continue
                    rows = pl.ds(r0 + 8 * j, 8)
                    x = ext_ref[rows, cols] if s == 0 else sh_ref[s - 1, rows, cols]
                    for a, k in used:
                        acc[j - a] = acc[j - a] + w[k] * x
            for i in range(n_groups):
                conv_ref[pl.ds(r0 + 8 * i, 8), cols] = acc[i]
        return carry

    lax.fori_loop(0, tl // CONV_RC, conv_rows, 0)
    h = _layer_norm(conv_ref[...], lg_ref[...], lb_ref[...])
    h = h * _sigmoid(h)
    y = jnp.dot(h.astype(BF16), wbf_ref[...], preferred_element_type=F32)
    o_ref[...] = (y * gate_ref[...].astype(F32)).astype(o_ref.dtype)


def _convbranch(c, gates, conv_w, conv_b3, ln_g3, ln_b3, w_conv_out, layer, bsz, seqlen, tl=256):
    t = c.shape[0]
    nl = seqlen // tl
    vec = pl.BlockSpec((None, 1, D_CONV), lambda b, i: (layer, 0, 0))
    return pl.pallas_call(
        functools.partial(_convbranch_kernel, tl=tl),
        grid=(bsz, nl),
        in_specs=[pl.BlockSpec((tl, D_CONV), lambda b, i: (b * nl + i, 0)),
                  pl.BlockSpec((None, CONV_WIDTH, D_CONV), lambda b, i: (layer, 0, 0)),
                  vec, vec, vec,
                  pl.BlockSpec((None, D_CONV, D_MODEL), lambda b, i: (layer, 0, 0)),
                  pl.BlockSpec((tl, D_MODEL), lambda b, i: (b * nl + i, 0))],
        out_specs=pl.BlockSpec((tl, D_MODEL), lambda b, i: (b * nl + i, 0)),
        out_shape=jax.ShapeDtypeStruct((t, D_MODEL), BF16),
        scratch_shapes=[pltpu.VMEM((tl + CONV_HALO, D_CONV), F32),
                        pltpu.VMEM((7, tl + CONV_HALO - 8, D_CONV), F32),
                        pltpu.VMEM((tl, D_CONV), F32),
                        pltpu.VMEM((D_CONV, D_MODEL), BF16)],
        compiler_params=_params(2),
        name="conv_module",
    )(c, conv_w, conv_b3, ln_g3, ln_b3, w_conv_out, gates)


def _expand_heads(v, g, lane_in_pair):
    rows = v.shape[0]
    b = [jnp.broadcast_to(v[:, g * HEADS_PER_GROUP + r:g * HEADS_PER_GROUP + r + 1], (rows, LANES))
         for r in range(HEADS_PER_GROUP)]
    first = lane_in_pair[:rows] < SSM_HEAD_DIM
    return jnp.concatenate([jnp.where(first, b[0], b[1]), jnp.where(first, b[2], b[3])], axis=1)


def _ssd_kernel(xbc_ref, u_ref, zs_ref, wdt_ref, dtb_ref, cw_ref, cb_ref, alog_ref, dskip_ref, ng_ref,
                o_ref, ext_ref, sh_ref, act_ref, state_ref, *, q):
    @pl.when(pl.program_id(1) == 0)
    def _():
        ext_ref[0:SSM_HALO, :] = jnp.zeros((SSM_HALO, D_XBC), F32)
        state_ref[...] = jnp.zeros(state_ref.shape, F32)

    @pl.when(pl.program_id(1) > 0)
    def _():
        ext_ref[0:SSM_HALO, :] = ext_ref[q:q + SSM_HALO, :]

    ext_ref[SSM_HALO:SSM_HALO + q, :] = xbc_ref[...].astype(F32)

    base = SSM_HALO - (SSM_CONV_WIDTH - 1)
    for k in range(SSM_CONV_WIDTH - 1):
        sh_ref[k] = ext_ref[base + k:base + k + q, :]
    for lo in range(0, D_XBC, CONV_CW):
        for r0 in range(0, q, CONV_RC):
            cols = slice(lo, lo + CONV_CW)
            acc = cb_ref[:, cols] + cw_ref[SSM_CONV_WIDTH - 1:SSM_CONV_WIDTH, cols] * \
                ext_ref[SSM_HALO + r0:SSM_HALO + r0 + CONV_RC, cols]
            for k in range(SSM_CONV_WIDTH - 1):
                acc = acc + cw_ref[k:k + 1, cols] * sh_ref[k, r0:r0 + CONV_RC, cols]
            act_ref[r0:r0 + CONV_RC, cols] = acc * _sigmoid(acc)

    dt_raw = jnp.dot(u_ref[...], wdt_ref[...], preferred_element_type=F32) + dtb_ref[...]
    dt = jnp.maximum(dt_raw, 0.0) + jnp.log1p(jnp.exp(-jnp.abs(dt_raw)))
    adt = dt * (-jnp.exp(alog_ref[...]))
    row = lax.broadcasted_iota(I32, (q, q), 0)
    col = lax.broadcasted_iota(I32, (q, q), 1)
    causal = row >= col
    tril = jnp.where(causal, 1.0, 0.0).astype(F32)
    acs = jnp.dot(tril, adt, preferred_element_type=F32, precision=lax.Precision.HIGHEST)
    acs_t = acs.T
    dt_t = dt.T
    last = acs[q - 1:q, :]
    exp_acs = jnp.exp(acs)
    dt_decay = dt * jnp.exp(last - acs)
    chunk_decay = jnp.exp(last)
    lane_in_pair = lax.broadcasted_iota(I32, (q, LANES), 1)
    head_of_lane = lax.broadcasted_iota(I32, (1, GROUP_CH), 1) // SSM_HEAD_DIM

    b_off = D_INNER
    c_off = D_INNER + SSM_GROUPS * D_STATE
    for g in range(SSM_GROUPS):
        ch = slice(g * GROUP_CH, (g + 1) * GROUP_CH)
        xg = act_ref[:, ch]
        xg_bf = xg.astype(BF16)
        bg = act_ref[:, b_off + g * D_STATE:b_off + (g + 1) * D_STATE]
        cg = act_ref[:, c_off + g * D_STATE:c_off + (g + 1) * D_STATE].astype(BF16)
        cb = lax.dot_general(cg, bg.astype(BF16), (((1,), (1,)), ((), ())),
                             preferred_element_type=F32)
        ms, xblocks = [], []
        for r in range(HEADS_PER_GROUP):
            h = g * HEADS_PER_GROUP + r
            lmat = jnp.exp(jnp.where(causal, acs[:, h:h + 1] - acs_t[h:h + 1, :], -jnp.inf))
            ms.append((cb * lmat * dt_t[h:h + 1, :]).astype(BF16))
            head_mask = jnp.where(head_of_lane == r, 1.0, 0.0).astype(BF16)
            xblocks.append(xg_bf * head_mask)
        y_diag = jnp.dot(jnp.concatenate(ms, axis=1), jnp.concatenate(xblocks, axis=0),
                         preferred_element_type=F32)
        s_prev = state_ref[g]
        y_off = jnp.dot(cg, s_prev.astype(BF16), preferred_element_type=F32)
        yg = y_diag + y_off * _expand_heads(exp_acs, g, lane_in_pair) + xg * dskip_ref[:, ch]
        xw = (xg * _expand_heads(dt_decay, g, lane_in_pair)).astype(BF16)
        dec = _expand_heads(chunk_decay, g, lane_in_pair)
        state_ref[g] = s_prev * dec + jnp.dot(bg.T.astype(BF16), xw, preferred_element_type=F32)

        yz = yg * zs_ref[:, g * GROUP_CH:(g + 1) * GROUP_CH].astype(F32)
        ms = jnp.mean(yz * yz, axis=-1, keepdims=True)
        yn = yz * lax.rsqrt(ms + RMS_EPS) * ng_ref[:, g * GROUP_CH:(g + 1) * GROUP_CH]
        o_ref[:, g * GROUP_CH:(g + 1) * GROUP_CH] = yn.astype(o_ref.dtype)


def _ssd(xbc, u, zs, w_dt, dtb3, ssm_conv_w, ssm_conv_b3, alog3, dskip3, ng3, layer, bsz, seqlen):
    t = xbc.shape[0]
    q = SSD_Q
    nq = seqlen // q
    tile = lambda width: pl.BlockSpec((q, width), lambda b, i: (b * nq + i, 0))
    vec = lambda width: pl.BlockSpec((None, 1, width), lambda b, i: (layer, 0, 0))
    return pl.pallas_call(
        functools.partial(_ssd_kernel, q=q),
        grid=(bsz, nq),
        in_specs=[tile(D_XBC), tile(D_MODEL), tile(D_INNER),
                  pl.BlockSpec((None, D_MODEL, LANES), lambda b, i: (layer, 0, 0)), vec(LANES),
                  pl.BlockSpec((None, SSM_CONV_WIDTH, D_XBC), lambda b, i: (layer, 0, 0)),
                  vec(D_XBC), vec(LANES), vec(D_INNER), vec(D_INNER)],
        out_specs=tile(D_INNER),
        out_shape=jax.ShapeDtypeStruct((t, D_INNER), BF16),
        scratch_shapes=[pltpu.VMEM((q + SSM_HALO, D_XBC), F32),
                        pltpu.VMEM((SSM_CONV_WIDTH - 1, q, D_XBC), F32),
                        pltpu.VMEM((q, D_XBC), F32),
                        pltpu.VMEM((SSM_GROUPS, D_STATE, GROUP_CH), F32)],
        compiler_params=_params(2),
        name="ssd_mixer",
    )(xbc, u, zs, w_dt, dtb3, ssm_conv_w, ssm_conv_b3, alog3, dskip3, ng3)


def _first_argmax(vals):
    best, idx = vals[0], jnp.zeros(vals[0].shape, I32)
    for j in range(1, len(vals)):
        gt = vals[j] > best
        idx = jnp.where(gt, j, idx)
        best = jnp.where(gt, vals[j], best)
    return idx, best


def _select(idx, vals):
    out = vals[len(vals) - 1]
    for j in range(len(vals) - 2, -1, -1):
        out = jnp.where(idx == j, vals[j], out)
    return out


def _outproj_kernel(yn_ref, y1_ref, gs_ref, x_ref, wso_ref, wo_ref, lg_ref, lb_ref, wr_ref, rb_ref,
                    x1_ref, x1b_ref, eid_ref, rank_ref, wts_ref, cnt_ref,
                    wso_bf, wo_bf, base_ref, *, tm):
    @pl.when(pl.program_id(0) == 0)
    def _():
        wso_bf[...] = wso_ref[...].astype(BF16)
        wo_bf[...] = wo_ref[...].astype(BF16)
        base_ref[...] = jnp.zeros(base_ref.shape, F32)

    y_ssm = jnp.dot(yn_ref[...], wso_bf[...], preferred_element_type=F32)
    merged = y1_ref[...].astype(F32) + gs_ref[...].astype(F32) * y_ssm
    mix = jnp.dot(merged.astype(BF16), wo_bf[...], preferred_element_type=F32)
    x1 = _layer_norm(ALPHA * x_ref[...] + mix, lg_ref[...], lb_ref[...])
    x1_ref[...] = x1
    x1b_ref[...] = x1.astype(BF16)

    logits = lax.dot_general(wr_ref[...], x1, (((1,), (1,)), ((), ())),
                             preferred_element_type=F32, precision=lax.Precision.HIGHEST)
    scores = _sigmoid(logits)
    sel = scores + rb_ref[...]
    sel_rows = [sel[e:e + 1, :] for e in range(N_EXPERTS)]
    sc_rows = [scores[e:e + 1, :] for e in range(N_EXPERTS)]

    gscores = []
    for gi in range(N_EXPERT_GROUPS):
        v = sel_rows[gi * EXPERTS_PER_GROUP:(gi + 1) * EXPERTS_PER_GROUP]
        best = None
        for a in range(EXPERTS_PER_GROUP):
            for b in range(a + 1, EXPERTS_PER_GROUP):
                s = v[a] + v[b]
                best = s if best is None else jnp.maximum(best, s)
        gscores.append(best)
    grp, _ = _first_argmax(gscores)

    sel_in = [_select(grp, [sel_rows[gi * EXPERTS_PER_GROUP + j] for gi in range(N_EXPERT_GROUPS)])
              for j in range(EXPERTS_PER_GROUP)]
    sc_in = [_select(grp, [sc_rows[gi * EXPERTS_PER_GROUP + j] for gi in range(N_EXPERT_GROUPS)])
             for j in range(EXPERTS_PER_GROUP)]
    i1, _ = _first_argmax(sel_in)
    neg = jnp.full(sel_in[0].shape, -jnp.inf, F32)
    i2, _ = _first_argmax([jnp.where(i1 == j, neg, sel_in[j]) for j in range(EXPERTS_PER_GROUP)])
    s1 = _select(i1, sc_in)
    s2 = _select(i2, sc_in)
    tot = s1 + s2
    e1 = grp * EXPERTS_PER_GROUP + i1
    e2 = grp * EXPERTS_PER_GROUP + i2
    eid_ref[0:1, :] = e1
    eid_ref[1:2, :] = e2
    wts_ref[0:1, :] = s1 / tot
    wts_ref[1:2, :] = s2 / tot

    eio = lax.broadcasted_iota(I32, (N_EXPERTS, tm), 0)
    oh1 = jnp.where(eio == e1, 1.0, 0.0).astype(F32)
    oh2 = jnp.where(eio == e2, 1.0, 0.0).astype(F32)
    both = oh1 + oh2
    srow = lax.broadcasted_iota(I32, (tm, tm), 0)
    scol = lax.broadcasted_iota(I32, (tm, tm), 1)
    before = jnp.where(srow < scol, 1.0, 0.0).astype(BF16)
    cum = jnp.dot(both.astype(BF16), before, preferred_element_type=F32)
    pos = base_ref[:, 0:1] + cum
    rank_ref[0:1, :] = jnp.sum(oh1 * pos, axis=0, keepdims=True).astype(I32)
    rank_ref[1:2, :] = jnp.sum(oh2 * pos, axis=0, keepdims=True).astype(I32)
    base_ref[...] = base_ref[...] + jnp.sum(both, axis=1, keepdims=True)
    cnt_ref[...] = base_ref[...].astype(I32)


def _outproj(yn, y1g, gates, x, w_ssm_out, w_out, ln_g3, ln_b3, wr_t, rbias, layer, tm=512):
    t = x.shape[0]
    tm = min(tm, t)
    tile = lambda width: pl.BlockSpec((tm, width), lambda m: (m, 0))
    vec = pl.BlockSpec((None, 1, D_MODEL), lambda m: (layer, 0, 0))
    pair = pl.BlockSpec((2, tm), lambda m: (0, m))
    return pl.pallas_call(
        functools.partial(_outproj_kernel, tm=tm),
        grid=(t // tm,),
        in_specs=[tile(D_INNER), tile(D_MODEL),
                  pl.BlockSpec((tm, D_MODEL), lambda m: (m, 1)),
                  tile(D_MODEL),
                  pl.BlockSpec((None, D_INNER, D_MODEL), lambda m: (layer, 0, 0)),
                  pl.BlockSpec((None, D_MODEL, D_MODEL), lambda m: (layer, 0, 0)),
                  vec, vec,
                  pl.BlockSpec((N_EXPERTS, D_MODEL), lambda m: (0, 0)),
                  pl.BlockSpec((N_EXPERTS, 1), lambda m: (0, 0))],
        out_specs=[tile(D_MODEL), tile(D_MODEL), pair, pair, pair,
                   pl.BlockSpec((N_EXPERTS, LANES), lambda m: (0, 0))],
        out_shape=[jax.ShapeDtypeStruct((t, D_MODEL), F32),
                   jax.ShapeDtypeStruct((t, D_MODEL), BF16),
                   jax.ShapeDtypeStruct((2, t), I32),
                   jax.ShapeDtypeStruct((2, t), I32),
                   jax.ShapeDtypeStruct((2, t), F32),
                   jax.ShapeDtypeStruct((N_EXPERTS, LANES), I32)],
        scratch_shapes=[pltpu.VMEM((D_INNER, D_MODEL), BF16),
                        pltpu.VMEM((D_MODEL, D_MODEL), BF16),
                        pltpu.VMEM((N_EXPERTS, LANES), F32)],
        compiler_params=_params(1),
        name="outproj_ln_router",
    )(yn, y1g, gates, x, w_ssm_out, w_out, ln_g3, ln_b3, wr_t, rbias)


def _row_copy(src_ref, src_row, dst_ref, dst_row, sem):
    return pltpu.make_async_copy(src_ref.at[pl.ds(src_row, 1)], dst_ref.at[pl.ds(dst_row, 1)], sem)


def _dispatch_kernel(fill_ref, dest_ref, x_ref, xs_hbm, zeros_ref, sem, blk_sem, *, tm, n_blk):
    def zero_row(row):
        return _row_copy(zeros_ref, 0, xs_hbm, row, sem)

    def zero_block(b):
        return pltpu.make_async_copy(
            zeros_ref, xs_hbm.at[pl.ds(pl.multiple_of(b * MOE_ROWS, MOE_ROWS), MOE_ROWS)], blk_sem)

    @pl.when(pl.program_id(0) == 0)
    def _():
        zeros_ref[...] = jnp.zeros(zeros_ref.shape, F32)
        n_active = fill_ref[2 * N_EXPERTS]
        for e in range(N_EXPERTS):
            pad_start = fill_ref[e]
            lax.fori_loop(0, fill_ref[N_EXPERTS + e],
                          lambda j, c: (zero_row(pad_start + j).start(), c)[1], 0)
        lax.fori_loop(n_active, n_blk, lambda b, c: (zero_block(b).start(), c)[1], 0)
        lax.fori_loop(0, fill_ref[2 * N_EXPERTS + 1], lambda j, c: (zero_row(0).wait(), c)[1], 0)
        lax.fori_loop(n_active, n_blk, lambda b, c: (zero_block(b).wait(), c)[1], 0)

    def issue(j8, carry):
        j0 = pl.multiple_of(j8 * 8, 8)
        for c in range(8):
            _row_copy(x_ref, j0 + c, xs_hbm, dest_ref[0, 0, 2 * j0 + 2 * c], sem).start()
            _row_copy(x_ref, j0 + c, xs_hbm, dest_ref[0, 0, 2 * j0 + 2 * c + 1], sem).start()
        return carry

    lax.fori_loop(0, tm // 8, issue, 0)
    for _ in range(2):
        pltpu.make_async_copy(x_ref, xs_hbm.at[pl.ds(0, tm)], sem).wait()


def _dispatch(fill, dest_tiles, x1, n_blk, tm):
    t = x1.shape[0]
    grid_spec = pltpu.PrefetchScalarGridSpec(
        num_scalar_prefetch=1,
        grid=(t // tm,),
        in_specs=[pl.BlockSpec((1, 1, 2 * tm), lambda m, fill: (m, 0, 0), memory_space=pltpu.SMEM),
                  pl.BlockSpec((tm, D_MODEL), lambda m, fill: (m, 0))],
        out_specs=pl.BlockSpec(memory_space=pl.ANY),
        scratch_shapes=[pltpu.VMEM((MOE_ROWS, D_MODEL), F32),
                        pltpu.SemaphoreType.DMA(()), pltpu.SemaphoreType.DMA(())],
    )
    return pl.pallas_call(
        functools.partial(_dispatch_kernel, tm=tm, n_blk=n_blk),
        grid_spec=grid_spec,
        out_shape=jax.ShapeDtypeStruct((n_blk * MOE_ROWS, D_MODEL), F32),
        compiler_params=_params(1),
        name="moe_dispatch",
    )(fill, dest_tiles, x1)


def _expert_kernel(src_ref, exp_ref, nvalid_ref, xs_ref, wg_ref, wu_ref, wd_ref, y_ref,
                   wgu_bf, wd_bf):
    i = pl.program_id(0)
    nvalid = nvalid_ref[i]
    changed = (i == 0) | (exp_ref[i] != exp_ref[jnp.maximum(i - 1, 0)])

    @pl.when((nvalid > 0) & changed)
    def _():
        wgu_bf[:, :D_EXPERT] = wg_ref[...].astype(BF16)
        wgu_bf[:, D_EXPERT:] = wu_ref[...].astype(BF16)
        wd_bf[...] = wd_ref[...].astype(BF16)

    @pl.when(nvalid > 0)
    def _():
        rows = lax.broadcasted_iota(I32, (MOE_ROWS, 1), 0)
        x = jnp.where(rows < nvalid, xs_ref[...], 0.0).astype(BF16)
        gu = jnp.dot(x, wgu_bf[...], preferred_element_type=F32)
        hg = gu[:, :D_EXPERT]
        h = hg * _sigmoid(hg) * gu[:, D_EXPERT:]
        y_ref[...] = jnp.dot(h.astype(BF16), wd_bf[...], preferred_element_type=F32)

    @pl.when(nvalid == 0)
    def _():
        y_ref[...] = jnp.zeros(y_ref.shape, F32)


def _experts(blk_src, blk_exp, blk_nvalid, xs, wg, wu, wd, layer):
    n_rows = xs.shape[0]
    n_blk = n_rows // MOE_ROWS
    grid_spec = pltpu.PrefetchScalarGridSpec(
        num_scalar_prefetch=3,
        grid=(n_blk,),
        in_specs=[pl.BlockSpec((MOE_ROWS, D_MODEL), lambda i, src, exp, nv: (src[i], 0)),
                  pl.BlockSpec((None, None, D_MODEL, D_EXPERT),
                               lambda i, src, exp, nv: (layer, exp[i], 0, 0)),
                  pl.BlockSpec((None, None, D_MODEL, D_EXPERT),
                               lambda i, src, exp, nv: (layer, exp[i], 0, 0)),
                  pl.BlockSpec((None, None, D_EXPERT, D_MODEL),
                               lambda i, src, exp, nv: (layer, exp[i], 0, 0))],
        out_specs=pl.BlockSpec((MOE_ROWS, D_MODEL), lambda i, src, exp, nv: (i, 0)),
        scratch_shapes=[pltpu.VMEM((D_MODEL, 2 * D_EXPERT), BF16),
                        pltpu.VMEM((D_EXPERT, D_MODEL), BF16)],
    )
    return pl.pallas_call(
        _expert_kernel,
        grid_spec=grid_spec,
        out_shape=jax.ShapeDtypeStruct((n_rows, D_MODEL), F32),
        compiler_params=_params(1),
        name="moe_experts",
    )(blk_src, blk_exp, blk_nvalid, xs, wg, wu, wd)


def _moe_plan(eid, rank, counts, t):
    cnt = counts[:, 0]
    nblk_e = (cnt + MOE_ROWS - 1) // MOE_ROWS
    blk_end = jnp.cumsum(nblk_e)
    blk_start = blk_end - nblk_e
    n_active = blk_end[N_EXPERTS - 1]
    experts = jnp.arange(N_EXPERTS, dtype=I32)
    row_start = blk_start * MOE_ROWS
    dest = rank + jnp.sum(jnp.where(eid[None] == experts[:, None, None],
                                    row_start[:, None, None], 0), axis=0)
    n_blk = (2 * t) // MOE_ROWS + N_EXPERTS
    ids = jnp.arange(n_blk, dtype=I32)
    src = jnp.minimum(ids, n_active - 1)
    exp = jnp.minimum(jnp.sum((blk_end[None, :] <= src[:, None]).astype(I32), axis=1), N_EXPERTS - 1)
    onehot = (exp[:, None] == experts[None, :]).astype(I32)
    pick = lambda table: jnp.sum(onehot * table[None, :], axis=1)
    left = pick(cnt) - (src - pick(blk_start)) * MOE_ROWS
    nvalid = jnp.where(ids < n_active, jnp.clip(left, 0, MOE_ROWS), 0)
    pad_len = nblk_e * MOE_ROWS - cnt
    fill = jnp.concatenate([blk_start * MOE_ROWS + cnt, pad_len,
                            jnp.stack([n_active, jnp.sum(pad_len)])]).astype(I32)
    return dest.astype(I32), src.astype(I32), exp, nvalid.astype(I32), fill, n_blk


def _final_kernel(dest_ref, dnext_ref, x1_ref, x1b_ref, p_ref, wt_ref, wpu_ref, wpg_ref, bpg_ref,
                  lg_ref, lb_ref, y_hbm, x2_ref, x2b_ref, g_ref, wpu_bf, wpg_bf, sems, *, tm, n_tiles):
    i = pl.program_id(0)
    slot = lax.rem(i, 2)

    def gather(d_ref, s):
        def issue(j, carry):
            _row_copy(y_hbm, d_ref[0, 0, 2 * j], g_ref.at[s, 0], j, sems.at[s]).start()
            _row_copy(y_hbm, d_ref[0, 0, 2 * j + 1], g_ref.at[s, 1], j, sems.at[s]).start()
            return carry

        lax.fori_loop(0, tm, issue, 0, unroll=8)

    @pl.when(i == 0)
    def _():
        wpu_bf[...] = wpu_ref[...].astype(BF16)
        wpg_bf[...] = wpg_ref[...].astype(BF16)
        gather(dest_ref, 0)

    @pl.when(i + 1 < n_tiles)
    def _():
        gather(dnext_ref, 1 - slot)

    up = jnp.dot(p_ref[...].astype(BF16), wpu_bf[...], preferred_element_type=F32)
    gate = _sigmoid(jnp.dot(x1b_ref[...], wpg_bf[...], preferred_element_type=F32) + bpg_ref[...])
    resid = ALPHA * x1_ref[...] + up * gate

    for k in range(2):
        pltpu.make_async_copy(y_hbm.at[pl.ds(0, tm)], g_ref.at[slot, k], sems.at[slot]).wait()

    wt = wt_ref[...]
    moe = wt[:, 0:1] * g_ref[slot, 0] + wt[:, 1:2] * g_ref[slot, 1]
    x2 = _layer_norm(resid + moe, lg_ref[...], lb_ref[...])
    x2_ref[...] = x2
    x2b_ref[...] = x2.astype(BF16)


def _final(dest_tiles, x1, x1b, p, wt_tok, w_ple_up, w_ple_gate, b_pg3, ln_g3, ln_b3, y_rows, layer, tm):
    t = x1.shape[0]
    n_tiles = t // tm
    tile = lambda width: pl.BlockSpec((tm, width), lambda m: (m, 0))
    vec = pl.BlockSpec((None, 1, D_MODEL), lambda m: (layer, 0, 0))
    return pl.pallas_call(
        functools.partial(_final_kernel, tm=tm, n_tiles=n_tiles),
        grid=(n_tiles,),
        in_specs=[pl.BlockSpec((1, 1, 2 * tm), lambda m: (m, 0, 0), memory_space=pltpu.SMEM),
                  pl.BlockSpec((1, 1, 2 * tm), lambda m: (jnp.minimum(m + 1, n_tiles - 1), 0, 0),
                               memory_space=pltpu.SMEM),
                  tile(D_MODEL), tile(D_MODEL),
                  pl.BlockSpec((None, tm, PLE_DIM), lambda m: (layer, m, 0)),
                  tile(2),
                  pl.BlockSpec((None, PLE_DIM, D_MODEL), lambda m: (layer, 0, 0)),
                  pl.BlockSpec((None, D_MODEL, D_MODEL), lambda m: (layer, 0, 0)),
                  vec, vec, vec,
                  pl.BlockSpec(memory_space=pl.ANY)],
        out_specs=[tile(D_MODEL), tile(D_MODEL)],
        out_shape=[jax.ShapeDtypeStruct((t, D_MODEL), F32),
                   jax.ShapeDtypeStruct((t, D_MODEL), BF16)],
        scratch_shapes=[pltpu.VMEM((2, 2, tm, D_MODEL), F32),
                        pltpu.VMEM((PLE_DIM, D_MODEL), BF16), pltpu.VMEM((D_MODEL, D_MODEL), BF16),
                        pltpu.SemaphoreType.DMA((2,))],
        compiler_params=_params(1),
        name="combine_ple_ln",
    )(dest_tiles, dest_tiles, x1, x1b, p, wt_tok, w_ple_up, w_ple_gate, b_pg3, ln_g3, ln_b3, y_rows)


def kernel(x, p, w_in, b_glu, b_branch_gate, conv_w, conv_b, conv_ln_g, conv_ln_b, w_conv_out,
           ssm_conv_w, ssm_conv_b, dt_bias, a_log, d_skip, ssm_norm_g, w_ssm_out, w_out,
           ln1_g, ln1_b, w_router, router_bias, w_exp_gate, w_exp_up, w_exp_down,
           w_ple_up, w_ple_gate, b_ple_gate, ln2_g, ln2_b):
    bsz, seqlen, d = x.shape
    depth = w_in.shape[0]
    t = bsz * seqlen
    tm_disp = min(512, t)
    tm_comb = min(256, t)

    row3 = lambda a: a.reshape(a.shape[0], 1, a.shape[1])
    pad_lanes = lambda a: jnp.pad(a, ((0, 0), (0, LANES - a.shape[1])))
    b_glu3, b_gate3 = row3(b_glu), row3(b_branch_gate)
    conv_b3, cln_g3, cln_b3 = row3(conv_b), row3(conv_ln_g), row3(conv_ln_b)
    ssm_conv_b3, ng3 = row3(ssm_conv_b), row3(ssm_norm_g)
    dtb3, alog3 = row3(pad_lanes(dt_bias)), row3(pad_lanes(a_log))
    dskip3 = row3(jnp.repeat(d_skip, SSM_HEAD_DIM, axis=1))
    ln1_g3, ln1_b3, ln2_g3, ln2_b3 = row3(ln1_g), row3(ln1_b), row3(ln2_g), row3(ln2_b)
    b_pg3 = row3(b_ple_gate)
    w_in_t = jnp.swapaxes(w_in, 1, 2)
    w_gate, w_dt = _tail_weights(w_in_t)
    wr_t = w_router.T
    rbias = router_bias.reshape(N_EXPERTS, 1)
    p2 = p.reshape(depth, t, PLE_DIM)

    xf = x.reshape(t, d)
    xb = xf.astype(BF16)
    for i in range(depth):
        c = _glu(xb, w_in_t, b_glu3, i)
        zs = _matmul(xb, w_in_t, lambda n: (i, OFF_Z // 1024 + n, 0), None, None,
                     D_INNER, 1024, "silu", BF16, w_rows_are_outputs=True)
        xbc = _matmul(xb, w_in_t, lambda n: (i, OFF_XBC // 1024 + n, 0), None, None,
                      D_XBC, 1024, "none", BF16, w_rows_are_outputs=True)
        gates = _matmul(xb, w_gate, lambda n: (i, 0, n), b_gate3, lambda n: (i, 0, n),
                        2 * D_MODEL, 1024, "sigmoid", BF16)
        y1g = _convbranch(c, gates, conv_w, conv_b3, cln_g3, cln_b3, w_conv_out, i, bsz, seqlen)
        yn = _ssd(xbc, xb, zs, w_dt, dtb3, ssm_conv_w, ssm_conv_b3, alog3, dskip3, ng3, i, bsz, seqlen)
        x1, x1b, eid, rank, wts, counts = _outproj(yn, y1g, gates, xf, w_ssm_out, w_out,
                                                   ln1_g3, ln1_b3, wr_t, rbias, i)
        dest, blk_src, blk_exp, blk_nvalid, fill, n_blk = _moe_plan(eid, rank, counts, t)
        dest_tok = dest.T
        xs = _dispatch(fill, dest_tok.reshape(t // tm_disp, 1, 2 * tm_disp), x1, n_blk, tm_disp)
        y_rows = _experts(blk_src, blk_exp, blk_nvalid, xs, w_exp_gate, w_exp_up, w_exp_down, i)
        xf, xb = _final(dest_tok.reshape(t // tm_comb, 1, 2 * tm_comb), x1, x1b, p2, wts.T,
                        w_ple_up, w_ple_gate, b_pg3, ln2_g3, ln2_b3, y_rows, i, tm_comb)
    return xf.reshape(bsz, seqlen, d)
```

```python
import functools

import jax
import jax.numpy as jnp
from jax import lax
from jax.experimental import pallas as pl
from jax.experimental.pallas import tpu as pltpu

F32 = jnp.float32
BF16 = jnp.bfloat16
I32 = jnp.int32

D_MODEL = 1024
D_CONV = 1024
CONV_WIDTH = 31
D_INNER = 2048
SSM_HEAD_DIM = 64
SSM_HEADS = 32
SSM_GROUPS = 8
HEADS_PER_GROUP = 4
D_STATE = 128
SSM_CONV_WIDTH = 4
D_XBC = D_INNER + 2 * SSM_GROUPS * D_STATE
GROUP_CH = HEADS_PER_GROUP * SSM_HEAD_DIM
N_EXPERTS = 16
N_EXPERT_GROUPS = 4
EXPERTS_PER_GROUP = 4
D_EXPERT = 512
PLE_DIM = 256
DEPTH = 4
ALPHA = (2.0 * DEPTH) ** 0.25
LN_EPS = 1e-5
RMS_EPS = 1e-5

LANES = 128
CONV_HALO = 32
CONV_RC = 64
CONV_CW = 256
SSM_HALO = 8
SSD_Q = 128
MOE_ROWS = 512
MOE_TILE = 512
SEGMENT_BITS = (64, 32, 16, 8, 4, 2, 1)
VMEM_LIMIT = 48 * 1024 * 1024
INPROJ_TM = 2048
INPROJ_VMEM_LIMIT = 56 * 1024 * 1024

OFF_GLU = 0
OFF_Z = 2 * D_CONV
OFF_XBC = OFF_Z + D_INNER
OFF_DT = OFF_XBC + D_XBC
OFF_GATE = OFF_DT + SSM_HEADS


def _sigmoid(x):
    return 1.0 / (1.0 + jnp.exp(-x))


def _layer_norm(x, g, b):
    mu = jnp.mean(x, axis=-1, keepdims=True)
    xc = x - mu
    var = jnp.mean(xc * xc, axis=-1, keepdims=True)
    return xc * lax.rsqrt(var + LN_EPS) * g + b


def _params(n_axes, vmem_limit=VMEM_LIMIT):
    return pltpu.CompilerParams(dimension_semantics=("arbitrary",) * n_axes,
                                vmem_limit_bytes=vmem_limit)


def _mm_kernel(x_ref, w_ref, *rest, act, has_bias, w_rows_are_outputs):
    if has_bias:
        b_ref, o_ref, wbf_ref = rest
    else:
        o_ref, wbf_ref = rest

    @pl.when(pl.program_id(1) == 0)
    def _():
        w = w_ref[...]
        wbf_ref[...] = (w.T if w_rows_are_outputs else w).astype(BF16)

    acc = jnp.dot(x_ref[...], wbf_ref[...], preferred_element_type=F32)
    if has_bias:
        acc = acc + b_ref[...]
    if act == "sigmoid":
        acc = _sigmoid(acc)
    elif act == "silu":
        acc = acc * _sigmoid(acc)
    o_ref[...] = acc.astype(o_ref.dtype)


def _matmul(x, w, w_index, bias, b_index, n_out, tn, act, out_dtype, w_rows_are_outputs=False):
    t, k = x.shape
    tm = min(INPROJ_TM, t)
    w_block = (None,) * (w.ndim - 2) + ((tn, k) if w_rows_are_outputs else (k, tn))
    in_specs = [pl.BlockSpec((tm, k), lambda n, m: (m, 0)),
                pl.BlockSpec(w_block, lambda n, m: w_index(n))]
    args = [x, w]
    if bias is not None:
        b_block = (None,) * (bias.ndim - 2) + (1, tn)
        in_specs.append(pl.BlockSpec(b_block, lambda n, m: b_index(n)))
        args.append(bias)
    return pl.pallas_call(
        functools.partial(_mm_kernel, act=act, has_bias=bias is not None,
                          w_rows_are_outputs=w_rows_are_outputs),
        grid=(n_out // tn, t // tm),
        in_specs=in_specs,
        out_specs=pl.BlockSpec((tm, tn), lambda n, m: (m, n)),
        out_shape=jax.ShapeDtypeStruct((t, n_out), out_dtype),
        scratch_shapes=[pltpu.VMEM((k, tn), BF16)],
        compiler_params=_params(2, INPROJ_VMEM_LIMIT),
        name="inproj_" + act,
    )(*args)


def _tail_weights_kernel(wt_hbm, wgate_ref, wdt_ref, buf_ref, sem):
    n_tail = buf_ref.shape[0]
    copy = pltpu.make_async_copy(wt_hbm.at[pl.program_id(0), pl.ds(OFF_DT, n_tail)], buf_ref, sem)
    copy.start()
    copy.wait()
    lane = lax.broadcasted_iota(I32, (D_MODEL, LANES), 1)
    wdt_ref[...] = jnp.where(lane < SSM_HEADS, buf_ref[0:LANES, :].T, 0.0).astype(BF16)
    step = 4 * LANES
    for lo in range(0, 2 * D_MODEL, step):
        rows = slice(SSM_HEADS + lo, SSM_HEADS + lo + step)
        wgate_ref[:, lo:lo + step] = buf_ref[rows, :].T.astype(BF16)


def _tail_weights(w_in_t):
    depth, n_all, k = w_in_t.shape
    n = 2 * D_MODEL
    return pl.pallas_call(
        _tail_weights_kernel,
        grid=(depth,),
        in_specs=[pl.BlockSpec(memory_space=pl.ANY)],
        out_specs=[pl.BlockSpec((None, k, n), lambda l: (l, 0, 0)),
                   pl.BlockSpec((None, k, LANES), lambda l: (l, 0, 0))],
        out_shape=[jax.ShapeDtypeStruct((depth, k, n), BF16),
                   jax.ShapeDtypeStruct((depth, k, LANES), BF16)],
        scratch_shapes=[pltpu.VMEM((n_all - OFF_DT, k), F32), pltpu.SemaphoreType.DMA(())],
        compiler_params=_params(1),
        name="tail_weights",
    )(w_in_t)


def _glu_kernel(x_ref, wa_ref, wg_ref, ba_ref, bg_ref, o_ref, wa_bf, wg_bf):
    @pl.when(pl.program_id(1) == 0)
    def _():
        wa_bf[...] = wa_ref[...].T.astype(BF16)
        wg_bf[...] = wg_ref[...].T.astype(BF16)

    x = x_ref[...]
    a = jnp.dot(x, wa_bf[...], preferred_element_type=F32) + ba_ref[...]
    g = jnp.dot(x, wg_bf[...], preferred_element_type=F32) + bg_ref[...]
    o_ref[...] = (a * _sigmoid(g)).astype(o_ref.dtype)


def _glu(x, w_in_t, b_glu3, layer, tn=512):
    t, k = x.shape
    tm = min(INPROJ_TM, t)
    half = D_CONV // tn
    return pl.pallas_call(
        _glu_kernel,
        grid=(half, t // tm),
        in_specs=[pl.BlockSpec((tm, k), lambda n, m: (m, 0)),
                  pl.BlockSpec((None, tn, k), lambda n, m: (layer, n, 0)),
                  pl.BlockSpec((None, tn, k), lambda n, m: (layer, n + half, 0)),
                  pl.BlockSpec((None, 1, tn), lambda n, m: (layer, 0, n)),
                  pl.BlockSpec((None, 1, tn), lambda n, m: (layer, 0, n + half))],
        out_specs=pl.BlockSpec((tm, tn), lambda n, m: (m, n)),
        out_shape=jax.ShapeDtypeStruct((t, D_CONV), BF16),
        scratch_shapes=[pltpu.VMEM((k, tn), BF16), pltpu.VMEM((k, tn), BF16)],
        compiler_params=_params(2, INPROJ_VMEM_LIMIT),
        name="inproj_glu",
    )(x, w_in_t, w_in_t, b_glu3, b_glu3)


def _convbranch_kernel(c_ref, cw_ref, cb_ref, lg_ref, lb_ref, w_ref, gate_ref, o_ref,
                       ext_ref, sh_ref, conv_ref, wbf_ref, *, tl):
    first = (pl.program_id(0) == 0) & (pl.program_id(1) == 0)

    @pl.when(first)
    def _():
        wbf_ref[...] = w_ref[...].astype(BF16)

    @pl.when(pl.program_id(1) == 0)
    def _():
        ext_ref[0:CONV_HALO, :] = jnp.zeros((CONV_HALO, D_CONV), F32)

    @pl.when(pl.program_id(1) > 0)
    def _():
        ext_ref[0:CONV_HALO, :] = ext_ref[tl:tl + CONV_HALO, :]

    ext_ref[CONV_HALO:CONV_HALO + tl, :] = c_ref[...].astype(F32)

    sh_rows = tl + CONV_HALO - 8
    for s in range(1, 8):
        sh_ref[s - 1] = ext_ref[s:s + sh_rows, :]

    base = CONV_HALO - (CONV_WIDTH - 1)

    n_groups = CONV_RC // 8
    taps_of_shift = [[(a, 8 * a + s - base) for a in range(5) if 0 <= 8 * a + s - base < CONV_WIDTH]
                     for s in range(8)]

    def conv_rows(rc, carry):
        r0 = pl.multiple_of(rc * CONV_RC, CONV_RC)
        for lo in range(0, D_CONV, LANES):
            cols = slice(lo, lo + LANES)
            w = [jnp.broadcast_to(cw_ref[k:k + 1, cols], (8, LANES)) for k in range(CONV_WIDTH)]
            acc = [jnp.broadcast_to(cb_ref[:, cols], (8, LANES))] * n_groups
            for s in range(8):
                taps = taps_of_shift[s]
                for j in range(n_groups + max(a for a, _ in taps)):
                    used = [(a, k) for a, k in taps if 0 <= j - a < n_groups]
                    if not used:
                        continue
                    rows = pl.ds(r0 + 8 * j, 8)
                    x = ext_ref[rows, cols] if s == 0 else sh_ref[s - 1, rows, cols]
                    for a, k in used:
                        acc[j - a] = acc[j - a] + w[k] * x
            for i in range(n_groups):
                conv_ref[pl.ds(r0 + 8 * i, 8), cols] = acc[i]
        return carry

    lax.fori_loop(0, tl // CONV_RC, conv_rows, 0)
    h = _layer_norm(conv_ref[...], lg_ref[...], lb_ref[...])
    h = h * _sigmoid(h)
    y = jnp.dot(h.astype(BF16), wbf_ref[...], preferred_element_type=F32)
    o_ref[...] = (y * gate_ref[...].astype(F32)).astype(o_ref.dtype)


def _convbranch(c, gates, conv_w, conv_b3, ln_g3, ln_b3, w_conv_out, layer, bsz, seqlen, tl=256):
    t = c.shape[0]
    nl = seqlen // tl
    vec = pl.BlockSpec((None, 1, D_CONV), lambda b, i: (layer, 0, 0))
    return pl.pallas_call(
        functools.partial(_convbranch_kernel, tl=tl),
        grid=(bsz, nl),
        in_specs=[pl.BlockSpec((tl, D_CONV), lambda b, i: (b * nl + i, 0)),
                  pl.BlockSpec((None, CONV_WIDTH, D_CONV), lambda b, i: (layer, 0, 0)),
                  vec, vec, vec,
                  pl.BlockSpec((None, D_CONV, D_MODEL), lambda b, i: (layer, 0, 0)),
                  pl.BlockSpec((tl, D_MODEL), lambda b, i: (b * nl + i, 0))],
        out_specs=pl.BlockSpec((tl, D_MODEL), lambda b, i: (b * nl + i, 0)),
        out_shape=jax.ShapeDtypeStruct((t, D_MODEL), BF16),
        scratch_shapes=[pltpu.VMEM((tl + CONV_HALO, D_CONV), F32),
                        pltpu.VMEM((7, tl + CONV_HALO - 8, D_CONV), F32),
                        pltpu.VMEM((tl, D_CONV), F32),
                        pltpu.VMEM((D_CONV, D_MODEL), BF16)],
        compiler_params=_params(2),
        name="conv_module",
    )(c, conv_w, conv_b3, ln_g3, ln_b3, w_conv_out, gates)


def _expand_heads(v, g, lane_in_pair):
    rows = v.shape[0]
    b = [jnp.broadcast_to(v[:, g * HEADS_PER_GROUP + r:g * HEADS_PER_GROUP + r + 1], (rows, LANES))
         for r in range(HEADS_PER_GROUP)]
    first = lane_in_pair[:rows] < SSM_HEAD_DIM
    return jnp.concatenate([jnp.where(first, b[0], b[1]), jnp.where(first, b[2], b[3])], axis=1)


def _ssd_kernel(xbc_ref, u_ref, zs_ref, wdt_ref, dtb_ref, cw_ref, cb_ref, alog_ref, dskip_ref, ng_ref,
                o_ref, ext_ref, sh_ref, act_ref, state_ref, *, q):
    @pl.when(pl.program_id(1) == 0)
    def _():
        ext_ref[0:SSM_HALO, :] = jnp.zeros((SSM_HALO, D_XBC), F32)
        state_ref[...] = jnp.zeros(state_ref.shape, F32)

    @pl.when(pl.program_id(1) > 0)
    def _():
        ext_ref[0:SSM_HALO, :] = ext_ref[q:q + SSM_HALO, :]

    ext_ref[SSM_HALO:SSM_HALO + q, :] = xbc_ref[...].astype(F32)

    base = SSM_HALO - (SSM_CONV_WIDTH - 1)
    for k in range(SSM_CONV_WIDTH - 1):
        sh_ref[k] = ext_ref[base + k:base + k + q, :]
    for lo in range(0, D_XBC, CONV_CW):
        for r0 in range(0, q, CONV_RC):
            cols = slice(lo, lo + CONV_CW)
            acc = cb_ref[:, cols] + cw_ref[SSM_CONV_WIDTH - 1:SSM_CONV_WIDTH, cols] * \
                ext_ref[SSM_HALO + r0:SSM_HALO + r0 + CONV_RC, cols]
            for k in range(SSM_CONV_WIDTH - 1):
                acc = acc + cw_ref[k:k + 1, cols] * sh_ref[k, r0:r0 + CONV_RC, cols]
            act_ref[r0:r0 + CONV_RC, cols] = acc * _sigmoid(acc)

    dt_raw = jnp.dot(u_ref[...], wdt_ref[...], preferred_element_type=F32) + dtb_ref[...]
    dt = jnp.maximum(dt_raw, 0.0) + jnp.log1p(jnp.exp(-jnp.abs(dt_raw)))
    adt = dt * (-jnp.exp(alog_ref[...]))
    row = lax.broadcasted_iota(I32, (q, q), 0)
    col = lax.broadcasted_iota(I32, (q, q), 1)
    causal = row >= col
    tril = jnp.where(causal, 1.0, 0.0).astype(F32)
    acs = jnp.dot(tril, adt, preferred_element_type=F32, precision=lax.Precision.HIGHEST)
    acs_t = acs.T
    dt_t = dt.T
    last = acs[q - 1:q, :]
    exp_acs = jnp.exp(acs)
    dt_decay = dt * jnp.exp(last - acs)
    chunk_decay = jnp.exp(last)
    lane_in_pair = lax.broadcasted_iota(I32, (q, LANES), 1)
    head_of_lane = lax.broadcasted_iota(I32, (1, GROUP_CH), 1) // SSM_HEAD_DIM

    b_off = D_INNER
    c_off = D_INNER + SSM_GROUPS * D_STATE
    for g in range(SSM_GROUPS):
        ch = slice(g * GROUP_CH, (g + 1) * GROUP_CH)
        xg = act_ref[:, ch]
        xg_bf = xg.astype(BF16)
        bg = act_ref[:, b_off + g * D_STATE:b_off + (g + 1) * D_STATE]
        cg = act_ref[:, c_off + g * D_STATE:c_off + (g + 1) * D_STATE].astype(BF16)
        cb = lax.dot_general(cg, bg.astype(BF16), (((1,), (1,)), ((), ())),
                             preferred_element_type=F32)
        ms, xblocks = [], []
        for r in range(HEADS_PER_GROUP):
            h = g * HEADS_PER_GROUP + r
            lmat = jnp.exp(jnp.where(causal, acs[:, h:h + 1] - acs_t[h:h + 1, :], -jnp.inf))
            ms.append((cb * lmat * dt_t[h:h + 1, :]).astype(BF16))
            head_mask = jnp.where(head_of_lane == r, 1.0, 0.0).astype(BF16)
            xblocks.append(xg_bf * head_mask)
        y_diag = jnp.dot(jnp.concatenate(ms, axis=1), jnp.concatenate(xblocks, axis=0),
                         preferred_element_type=F32)
        s_prev = state_ref[g]
        y_off = jnp.dot(cg, s_prev.astype(BF16), preferred_element_type=F32)
        yg = y_diag + y_off * _expand_heads(exp_acs, g, lane_in_pair) + xg * dskip_ref[:, ch]
        xw = (xg * _expand_heads(dt_decay, g, lane_in_pair)).astype(BF16)
        dec = _expand_heads(chunk_decay, g, lane_in_pair)
        state_ref[g] = s_prev * dec + jnp.dot(bg.T.astype(BF16), xw, preferred_element_type=F32)

        yz = yg * zs_ref[:, g * GROUP_CH:(g + 1) * GROUP_CH].astype(F32)
        ms = jnp.mean(yz * yz, axis=-1, keepdims=True)
        yn = yz * lax.rsqrt(ms + RMS_EPS) * ng_ref[:, g * GROUP_CH:(g + 1) * GROUP_CH]
        o_ref[:, g * GROUP_CH:(g + 1) * GROUP_CH] = yn.astype(o_ref.dtype)


def _ssd(xbc, u, zs, w_dt, dtb3, ssm_conv_w, ssm_conv_b3, alog3, dskip3, ng3, layer, bsz, seqlen):
    t = xbc.shape[0]
    q = SSD_Q
    nq = seqlen // q
    tile = lambda width: pl.BlockSpec((q, width), lambda b, i: (b * nq + i, 0))
    vec = lambda width: pl.BlockSpec((None, 1, width), lambda b, i: (layer, 0, 0))
    return pl.pallas_call(
        functools.partial(_ssd_kernel, q=q),
        grid=(bsz, nq),
        in_specs=[tile(D_XBC), tile(D_MODEL), tile(D_INNER),
                  pl.BlockSpec((None, D_MODEL, LANES), lambda b, i: (layer, 0, 0)), vec(LANES),
                  pl.BlockSpec((None, SSM_CONV_WIDTH, D_XBC), lambda b, i: (layer, 0, 0)),
                  vec(D_XBC), vec(LANES), vec(D_INNER), vec(D_INNER)],
        out_specs=tile(D_INNER),
        out_shape=jax.ShapeDtypeStruct((t, D_INNER), BF16),
        scratch_shapes=[pltpu.VMEM((q + SSM_HALO, D_XBC), F32),
                        pltpu.VMEM((SSM_CONV_WIDTH - 1, q, D_XBC), F32),
                        pltpu.VMEM((q, D_XBC), F32),
                        pltpu.VMEM((SSM_GROUPS, D_STATE, GROUP_CH), F32)],
        compiler_params=_params(2),
        name="ssd_mixer",
    )(xbc, u, zs, w_dt, dtb3, ssm_conv_w, ssm_conv_b3, alog3, dskip3, ng3)


def _first_argmax(vals):
    best, idx = vals[0], jnp.zeros(vals[0].shape, I32)
    for j in range(1, len(vals)):
        gt = vals[j] > best
        idx = jnp.where(gt, j, idx)
        best = jnp.where(gt, vals[j], best)
    return idx, best


def _select(idx, vals):
    out = vals[len(vals) - 1]
    for j in range(len(vals) - 2, -1, -1):
        out = jnp.where(idx == j, vals[j], out)
    return out


def _outproj_kernel(yn_ref, y1_ref, gs_ref, x_ref, wso_ref, wo_ref, lg_ref, lb_ref, wr_ref, rb_ref,
                    x1_ref, x1b_ref, eid_ref, rank_ref, pos_ref, wts_ref, cnt_ref, seg_ref,
                    wso_bf, wo_bf, base_ref, *, tm):
    @pl.when(pl.program_id(0) == 0)
    def _():
        wso_bf[...] = wso_ref[...].astype(BF16)
        wo_bf[...] = wo_ref[...].astype(BF16)
        base_ref[...] = jnp.zeros(base_ref.shape, F32)

    y_ssm = jnp.dot(yn_ref[...], wso_bf[...], preferred_element_type=F32)
    merged = y1_ref[...].astype(F32) + gs_ref[...].astype(F32) * y_ssm
    mix = jnp.dot(merged.astype(BF16), wo_bf[...], preferred_element_type=F32)
    x1 = _layer_norm(ALPHA * x_ref[...] + mix, lg_ref[...], lb_ref[...])
    x1_ref[...] = x1
    x1b_ref[...] = x1.astype(BF16)

    logits = lax.dot_general(wr_ref[...], x1, (((1,), (1,)), ((), ())),
                             preferred_element_type=F32, precision=lax.Precision.HIGHEST)
    scores = _sigmoid(logits)
    sel = scores + rb_ref[...]
    sel_rows = [sel[e:e + 1, :] for e in range(N_EXPERTS)]
    sc_rows = [scores[e:e + 1, :] for e in range(N_EXPERTS)]

    gscores = []
    for gi in range(N_EXPERT_GROUPS):
        v = sel_rows[gi * EXPERTS_PER_GROUP:(gi + 1) * EXPERTS_PER_GROUP]
        best = None
        for a in range(EXPERTS_PER_GROUP):
            for b in range(a + 1, EXPERTS_PER_GROUP):
                s = v[a] + v[b]
                best = s if best is None else jnp.maximum(best, s)
        gscores.append(best)
    grp, _ = _first_argmax(gscores)

    sel_in = [_select(grp, [sel_rows[gi * EXPERTS_PER_GROUP + j] for gi in range(N_EXPERT_GROUPS)])
              for j in range(EXPERTS_PER_GROUP)]
    sc_in = [_select(grp, [sc_rows[gi * EXPERTS_PER_GROUP + j] for gi in range(N_EXPERT_GROUPS)])
             for j in range(EXPERTS_PER_GROUP)]
    i1, _ = _first_argmax(sel_in)
    neg = jnp.full(sel_in[0].shape, -jnp.inf, F32)
    i2, _ = _first_argmax([jnp.where(i1 == j, neg, sel_in[j]) for j in range(EXPERTS_PER_GROUP)])
    s1 = _select(i1, sc_in)
    s2 = _select(i2, sc_in)
    tot = s1 + s2
    e1 = grp * EXPERTS_PER_GROUP + i1
    e2 = grp * EXPERTS_PER_GROUP + i2
    eid_ref[0:1, :] = e1
    eid_ref[1:2, :] = e2
    wts_ref[0:1, :] = s1 / tot
    wts_ref[1:2, :] = s2 / tot

    eio = lax.broadcasted_iota(I32, (N_EXPERTS, tm), 0)
    oh1 = jnp.where(eio == e1, 1.0, 0.0).astype(F32)
    oh2 = jnp.where(eio == e2, 1.0, 0.0).astype(F32)
    both = oh1 + oh2
    srow = lax.broadcasted_iota(I32, (tm, tm), 0)
    scol = lax.broadcasted_iota(I32, (tm, tm), 1)
    before = jnp.where(srow < scol, 1.0, 0.0).astype(BF16)
    cum = jnp.dot(both.astype(BF16), before, preferred_element_type=F32)
    cnt8 = jnp.floor((jnp.sum(both, axis=1, keepdims=True) + 7.0) * 0.125) * 8.0
    cnt8_l = jnp.broadcast_to(cnt8, (N_EXPERTS, LANES))
    erow = lax.broadcasted_iota(I32, (N_EXPERTS, N_EXPERTS), 0)
    ecol = lax.broadcasted_iota(I32, (N_EXPERTS, N_EXPERTS), 1)
    seg_off = jnp.dot(jnp.where(ecol < erow, 1.0, 0.0).astype(F32), cnt8_l,
                      preferred_element_type=F32, precision=lax.Precision.HIGHEST)
    base = base_ref[...]
    in_tile = seg_off[:, 0:1] + cum
    in_expert = base[:, 0:1] + cum
    pos_ref[0:1, :] = jnp.sum(oh1 * in_tile, axis=0, keepdims=True).astype(I32)
    pos_ref[1:2, :] = jnp.sum(oh2 * in_tile, axis=0, keepdims=True).astype(I32)
    rank_ref[0:1, :] = jnp.sum(oh1 * in_expert, axis=0, keepdims=True).astype(I32)
    rank_ref[1:2, :] = jnp.sum(oh2 * in_expert, axis=0, keepdims=True).astype(I32)
    lane = lax.broadcasted_iota(I32, (N_EXPERTS, LANES), 1)
    seg_ref[...] = jnp.where(lane == 0, seg_off, jnp.where(lane == 1, cnt8_l, base)).astype(I32)
    base_ref[...] = base + cnt8_l
    cnt_ref[...] = base_ref[...].astype(I32)


def _outproj(yn, y1g, gates, x, w_ssm_out, w_out, ln_g3, ln_b3, wr_t, rbias, layer, tm):
    t = x.shape[0]
    tile = lambda width: pl.BlockSpec((tm, width), lambda m: (m, 0))
    vec = pl.BlockSpec((None, 1, D_MODEL), lambda m: (layer, 0, 0))
    pair = pl.BlockSpec((2, tm), lambda m: (0, m))
    return pl.pallas_call(
        functools.partial(_outproj_kernel, tm=tm),
        grid=(t // tm,),
        in_specs=[tile(D_INNER), tile(D_MODEL),
                  pl.BlockSpec((tm, D_MODEL), lambda m: (m, 1)),
                  tile(D_MODEL),
                  pl.BlockSpec((None, D_INNER, D_MODEL), lambda m: (layer, 0, 0)),
                  pl.BlockSpec((None, D_MODEL, D_MODEL), lambda m: (layer, 0, 0)),
                  vec, vec,
                  pl.BlockSpec((N_EXPERTS, D_MODEL), lambda m: (0, 0)),
                  pl.BlockSpec((N_EXPERTS, 1), lambda m: (0, 0))],
        out_specs=[tile(D_MODEL), tile(D_MODEL), pair, pair, pair, pair,
                   pl.BlockSpec((N_EXPERTS, LANES), lambda m: (0, 0)),
                   pl.BlockSpec((None, N_EXPERTS, LANES), lambda m: (m, 0, 0))],
        out_shape=[jax.ShapeDtypeStruct((t, D_MODEL), F32),
                   jax.ShapeDtypeStruct((t, D_MODEL), BF16),
                   jax.ShapeDtypeStruct((2, t), I32),
                   jax.ShapeDtypeStruct((2, t), I32),
                   jax.ShapeDtypeStruct((2, t), I32),
                   jax.ShapeDtypeStruct((2, t), F32),
                   jax.ShapeDtypeStruct((N_EXPERTS, LANES), I32),
                   jax.ShapeDtypeStruct((t // tm, N_EXPERTS, LANES), I32)],
        scratch_shapes=[pltpu.VMEM((D_INNER, D_MODEL), BF16),
                        pltpu.VMEM((D_MODEL, D_MODEL), BF16),
                        pltpu.VMEM((N_EXPERTS, LANES), F32)],
        compiler_params=_params(1),
        name="outproj_ln_router",
    )(yn, y1g, gates, x, w_ssm_out, w_out, ln_g3, ln_b3, wr_t, rbias)


def _row_copy(src_ref, src_row, dst_ref, dst_row, sem):
    return pltpu.make_async_copy(src_ref.at[pl.ds(src_row, 1)], dst_ref.at[pl.ds(dst_row, 1)], sem)


def _dispatch_kernel(fill_ref, segoff_ref, cnt8_ref, gdst_ref, pos_ref, x_ref, xs_hbm,
                     sorted_ref, zeros_ref, sem, blk_sem, seg_sems, *, tm, n_blk, n_tiles):
    m = pl.program_id(0)
    slot = lax.rem(m, 2)

    def zero_rows(row):
        return pltpu.make_async_copy(zeros_ref.at[pl.ds(0, 8)],
                                     xs_hbm.at[pl.ds(pl.multiple_of(row, 8), 8)], sem)

    def zero_block(b):
        return pltpu.make_async_copy(
            zeros_ref, xs_hbm.at[pl.ds(pl.multiple_of(b * MOE_ROWS, MOE_ROWS), MOE_ROWS)], blk_sem)

    @pl.when(m == 0)
    def _():
        zeros_ref[...] = jnp.zeros(zeros_ref.shape, F32)
        n_active = fill_ref[2 * N_EXPERTS]
        for e in range(N_EXPERTS):
            pad_start = fill_ref[e]
            lax.fori_loop(0, fill_ref[N_EXPERTS + e],
                          lambda j, c: (zero_rows(pad_start + 8 * j).start(), c)[1], 0)
        lax.fori_loop(n_active, n_blk, lambda b, c: (zero_block(b).start(), c)[1], 0)
        lax.fori_loop(0, fill_ref[2 * N_EXPERTS + 1], lambda j, c: (zero_rows(0).wait(), c)[1], 0)
        lax.fori_loop(n_active, n_blk, lambda b, c: (zero_block(b).wait(), c)[1], 0)

    def for_each_segment_block(tile, s, act):
        for e in range(N_EXPERTS):
            idx = tile * N_EXPERTS + e
            n8 = lax.shift_right_logical(cnt8_ref[idx], 3)
            src = segoff_ref[idx]
            dst = gdst_ref[idx]
            for bit in SEGMENT_BITS:
                rows = 8 * bit
                hit = (n8 & bit) != 0

                @pl.when(hit)
                def _(src=src, dst=dst, rows=rows):
                    act(pltpu.make_async_copy(
                        sorted_ref.at[s, pl.ds(pl.multiple_of(src, 8), rows)],
                        xs_hbm.at[pl.ds(pl.multiple_of(dst, 8), rows)], seg_sems.at[s]))

                step = jnp.where(hit, rows, 0)
                src = src + step
                dst = dst + step

    start = lambda copy: copy.start()
    wait = lambda copy: copy.wait()

    @pl.when(m >= 2)
    def _():
        for_each_segment_block(m - 2, slot, wait)

    pos = pos_ref[...]
    srow = lax.broadcasted_iota(I32, (sorted_ref.shape[1], tm), 0)
    onehot = jnp.where(srow == pos[0:1, :], 1.0, jnp.where(srow == pos[1:2, :], 1.0, 0.0)).astype(BF16)
    sorted_ref[slot] = jnp.dot(onehot, x_ref[...], preferred_element_type=F32)
    for_each_segment_block(m, slot, start)

    @pl.when(m == n_tiles - 1)
    def _():
        if n_tiles > 1:
            for_each_segment_block(m - 1, 1 - slot, wait)
        for_each_segment_block(m, slot, wait)


def _dispatch(fill, seg_off, seg_cnt8, seg_dst, pos, x1b, n_blk, tm):
    t = x1b.shape[0]
    n_tiles = t // tm
    sorted_rows = 2 * tm + 8 * N_EXPERTS
    grid_spec = pltpu.PrefetchScalarGridSpec(
        num_scalar_prefetch=4,
        grid=(n_tiles,),
        in_specs=[pl.BlockSpec((2, tm), lambda m, *_: (0, m)),
                  pl.BlockSpec((tm, D_MODEL), lambda m, *_: (m, 0))],
        out_specs=pl.BlockSpec(memory_space=pl.ANY),
        scratch_shapes=[pltpu.VMEM((2, sorted_rows, D_MODEL), F32),
                        pltpu.VMEM((MOE_ROWS, D_MODEL), F32),
                        pltpu.SemaphoreType.DMA(()), pltpu.SemaphoreType.DMA(()),
                        pltpu.SemaphoreType.DMA((2,))],
    )
    return pl.pallas_call(
        functools.partial(_dispatch_kernel, tm=tm, n_blk=n_blk, n_tiles=n_tiles),
        grid_spec=grid_spec,
        out_shape=jax.ShapeDtypeStruct((n_blk * MOE_ROWS, D_MODEL), F32),
        compiler_params=_params(1),
        name="moe_dispatch",
    )(fill, seg_off, seg_cnt8, seg_dst, pos, x1b)


def _expert_kernel(src_ref, exp_ref, nvalid_ref, xs_ref, wg_ref, wu_ref, wd_ref, y_ref,
                   wgu_bf, wd_bf):
    i = pl.program_id(0)
    nvalid = nvalid_ref[i]
    changed = (i == 0) | (exp_ref[i] != exp_ref[jnp.maximum(i - 1, 0)])

    @pl.when((nvalid > 0) & changed)
    def _():
        wgu_bf[:, :D_EXPERT] = wg_ref[...].astype(BF16)
        wgu_bf[:, D_EXPERT:] = wu_ref[...].astype(BF16)
        wd_bf[...] = wd_ref[...].astype(BF16)

    @pl.when(nvalid > 0)
    def _():
        rows = lax.broadcasted_iota(I32, (MOE_ROWS, 1), 0)
        x = jnp.where(rows < nvalid, xs_ref[...], 0.0).astype(BF16)
        gu = jnp.dot(x, wgu_bf[...], preferred_element_type=F32)
        hg = gu[:, :D_EXPERT]
        h = hg * _sigmoid(hg) * gu[:, D_EXPERT:]
        y_ref[...] = jnp.dot(h.astype(BF16), wd_bf[...], preferred_element_type=F32)

    @pl.when(nvalid == 0)
    def _():
        y_ref[...] = jnp.zeros(y_ref.shape, F32)


def _experts(blk_src, blk_exp, blk_nvalid, xs, wg, wu, wd, layer):
    n_rows = xs.shape[0]
    n_blk = n_rows // MOE_ROWS
    grid_spec = pltpu.PrefetchScalarGridSpec(
        num_scalar_prefetch=3,
        grid=(n_blk,),
        in_specs=[pl.BlockSpec((MOE_ROWS, D_MODEL), lambda i, src, exp, nv: (src[i], 0)),
                  pl.BlockSpec((None, None, D_MODEL, D_EXPERT),
                               lambda i, src, exp, nv: (layer, exp[i], 0, 0)),
                  pl.BlockSpec((None, None, D_MODEL, D_EXPERT),
                               lambda i, src, exp, nv: (layer, exp[i], 0, 0)),
                  pl.BlockSpec((None, None, D_EXPERT, D_MODEL),
                               lambda i, src, exp, nv: (layer, exp[i], 0, 0))],
        out_specs=pl.BlockSpec((MOE_ROWS, D_MODEL), lambda i, src, exp, nv: (i, 0)),
        scratch_shapes=[pltpu.VMEM((D_MODEL, 2 * D_EXPERT), BF16),
                        pltpu.VMEM((D_EXPERT, D_MODEL), BF16)],
    )
    return pl.pallas_call(
        _expert_kernel,
        grid_spec=grid_spec,
        out_shape=jax.ShapeDtypeStruct((n_rows, D_MODEL), F32),
        compiler_params=_params(1),
        name="moe_experts",
    )(blk_src, blk_exp, blk_nvalid, xs, wg, wu, wd)


def _moe_plan(eid, rank, counts, seg, t):
    cnt = counts[:, 0]
    nblk_e = (cnt + MOE_ROWS - 1) // MOE_ROWS
    blk_end = jnp.cumsum(nblk_e)
    blk_start = blk_end - nblk_e
    n_active = blk_end[N_EXPERTS - 1]
    experts = jnp.arange(N_EXPERTS, dtype=I32)
    row_start = blk_start * MOE_ROWS
    dest = rank + jnp.sum(jnp.where(eid[None] == experts[:, None, None],
                                    row_start[:, None, None], 0), axis=0)
    n_tiles = seg.shape[0]
    n_blk = -(-(2 * t + 8 * N_EXPERTS * n_tiles) // MOE_ROWS) + N_EXPERTS
    ids = jnp.arange(n_blk, dtype=I32)
    src = jnp.minimum(ids, n_active - 1)
    exp = jnp.minimum(jnp.sum((blk_end[None, :] <= src[:, None]).astype(I32), axis=1), N_EXPERTS - 1)
    onehot = (exp[:, None] == experts[None, :]).astype(I32)
    pick = lambda table: jnp.sum(onehot * table[None, :], axis=1)
    left = pick(cnt) - (src - pick(blk_start)) * MOE_ROWS
    nvalid = jnp.where(ids < n_active, jnp.clip(left, 0, MOE_ROWS), 0)
    pad8 = (nblk_e * MOE_ROWS - cnt) // 8
    fill = jnp.concatenate([row_start + cnt, pad8, jnp.stack([n_active, jnp.sum(pad8)])]).astype(I32)
    seg_off = seg[:, :, 0].reshape(-1)
    seg_cnt8 = seg[:, :, 1].reshape(-1)
    seg_dst = (row_start[None, :] + seg[:, :, 2]).reshape(-1)
    return (dest.astype(I32), src.astype(I32), exp, nvalid.astype(I32), fill,
            seg_off, seg_cnt8, seg_dst, n_blk)


def _final_kernel(dest_ref, dnext_ref, x1_ref, x1b_ref, p_ref, wt_ref, wpu_ref, wpg_ref, bpg_ref,
                  lg_ref, lb_ref, y_hbm, x2_ref, x2b_ref, g_ref, wpu_bf, wpg_bf, sems, *, tm, n_tiles):
    i = pl.program_id(0)
    slot = lax.rem(i, 2)

    def gather(d_ref, s):
        def issue(j, carry):
            _row_copy(y_hbm, d_ref[0, 0, 2 * j], g_ref.at[s, 0], j, sems.at[s]).start()
            _row_copy(y_hbm, d_ref[0, 0, 2 * j + 1], g_ref.at[s, 1], j, sems.at[s]).start()
            return carry

        lax.fori_loop(0, tm, issue, 0, unroll=8)

    @pl.when(i == 0)
    def _():
        wpu_bf[...] = wpu_ref[...].astype(BF16)
        wpg_bf[...] = wpg_ref[...].astype(BF16)
        gather(dest_ref, 0)

    @pl.when(i + 1 < n_tiles)
    def _():
        gather(dnext_ref, 1 - slot)

    up = jnp.dot(p_ref[...].astype(BF16), wpu_bf[...], preferred_element_type=F32)
    gate = _sigmoid(jnp.dot(x1b_ref[...], wpg_bf[...], preferred_element_type=F32) + bpg_ref[...])
    resid = ALPHA * x1_ref[...] + up * gate

    for k in range(2):
        pltpu.make_async_copy(y_hbm.at[pl.ds(0, tm)], g_ref.at[slot, k], sems.at[slot]).wait()

    wt = wt_ref[...]
    moe = wt[:, 0:1] * g_ref[slot, 0] + wt[:, 1:2] * g_ref[slot, 1]
    x2 = _layer_norm(resid + moe, lg_ref[...], lb_ref[...])
    x2_ref[...] = x2
    x2b_ref[...] = x2.astype(BF16)


def _final(dest_tiles, x1, x1b, p, wt_tok, w_ple_up, w_ple_gate, b_pg3, ln_g3, ln_b3, y_rows, layer, tm):
    t = x1.shape[0]
    n_tiles = t // tm
    tile = lambda width: pl.BlockSpec((tm, width), lambda m: (m, 0))
    vec = pl.BlockSpec((None, 1, D_MODEL), lambda m: (layer, 0, 0))
    return pl.pallas_call(
        functools.partial(_final_kernel, tm=tm, n_tiles=n_tiles),
        grid=(n_tiles,),
        in_specs=[pl.BlockSpec((1, 1, 2 * tm), lambda m: (m, 0, 0), memory_space=pltpu.SMEM),
                  pl.BlockSpec((1, 1, 2 * tm), lambda m: (jnp.minimum(m + 1, n_tiles - 1), 0, 0),
                               memory_space=pltpu.SMEM),
                  tile(D_MODEL), tile(D_MODEL),
                  pl.BlockSpec((None, tm, PLE_DIM), lambda m: (layer, m, 0)),
                  tile(2),
                  pl.BlockSpec((None, PLE_DIM, D_MODEL), lambda m: (layer, 0, 0)),
                  pl.BlockSpec((None, D_MODEL, D_MODEL), lambda m: (layer, 0, 0)),
                  vec, vec, vec,
                  pl.BlockSpec(memory_space=pl.ANY)],
        out_specs=[tile(D_MODEL), tile(D_MODEL)],
        out_shape=[jax.ShapeDtypeStruct((t, D_MODEL), F32),
                   jax.ShapeDtypeStruct((t, D_MODEL), BF16)],
        scratch_shapes=[pltpu.VMEM((2, 2, tm, D_MODEL), F32),
                        pltpu.VMEM((PLE_DIM, D_MODEL), BF16), pltpu.VMEM((D_MODEL, D_MODEL), BF16),
                        pltpu.SemaphoreType.DMA((2,))],
        compiler_params=_params(1),
        name="combine_ple_ln",
    )(dest_tiles, dest_tiles, x1, x1b, p, wt_tok, w_ple_up, w_ple_gate, b_pg3, ln_g3, ln_b3, y_rows)


def kernel(x, p, w_in, b_glu, b_branch_gate, conv_w, conv_b, conv_ln_g, conv_ln_b, w_conv_out,
           ssm_conv_w, ssm_conv_b, dt_bias, a_log, d_skip, ssm_norm_g, w_ssm_out, w_out,
           ln1_g, ln1_b, w_router, router_bias, w_exp_gate, w_exp_up, w_exp_down,
           w_ple_up, w_ple_gate, b_ple_gate, ln2_g, ln2_b):
    bsz, seqlen, d = x.shape
    depth = w_in.shape[0]
    t = bsz * seqlen
    tm_moe = min(MOE_TILE, t)
    tm_comb = min(256, t)

    row3 = lambda a: a.reshape(a.shape[0], 1, a.shape[1])
    pad_lanes = lambda a: jnp.pad(a, ((0, 0), (0, LANES - a.shape[1])))
    b_glu3, b_gate3 = row3(b_glu), row3(b_branch_gate)
    conv_b3, cln_g3, cln_b3 = row3(conv_b), row3(conv_ln_g), row3(conv_ln_b)
    ssm_conv_b3, ng3 = row3(ssm_conv_b), row3(ssm_norm_g)
    dtb3, alog3 = row3(pad_lanes(dt_bias)), row3(pad_lanes(a_log))
    dskip3 = row3(jnp.repeat(d_skip, SSM_HEAD_DIM, axis=1))
    ln1_g3, ln1_b3, ln2_g3, ln2_b3 = row3(ln1_g), row3(ln1_b), row3(ln2_g), row3(ln2_b)
    b_pg3 = row3(b_ple_gate)
    w_in_t = jnp.swapaxes(w_in, 1, 2)
    w_gate, w_dt = _tail_weights(w_in_t)
    wr_t = w_router.T
    rbias = router_bias.reshape(N_EXPERTS, 1)
    p2 = p.reshape(depth, t, PLE_DIM)

    xf = x.reshape(t, d)
    xb = xf.astype(BF16)
    for i in range(depth):
        c = _glu(xb, w_in_t, b_glu3, i)
        zs = _matmul(xb, w_in_t, lambda n: (i, OFF_Z // 1024 + n, 0), None, None,
                     D_INNER, 1024, "silu", BF16, w_rows_are_outputs=True)
        xbc = _matmul(xb, w_in_t, lambda n: (i, OFF_XBC // 1024 + n, 0), None, None,
                      D_XBC, 1024, "none", BF16, w_rows_are_outputs=True)
        gates = _matmul(xb, w_gate, lambda n: (i, 0, n), b_gate3, lambda n: (i, 0, n),
                        2 * D_MODEL, 1024, "sigmoid", BF16)
        y1g = _convbranch(c, gates, conv_w, conv_b3, cln_g3, cln_b3, w_conv_out, i, bsz, seqlen)
        yn = _ssd(xbc, xb, zs, w_dt, dtb3, ssm_conv_w, ssm_conv_b3, alog3, dskip3, ng3, i, bsz, seqlen)
        x1, x1b, eid, rank, pos, wts, counts, seg = _outproj(
            yn, y1g, gates, xf, w_ssm_out, w_out, ln1_g3, ln1_b3, wr_t, rbias, i, tm_moe)
        (dest, blk_src, blk_exp, blk_nvalid, fill, seg_off, seg_cnt8, seg_dst,
         n_blk) = _moe_plan(eid, rank, counts, seg, t)
        dest_tok = dest.T
        xs = _dispatch(fill, seg_off, seg_cnt8, seg_dst, pos, x1b, n_blk, tm_moe)
        y_rows = _experts(blk_src, blk_exp, blk_nvalid, xs, w_exp_gate, w_exp_up, w_exp_down, i)
        xf, xb = _final(dest_tok.reshape(t // tm_comb, 1, 2 * tm_comb), x1, x1b, p2, wts.T,
                        w_ple_up, w_ple_gate, b_pg3, ln2_g3, ln2_b3, y_rows, i, tm_comb)
    return xf.reshape(bsz, seqlen, d)
```

```python
import functools

import jax
import jax.numpy as jnp
from jax import lax
from jax.experimental import pallas as pl
from jax.experimental.pallas import tpu as pltpu

F32 = jnp.float32
BF16 = jnp.bfloat16
I32 = jnp.int32

D_MODEL = 1024
D_CONV = 1024
CONV_WIDTH = 31
D_INNER = 2048
SSM_HEAD_DIM = 64
SSM_HEADS = 32
SSM_GROUPS = 8
HEADS_PER_GROUP = 4
D_STATE = 128
SSM_CONV_WIDTH = 4
D_XBC = D_INNER + 2 * SSM_GROUPS * D_STATE
GROUP_CH = HEADS_PER_GROUP * SSM_HEAD_DIM
N_EXPERTS = 16
N_EXPERT_GROUPS = 4
TOP_K = 2
EXPERTS_PER_GROUP = 4
D_EXPERT = 512
PLE_DIM = 256
DEPTH = 4
ALPHA = (2.0 * DEPTH) ** 0.25
LN_EPS = 1e-5
RMS_EPS = 1e-5

LANES = 128
CONV_HALO = 32
CONV_RC = 64
CONV_CW = 256
SSM_HALO = 8
SSD_Q = 128
MOE_ROWS = 512
MOE_TILE = 512
SEGMENT_BITS = (64, 32, 16, 8, 4, 2, 1)
VMEM_LIMIT = 48 * 1024 * 1024
INPROJ_TM = 2048
INPROJ_VMEM_LIMIT = 56 * 1024 * 1024

OFF_GLU = 0
OFF_Z = 2 * D_CONV
OFF_XBC = OFF_Z + D_INNER
OFF_DT = OFF_XBC + D_XBC
OFF_GATE = OFF_DT + SSM_HEADS


def _sigmoid(x):
    return 1.0 / (1.0 + jnp.exp(-x))


def _layer_norm(x, g, b):
    mu = jnp.mean(x, axis=-1, keepdims=True)
    xc = x - mu
    var = jnp.mean(xc * xc, axis=-1, keepdims=True)
    return xc * lax.rsqrt(var + LN_EPS) * g + b


def _params(n_axes, vmem_limit=VMEM_LIMIT):
    return pltpu.CompilerParams(dimension_semantics=("arbitrary",) * n_axes,
                                vmem_limit_bytes=vmem_limit)


def _mm_kernel(x_ref, w_ref, *rest, act, has_bias, w_rows_are_outputs):
    if has_bias:
        b_ref, o_ref, wbf_ref = rest
    else:
        o_ref, wbf_ref = rest

    @pl.when(pl.program_id(1) == 0)
    def _():
        w = w_ref[...]
        wbf_ref[...] = (w.T if w_rows_are_outputs else w).astype(BF16)

    acc = jnp.dot(x_ref[...], wbf_ref[...], preferred_element_type=F32)
    if has_bias:
        acc = acc + b_ref[...]
    if act == "sigmoid":
        acc = _sigmoid(acc)
    elif act == "silu":
        acc = acc * _sigmoid(acc)
    o_ref[...] = acc.astype(o_ref.dtype)


def _matmul(x, w, w_index, bias, b_index, n_out, tn, act, out_dtype, w_rows_are_outputs=False):
    t, k = x.shape
    tm = min(INPROJ_TM, t)
    w_block = (None,) * (w.ndim - 2) + ((tn, k) if w_rows_are_outputs else (k, tn))
    in_specs = [pl.BlockSpec((tm, k), lambda n, m: (m, 0)),
                pl.BlockSpec(w_block, lambda n, m: w_index(n))]
    args = [x, w]
    if bias is not None:
        b_block = (None,) * (bias.ndim - 2) + (1, tn)
        in_specs.append(pl.BlockSpec(b_block, lambda n, m: b_index(n)))
        args.append(bias)
    return pl.pallas_call(
        functools.partial(_mm_kernel, act=act, has_bias=bias is not None,
                          w_rows_are_outputs=w_rows_are_outputs),
        grid=(n_out // tn, t // tm),
        in_specs=in_specs,
        out_specs=pl.BlockSpec((tm, tn), lambda n, m: (m, n)),
        out_shape=jax.ShapeDtypeStruct((t, n_out), out_dtype),
        scratch_shapes=[pltpu.VMEM((k, tn), BF16)],
        compiler_params=_params(2, INPROJ_VMEM_LIMIT),
        name="inproj_" + act,
    )(*args)


def _tail_weights_kernel(wt_hbm, wgate_ref, wdt_ref, buf_ref, sem):
    n_tail = buf_ref.shape[0]
    copy = pltpu.make_async_copy(wt_hbm.at[pl.program_id(0), pl.ds(OFF_DT, n_tail)], buf_ref, sem)
    copy.start()
    copy.wait()
    lane = lax.broadcasted_iota(I32, (D_MODEL, LANES), 1)
    wdt_ref[...] = jnp.where(lane < SSM_HEADS, buf_ref[0:LANES, :].T, 0.0).astype(BF16)
    step = 4 * LANES
    for lo in range(0, 2 * D_MODEL, step):
        rows = slice(SSM_HEADS + lo, SSM_HEADS + lo + step)
        wgate_ref[:, lo:lo + step] = buf_ref[rows, :].T.astype(BF16)


def _tail_weights(w_in_t):
    depth, n_all, k = w_in_t.shape
    n = 2 * D_MODEL
    return pl.pallas_call(
        _tail_weights_kernel,
        grid=(depth,),
        in_specs=[pl.BlockSpec(memory_space=pl.ANY)],
        out_specs=[pl.BlockSpec((None, k, n), lambda l: (l, 0, 0)),
                   pl.BlockSpec((None, k, LANES), lambda l: (l, 0, 0))],
        out_shape=[jax.ShapeDtypeStruct((depth, k, n), BF16),
                   jax.ShapeDtypeStruct((depth, k, LANES), BF16)],
        scratch_shapes=[pltpu.VMEM((n_all - OFF_DT, k), F32), pltpu.SemaphoreType.DMA(())],
        compiler_params=_params(1),
        name="tail_weights",
    )(w_in_t)


def _glu_kernel(x_ref, wa_ref, wg_ref, ba_ref, bg_ref, o_ref, wa_bf, wg_bf):
    @pl.when(pl.program_id(1) == 0)
    def _():
        wa_bf[...] = wa_ref[...].T.astype(BF16)
        wg_bf[...] = wg_ref[...].T.astype(BF16)

    x = x_ref[...]
    a = jnp.dot(x, wa_bf[...], preferred_element_type=F32) + ba_ref[...]
    g = jnp.dot(x, wg_bf[...], preferred_element_type=F32) + bg_ref[...]
    o_ref[...] = (a * _sigmoid(g)).astype(o_ref.dtype)


def _glu(x, w_in_t, b_glu3, layer, tn=512):
    t, k = x.shape
    tm = min(INPROJ_TM, t)
    half = D_CONV // tn
    return pl.pallas_call(
        _glu_kernel,
        grid=(half, t // tm),
        in_specs=[pl.BlockSpec((tm, k), lambda n, m: (m, 0)),
                  pl.BlockSpec((None, tn, k), lambda n, m: (layer, n, 0)),
                  pl.BlockSpec((None, tn, k), lambda n, m: (layer, n + half, 0)),
                  pl.BlockSpec((None, 1, tn), lambda n, m: (layer, 0, n)),
                  pl.BlockSpec((None, 1, tn), lambda n, m: (layer, 0, n + half))],
        out_specs=pl.BlockSpec((tm, tn), lambda n, m: (m, n)),
        out_shape=jax.ShapeDtypeStruct((t, D_CONV), BF16),
        scratch_shapes=[pltpu.VMEM((k, tn), BF16), pltpu.VMEM((k, tn), BF16)],
        compiler_params=_params(2, INPROJ_VMEM_LIMIT),
        name="inproj_glu",
    )(x, w_in_t, w_in_t, b_glu3, b_glu3)


def _convbranch_kernel(c_ref, cw_ref, cb_ref, lg_ref, lb_ref, w_ref, gate_ref, o_ref,
                       ext_ref, sh_ref, conv_ref, wbf_ref, *, tl):
    first = (pl.program_id(0) == 0) & (pl.program_id(1) == 0)

    @pl.when(first)
    def _():
        wbf_ref[...] = w_ref[...].astype(BF16)

    @pl.when(pl.program_id(1) == 0)
    def _():
        ext_ref[0:CONV_HALO, :] = jnp.zeros((CONV_HALO, D_CONV), F32)

    @pl.when(pl.program_id(1) > 0)
    def _():
        ext_ref[0:CONV_HALO, :] = ext_ref[tl:tl + CONV_HALO, :]

    ext_ref[CONV_HALO:CONV_HALO + tl, :] = c_ref[...].astype(F32)

    sh_rows = tl + CONV_HALO - 8
    for s in range(1, 8):
        sh_ref[s - 1] = ext_ref[s:s + sh_rows, :]

    base = CONV_HALO - (CONV_WIDTH - 1)

    n_groups = CONV_RC // 8
    taps_of_shift = [[(a, 8 * a + s - base) for a in range(5) if 0 <= 8 * a + s - base < CONV_WIDTH]
                     for s in range(8)]

    def conv_rows(rc, carry):
        r0 = pl.multiple_of(rc * CONV_RC, CONV_RC)
        for lo in range(0, D_CONV, LANES):
            cols = slice(lo, lo + LANES)
            w = [jnp.broadcast_to(cw_ref[k:k + 1, cols], (8, LANES)) for k in range(CONV_WIDTH)]
            acc = [jnp.broadcast_to(cb_ref[:, cols], (8, LANES))] * n_groups
            for s in range(8):
                taps = taps_of_shift[s]
                for j in range(n_groups + max(a for a, _ in taps)):
                    used = [(a, k) for a, k in taps if 0 <= j - a < n_groups]
                    if not used:
                        continue
                    rows = pl.ds(r0 + 8 * j, 8)
                    x = ext_ref[rows, cols] if s == 0 else sh_ref[s - 1, rows, cols]
                    for a, k in used:
                        acc[j - a] = acc[j - a] + w[k] * x
            for i in range(n_groups):
                conv_ref[pl.ds(r0 + 8 * i, 8), cols] = acc[i]
        return carry

    lax.fori_loop(0, tl // CONV_RC, conv_rows, 0)
    h = _layer_norm(conv_ref[...], lg_ref[...], lb_ref[...])
    h = h * _sigmoid(h)
    y = jnp.dot(h.astype(BF16), wbf_ref[...], preferred_element_type=F32)
    o_ref[...] = (y * gate_ref[...].astype(F32)).astype(o_ref.dtype)


def _convbranch(c, gates, conv_w, conv_b3, ln_g3, ln_b3, w_conv_out, layer, bsz, seqlen, tl=256):
    t = c.shape[0]
    nl = seqlen // tl
    vec = pl.BlockSpec((None, 1, D_CONV), lambda b, i: (layer, 0, 0))
    return pl.pallas_call(
        functools.partial(_convbranch_kernel, tl=tl),
        grid=(bsz, nl),
        in_specs=[pl.BlockSpec((tl, D_CONV), lambda b, i: (b * nl + i, 0)),
                  pl.BlockSpec((None, CONV_WIDTH, D_CONV), lambda b, i: (layer, 0, 0)),
                  vec, vec, vec,
                  pl.BlockSpec((None, D_CONV, D_MODEL), lambda b, i: (layer, 0, 0)),
                  pl.BlockSpec((tl, D_MODEL), lambda b, i: (b * nl + i, 0))],
        out_specs=pl.BlockSpec((tl, D_MODEL), lambda b, i: (b * nl + i, 0)),
        out_shape=jax.ShapeDtypeStruct((t, D_MODEL), BF16),
        scratch_shapes=[pltpu.VMEM((tl + CONV_HALO, D_CONV), F32),
                        pltpu.VMEM((7, tl + CONV_HALO - 8, D_CONV), F32),
                        pltpu.VMEM((tl, D_CONV), F32),
                        pltpu.VMEM((D_CONV, D_MODEL), BF16)],
        compiler_params=_params(2),
        name="conv_module",
    )(c, conv_w, conv_b3, ln_g3, ln_b3, w_conv_out, gates)


def _expand_heads(v, g, lane_in_pair):
    rows = v.shape[0]
    b = [jnp.broadcast_to(v[:, g * HEADS_PER_GROUP + r:g * HEADS_PER_GROUP + r + 1], (rows, LANES))
         for r in range(HEADS_PER_GROUP)]
    first = lane_in_pair[:rows] < SSM_HEAD_DIM
    return jnp.concatenate([jnp.where(first, b[0], b[1]), jnp.where(first, b[2], b[3])], axis=1)


def _ssd_kernel(xbc_ref, u_ref, zs_ref, wdt_ref, dtb_ref, cw_ref, cb_ref, alog_ref, dskip_ref, ng_ref,
                o_ref, ext_ref, sh_ref, act_ref, state_ref, *, q):
    @pl.when(pl.program_id(1) == 0)
    def _():
        ext_ref[0:SSM_HALO, :] = jnp.zeros((SSM_HALO, D_XBC), F32)
        state_ref[...] = jnp.zeros(state_ref.shape, F32)

    @pl.when(pl.program_id(1) > 0)
    def _():
        ext_ref[0:SSM_HALO, :] = ext_ref[q:q + SSM_HALO, :]

    ext_ref[SSM_HALO:SSM_HALO + q, :] = xbc_ref[...].astype(F32)

    base = SSM_HALO - (SSM_CONV_WIDTH - 1)
    for k in range(SSM_CONV_WIDTH - 1):
        sh_ref[k] = ext_ref[base + k:base + k + q, :]
    for lo in range(0, D_XBC, CONV_CW):
        for r0 in range(0, q, CONV_RC):
            cols = slice(lo, lo + CONV_CW)
            acc = cb_ref[:, cols] + cw_ref[SSM_CONV_WIDTH - 1:SSM_CONV_WIDTH, cols] * \
                ext_ref[SSM_HALO + r0:SSM_HALO + r0 + CONV_RC, cols]
            for k in range(SSM_CONV_WIDTH - 1):
                acc = acc + cw_ref[k:k + 1, cols] * sh_ref[k, r0:r0 + CONV_RC, cols]
            act_ref[r0:r0 + CONV_RC, cols] = acc * _sigmoid(acc)

    dt_raw = jnp.dot(u_ref[...], wdt_ref[...], preferred_element_type=F32) + dtb_ref[...]
    dt = jnp.maximum(dt_raw, 0.0) + jnp.log1p(jnp.exp(-jnp.abs(dt_raw)))
    adt = dt * (-jnp.exp(alog_ref[...]))
    row = lax.broadcasted_iota(I32, (q, q), 0)
    col = lax.broadcasted_iota(I32, (q, q), 1)
    causal = row >= col
    tril = jnp.where(causal, 1.0, 0.0).astype(F32)
    acs = jnp.dot(tril, adt, preferred_element_type=F32, precision=lax.Precision.HIGHEST)
    acs_t = acs.T
    dt_t = dt.T
    last = acs[q - 1:q, :]
    exp_acs = jnp.exp(acs)
    dt_decay = dt * jnp.exp(last - acs)
    chunk_decay = jnp.exp(last)
    lane_in_pair = lax.broadcasted_iota(I32, (q, LANES), 1)
    head_of_lane = lax.broadcasted_iota(I32, (1, GROUP_CH), 1) // SSM_HEAD_DIM

    b_off = D_INNER
    c_off = D_INNER + SSM_GROUPS * D_STATE
    for g in range(SSM_GROUPS):
        ch = slice(g * GROUP_CH, (g + 1) * GROUP_CH)
        xg = act_ref[:, ch]
        xg_bf = xg.astype(BF16)
        bg = act_ref[:, b_off + g * D_STATE:b_off + (g + 1) * D_STATE]
        cg = act_ref[:, c_off + g * D_STATE:c_off + (g + 1) * D_STATE].astype(BF16)
        cb = lax.dot_general(cg, bg.astype(BF16), (((1,), (1,)), ((), ())),
                             preferred_element_type=F32)
        ms, xblocks = [], []
        for r in range(HEADS_PER_GROUP):
            h = g * HEADS_PER_GROUP + r
            lmat = jnp.exp(jnp.where(causal, acs[:, h:h + 1] - acs_t[h:h + 1, :], -jnp.inf))
            ms.append((cb * lmat * dt_t[h:h + 1, :]).astype(BF16))
            head_mask = jnp.where(head_of_lane == r, 1.0, 0.0).astype(BF16)
            xblocks.append(xg_bf * head_mask)
        y_diag = jnp.dot(jnp.concatenate(ms, axis=1), jnp.concatenate(xblocks, axis=0),
                         preferred_element_type=F32)
        s_prev = state_ref[g]
        y_off = jnp.dot(cg, s_prev.astype(BF16), preferred_element_type=F32)
        yg = y_diag + y_off * _expand_heads(exp_acs, g, lane_in_pair) + xg * dskip_ref[:, ch]
        xw = (xg * _expand_heads(dt_decay, g, lane_in_pair)).astype(BF16)
        dec = _expand_heads(chunk_decay, g, lane_in_pair)
        state_ref[g] = s_prev * dec + jnp.dot(bg.T.astype(BF16), xw, preferred_element_type=F32)

        yz = yg * zs_ref[:, g * GROUP_CH:(g + 1) * GROUP_CH].astype(F32)
        ms = jnp.mean(yz * yz, axis=-1, keepdims=True)
        yn = yz * lax.rsqrt(ms + RMS_EPS) * ng_ref[:, g * GROUP_CH:(g + 1) * GROUP_CH]
        o_ref[:, g * GROUP_CH:(g + 1) * GROUP_CH] = yn.astype(o_ref.dtype)


def _ssd(xbc, u, zs, w_dt, dtb3, ssm_conv_w, ssm_conv_b3, alog3, dskip3, ng3, layer, bsz, seqlen):
    t = xbc.shape[0]
    q = SSD_Q
    nq = seqlen // q
    tile = lambda width: pl.BlockSpec((q, width), lambda b, i: (b * nq + i, 0))
    vec = lambda width: pl.BlockSpec((None, 1, width), lambda b, i: (layer, 0, 0))
    return pl.pallas_call(
        functools.partial(_ssd_kernel, q=q),
        grid=(bsz, nq),
        in_specs=[tile(D_XBC), tile(D_MODEL), tile(D_INNER),
                  pl.BlockSpec((None, D_MODEL, LANES), lambda b, i: (layer, 0, 0)), vec(LANES),
                  pl.BlockSpec((None, SSM_CONV_WIDTH, D_XBC), lambda b, i: (layer, 0, 0)),
                  vec(D_XBC), vec(LANES), vec(D_INNER), vec(D_INNER)],
        out_specs=tile(D_INNER),
        out_shape=jax.ShapeDtypeStruct((t, D_INNER), BF16),
        scratch_shapes=[pltpu.VMEM((q + SSM_HALO, D_XBC), F32),
                        pltpu.VMEM((SSM_CONV_WIDTH - 1, q, D_XBC), F32),
                        pltpu.VMEM((q, D_XBC), F32),
                        pltpu.VMEM((SSM_GROUPS, D_STATE, GROUP_CH), F32)],
        compiler_params=_params(2),
        name="ssd_mixer",
    )(xbc, u, zs, w_dt, dtb3, ssm_conv_w, ssm_conv_b3, alog3, dskip3, ng3)


def _first_argmax(vals):
    best, idx = vals[0], jnp.zeros(vals[0].shape, I32)
    for j in range(1, len(vals)):
        gt = vals[j] > best
        idx = jnp.where(gt, j, idx)
        best = jnp.where(gt, vals[j], best)
    return idx, best


def _select(idx, vals):
    out = vals[len(vals) - 1]
    for j in range(len(vals) - 2, -1, -1):
        out = jnp.where(idx == j, vals[j], out)
    return out


def _outproj_kernel(yn_ref, y1_ref, gs_ref, x_ref, wso_ref, wo_ref, lg_ref, lb_ref, wr_ref, rb_ref,
                    x1_ref, x1b_ref, pos_ref, wts_ref, cnt_ref, seg_ref,
                    wso_bf, wo_bf, base_ref, *, tm):
    @pl.when(pl.program_id(0) == 0)
    def _():
        wso_bf[...] = wso_ref[...].astype(BF16)
        wo_bf[...] = wo_ref[...].astype(BF16)
        base_ref[...] = jnp.zeros(base_ref.shape, F32)

    y_ssm = jnp.dot(yn_ref[...], wso_bf[...], preferred_element_type=F32)
    merged = y1_ref[...].astype(F32) + gs_ref[...].astype(F32) * y_ssm
    mix = jnp.dot(merged.astype(BF16), wo_bf[...], preferred_element_type=F32)
    x1 = _layer_norm(ALPHA * x_ref[...] + mix, lg_ref[...], lb_ref[...])
    x1_ref[...] = x1
    x1b_ref[...] = x1.astype(BF16)

    logits = lax.dot_general(wr_ref[...], x1, (((1,), (1,)), ((), ())),
                             preferred_element_type=F32, precision=lax.Precision.HIGHEST)
    scores = _sigmoid(logits)
    sel = scores + rb_ref[...]
    sel_rows = [sel[e:e + 1, :] for e in range(N_EXPERTS)]
    sc_rows = [scores[e:e + 1, :] for e in range(N_EXPERTS)]

    gscores = []
    for gi in range(N_EXPERT_GROUPS):
        v = sel_rows[gi * EXPERTS_PER_GROUP:(gi + 1) * EXPERTS_PER_GROUP]
        best = None
        for a in range(EXPERTS_PER_GROUP):
            for b in range(a + 1, EXPERTS_PER_GROUP):
                s = v[a] + v[b]
                best = s if best is None else jnp.maximum(best, s)
        gscores.append(best)
    grp, _ = _first_argmax(gscores)

    sel_in = [_select(grp, [sel_rows[gi * EXPERTS_PER_GROUP + j] for gi in range(N_EXPERT_GROUPS)])
              for j in range(EXPERTS_PER_GROUP)]
    sc_in = [_select(grp, [sc_rows[gi * EXPERTS_PER_GROUP + j] for gi in range(N_EXPERT_GROUPS)])
             for j in range(EXPERTS_PER_GROUP)]
    i1, _ = _first_argmax(sel_in)
    neg = jnp.full(sel_in[0].shape, -jnp.inf, F32)
    i2, _ = _first_argmax([jnp.where(i1 == j, neg, sel_in[j]) for j in range(EXPERTS_PER_GROUP)])
    s1 = _select(i1, sc_in)
    s2 = _select(i2, sc_in)
    tot = s1 + s2
    e1 = grp * EXPERTS_PER_GROUP + i1
    e2 = grp * EXPERTS_PER_GROUP + i2
    wts_ref[0:1, :] = s1 / tot
    wts_ref[1:2, :] = s2 / tot

    eio = lax.broadcasted_iota(I32, (N_EXPERTS, tm), 0)
    oh1 = jnp.where(eio == e1, 1.0, 0.0).astype(F32)
    oh2 = jnp.where(eio == e2, 1.0, 0.0).astype(F32)
    both = oh1 + oh2
    srow = lax.broadcasted_iota(I32, (tm, tm), 0)
    scol = lax.broadcasted_iota(I32, (tm, tm), 1)
    before = jnp.where(srow < scol, 1.0, 0.0).astype(BF16)
    cum = jnp.dot(both.astype(BF16), before, preferred_element_type=F32)
    cnt8 = jnp.floor((jnp.sum(both, axis=1, keepdims=True) + 7.0) * 0.125) * 8.0
    cnt8_l = jnp.broadcast_to(cnt8, (N_EXPERTS, LANES))
    erow = lax.broadcasted_iota(I32, (N_EXPERTS, N_EXPERTS), 0)
    ecol = lax.broadcasted_iota(I32, (N_EXPERTS, N_EXPERTS), 1)
    seg_off = jnp.dot(jnp.where(ecol < erow, 1.0, 0.0).astype(F32), cnt8_l,
                      preferred_element_type=F32, precision=lax.Precision.HIGHEST)
    base = base_ref[...]
    in_tile = seg_off[:, 0:1] + cum
    pos_ref[0:1, :] = jnp.sum(oh1 * in_tile, axis=0, keepdims=True).astype(I32)
    pos_ref[1:2, :] = jnp.sum(oh2 * in_tile, axis=0, keepdims=True).astype(I32)
    lane = lax.broadcasted_iota(I32, (N_EXPERTS, LANES), 1)
    seg_ref[...] = jnp.where(lane == 0, seg_off, jnp.where(lane == 1, cnt8_l, base)).astype(I32)
    base_ref[...] = base + cnt8_l
    cnt_ref[...] = base_ref[...].astype(I32)


def _outproj(yn, y1g, gates, x, w_ssm_out, w_out, ln_g3, ln_b3, wr_t, rbias, layer, tm):
    t = x.shape[0]
    tile = lambda width: pl.BlockSpec((tm, width), lambda m: (m, 0))
    vec = pl.BlockSpec((None, 1, D_MODEL), lambda m: (layer, 0, 0))
    pair = pl.BlockSpec((2, tm), lambda m: (0, m))
    return pl.pallas_call(
        functools.partial(_outproj_kernel, tm=tm),
        grid=(t // tm,),
        in_specs=[tile(D_INNER), tile(D_MODEL),
                  pl.BlockSpec((tm, D_MODEL), lambda m: (m, 1)),
                  tile(D_MODEL),
                  pl.BlockSpec((None, D_INNER, D_MODEL), lambda m: (layer, 0, 0)),
                  pl.BlockSpec((None, D_MODEL, D_MODEL), lambda m: (layer, 0, 0)),
                  vec, vec,
                  pl.BlockSpec((N_EXPERTS, D_MODEL), lambda m: (0, 0)),
                  pl.BlockSpec((N_EXPERTS, 1), lambda m: (0, 0))],
        out_specs=[tile(D_MODEL), tile(D_MODEL), pair, pair,
                   pl.BlockSpec((N_EXPERTS, LANES), lambda m: (0, 0)),
                   pl.BlockSpec((None, N_EXPERTS, LANES), lambda m: (m, 0, 0))],
        out_shape=[jax.ShapeDtypeStruct((t, D_MODEL), F32),
                   jax.ShapeDtypeStruct((t, D_MODEL), BF16),
                   jax.ShapeDtypeStruct((2, t), I32),
                   jax.ShapeDtypeStruct((2, t), F32),
                   jax.ShapeDtypeStruct((N_EXPERTS, LANES), I32),
                   jax.ShapeDtypeStruct((t // tm, N_EXPERTS, LANES), I32)],
        scratch_shapes=[pltpu.VMEM((D_INNER, D_MODEL), BF16),
                        pltpu.VMEM((D_MODEL, D_MODEL), BF16),
                        pltpu.VMEM((N_EXPERTS, LANES), F32)],
        compiler_params=_params(1),
        name="outproj_ln_router",
    )(yn, y1g, gates, x, w_ssm_out, w_out, ln_g3, ln_b3, wr_t, rbias)


def _segment_blocks(tile, segoff_ref, cnt8_ref, gdst_ref, make_copy, act):
    for e in range(N_EXPERTS):
        idx = tile * N_EXPERTS + e
        n8 = lax.shift_right_logical(cnt8_ref[idx], 3)
        local = segoff_ref[idx]
        glob = gdst_ref[idx]
        for bit in SEGMENT_BITS:
            rows = 8 * bit
            hit = (n8 & bit) != 0

            @pl.when(hit)
            def _(local=local, glob=glob, rows=rows):
                act(make_copy(pl.multiple_of(local, 8), pl.multiple_of(glob, 8), rows))

            step = jnp.where(hit, rows, 0)
            local = local + step
            glob = glob + step


def _dispatch_kernel(fill_ref, segoff_ref, cnt8_ref, gdst_ref, pos_ref, x_ref, xs_hbm,
                     sorted_ref, zeros_ref, sem, blk_sem, seg_sems, *, tm, n_blk, n_tiles):
    m = pl.program_id(0)
    slot = lax.rem(m, 2)

    def zero_rows(row):
        return pltpu.make_async_copy(zeros_ref.at[pl.ds(0, 8)],
                                     xs_hbm.at[pl.ds(pl.multiple_of(row, 8), 8)], sem)

    def zero_block(b):
        return pltpu.make_async_copy(
            zeros_ref, xs_hbm.at[pl.ds(pl.multiple_of(b * MOE_ROWS, MOE_ROWS), MOE_ROWS)], blk_sem)

    @pl.when(m == 0)
    def _():
        zeros_ref[...] = jnp.zeros(zeros_ref.shape, F32)
        n_active = fill_ref[2 * N_EXPERTS]
        for e in range(N_EXPERTS):
            pad_start = fill_ref[e]
            lax.fori_loop(0, fill_ref[N_EXPERTS + e],
                          lambda j, c: (zero_rows(pad_start + 8 * j).start(), c)[1], 0)
        lax.fori_loop(n_active, n_blk, lambda b, c: (zero_block(b).start(), c)[1], 0)
        lax.fori_loop(0, fill_ref[2 * N_EXPERTS + 1], lambda j, c: (zero_rows(0).wait(), c)[1], 0)
        lax.fori_loop(n_active, n_blk, lambda b, c: (zero_block(b).wait(), c)[1], 0)

    def for_each_segment_block(tile, s, act):
        copy = lambda local, glob, rows: pltpu.make_async_copy(
            sorted_ref.at[s, pl.ds(local, rows)], xs_hbm.at[pl.ds(glob, rows)], seg_sems.at[s])
        _segment_blocks(tile, segoff_ref, cnt8_ref, gdst_ref, copy, act)

    start = lambda copy: copy.start()
    wait = lambda copy: copy.wait()

    @pl.when(m >= 2)
    def _():
        for_each_segment_block(m - 2, slot, wait)

    pos = pos_ref[...]
    srow = lax.broadcasted_iota(I32, (sorted_ref.shape[1], tm), 0)
    onehot = jnp.where(srow == pos[0:1, :], 1.0, jnp.where(srow == pos[1:2, :], 1.0, 0.0)).astype(BF16)
    sorted_ref[slot] = jnp.dot(onehot, x_ref[...], preferred_element_type=F32)
    for_each_segment_block(m, slot, start)

    @pl.when(m == n_tiles - 1)
    def _():
        if n_tiles > 1:
            for_each_segment_block(m - 1, 1 - slot, wait)
        for_each_segment_block(m, slot, wait)


def _dispatch(fill, seg_off, seg_cnt8, seg_dst, pos, x1b, n_blk, tm):
    t = x1b.shape[0]
    n_tiles = t // tm
    sorted_rows = 2 * tm + 8 * N_EXPERTS
    grid_spec = pltpu.PrefetchScalarGridSpec(
        num_scalar_prefetch=4,
        grid=(n_tiles,),
        in_specs=[pl.BlockSpec((2, tm), lambda m, *_: (0, m)),
                  pl.BlockSpec((tm, D_MODEL), lambda m, *_: (m, 0))],
        out_specs=pl.BlockSpec(memory_space=pl.ANY),
        scratch_shapes=[pltpu.VMEM((2, sorted_rows, D_MODEL), F32),
                        pltpu.VMEM((MOE_ROWS, D_MODEL), F32),
                        pltpu.SemaphoreType.DMA(()), pltpu.SemaphoreType.DMA(()),
                        pltpu.SemaphoreType.DMA((2,))],
    )
    return pl.pallas_call(
        functools.partial(_dispatch_kernel, tm=tm, n_blk=n_blk, n_tiles=n_tiles),
        grid_spec=grid_spec,
        out_shape=jax.ShapeDtypeStruct((n_blk * MOE_ROWS, D_MODEL), F32),
        compiler_params=_params(1),
        name="moe_dispatch",
    )(fill, seg_off, seg_cnt8, seg_dst, pos, x1b)


def _expert_kernel(src_ref, exp_ref, nvalid_ref, xs_ref, wg_ref, wu_ref, wd_ref, y_ref,
                   wgu_bf, wd_bf):
    i = pl.program_id(0)
    nvalid = nvalid_ref[i]
    changed = (i == 0) | (exp_ref[i] != exp_ref[jnp.maximum(i - 1, 0)])

    @pl.when((nvalid > 0) & changed)
    def _():
        wgu_bf[:, :D_EXPERT] = wg_ref[...].astype(BF16)
        wgu_bf[:, D_EXPERT:] = wu_ref[...].astype(BF16)
        wd_bf[...] = wd_ref[...].astype(BF16)

    @pl.when(nvalid > 0)
    def _():
        rows = lax.broadcasted_iota(I32, (MOE_ROWS, 1), 0)
        x = jnp.where(rows < nvalid, xs_ref[...], 0.0).astype(BF16)
        gu = jnp.dot(x, wgu_bf[...], preferred_element_type=F32)
        hg = gu[:, :D_EXPERT]
        h = hg * _sigmoid(hg) * gu[:, D_EXPERT:]
        y_ref[...] = jnp.dot(h.astype(BF16), wd_bf[...], preferred_element_type=F32)

    @pl.when(nvalid == 0)
    def _():
        y_ref[...] = jnp.zeros(y_ref.shape, F32)


def _experts(blk_src, blk_exp, blk_nvalid, xs, wg, wu, wd, layer):
    n_rows = xs.shape[0]
    n_blk = n_rows // MOE_ROWS
    grid_spec = pltpu.PrefetchScalarGridSpec(
        num_scalar_prefetch=3,
        grid=(n_blk,),
        in_specs=[pl.BlockSpec((MOE_ROWS, D_MODEL), lambda i, src, exp, nv: (src[i], 0)),
                  pl.BlockSpec((None, None, D_MODEL, D_EXPERT),
                               lambda i, src, exp, nv: (layer, exp[i], 0, 0)),
                  pl.BlockSpec((None, None, D_MODEL, D_EXPERT),
                               lambda i, src, exp, nv: (layer, exp[i], 0, 0)),
                  pl.BlockSpec((None, None, D_EXPERT, D_MODEL),
                               lambda i, src, exp, nv: (layer, exp[i], 0, 0))],
        out_specs=pl.BlockSpec((MOE_ROWS, D_MODEL), lambda i, src, exp, nv: (i, 0)),
        scratch_shapes=[pltpu.VMEM((D_MODEL, 2 * D_EXPERT), BF16),
                        pltpu.VMEM((D_EXPERT, D_MODEL), BF16)],
    )
    return pl.pallas_call(
        _expert_kernel,
        grid_spec=grid_spec,
        out_shape=jax.ShapeDtypeStruct((n_rows, D_MODEL), F32),
        compiler_params=_params(1),
        name="moe_experts",
    )(blk_src, blk_exp, blk_nvalid, xs, wg, wu, wd)


def _moe_plan(counts, seg, t):
    cnt = counts[:, 0]
    nblk_e = (cnt + MOE_ROWS - 1) // MOE_ROWS
    blk_end = jnp.cumsum(nblk_e)
    blk_start = blk_end - nblk_e
    n_active = blk_end[N_EXPERTS - 1]
    experts = jnp.arange(N_EXPERTS, dtype=I32)
    row_start = blk_start * MOE_ROWS
    n_tiles = seg.shape[0]
    n_blk = -(-(2 * t + 8 * N_EXPERTS * n_tiles) // MOE_ROWS) + N_EXPERTS
    ids = jnp.arange(n_blk, dtype=I32)
    src = jnp.minimum(ids, n_active - 1)
    exp = jnp.minimum(jnp.sum((blk_end[None, :] <= src[:, None]).astype(I32), axis=1), N_EXPERTS - 1)
    onehot = (exp[:, None] == experts[None, :]).astype(I32)
    pick = lambda table: jnp.sum(onehot * table[None, :], axis=1)
    left = pick(cnt) - (src - pick(blk_start)) * MOE_ROWS
    nvalid = jnp.where(ids < n_active, jnp.clip(left, 0, MOE_ROWS), 0)
    pad8 = (nblk_e * MOE_ROWS - cnt) // 8
    fill = jnp.concatenate([row_start + cnt, pad8, jnp.stack([n_active, jnp.sum(pad8)])]).astype(I32)
    seg_off = seg[:, :, 0].reshape(-1)
    seg_cnt8 = seg[:, :, 1].reshape(-1)
    seg_dst = (row_start[None, :] + seg[:, :, 2]).reshape(-1)
    return src.astype(I32), exp, nvalid.astype(I32), fill, seg_off, seg_cnt8, seg_dst, n_blk


def _final_kernel(segoff_ref, cnt8_ref, gsrc_ref, x1_ref, x1b_ref, p_ref, wt_ref, pos_ref, wpu_ref,
                  wpg_ref, bpg_ref, lg_ref, lb_ref, y_hbm, x2_ref, x2b_ref, ys_ref, wpu_bf, wpg_bf, sems,
                  *, tm, n_tiles):
    m = pl.program_id(0)
    slot = lax.rem(m, 2)

    def for_each_segment_block(tile, s, act):
        copy = lambda local, glob, rows: pltpu.make_async_copy(
            y_hbm.at[pl.ds(glob, rows)], ys_ref.at[s, pl.ds(local, rows)], sems.at[s])
        _segment_blocks(tile, segoff_ref, cnt8_ref, gsrc_ref, copy, act)

    @pl.when(m == 0)
    def _():
        wpu_bf[...] = wpu_ref[...].astype(BF16)
        wpg_bf[...] = wpg_ref[...].astype(BF16)
        for_each_segment_block(0, 0, lambda copy: copy.start())

    @pl.when(m + 1 < n_tiles)
    def _():
        for_each_segment_block(m + 1, 1 - slot, lambda copy: copy.start())

    up = jnp.dot(p_ref[...].astype(BF16), wpu_bf[...], preferred_element_type=F32)
    gate = _sigmoid(jnp.dot(x1b_ref[...], wpg_bf[...], preferred_element_type=F32) + bpg_ref[...])
    resid = ALPHA * x1_ref[...] + up * gate

    for_each_segment_block(m, slot, lambda copy: copy.wait())

    n_sorted = ys_ref.shape[1]
    used = segoff_ref[m * N_EXPERTS + N_EXPERTS - 1] + cnt8_ref[m * N_EXPERTS + N_EXPERTS - 1]
    srow = lax.broadcasted_iota(I32, (n_sorted, 1), 0)
    ys = jnp.where(srow < used, ys_ref[slot], 0.0).astype(BF16)
    pos = pos_ref[...]
    scol = lax.broadcasted_iota(I32, (tm, n_sorted), 1)
    wt = wt_ref[...]
    moe = jnp.zeros((tm, D_MODEL), F32)
    for k in range(TOP_K):
        pick = jnp.where(scol == pos[:, k:k + 1], 1.0, 0.0).astype(BF16)
        moe = moe + wt[:, k:k + 1] * jnp.dot(pick, ys, preferred_element_type=F32)
    x2 = _layer_norm(resid + moe, lg_ref[...], lb_ref[...])
    x2_ref[...] = x2
    x2b_ref[...] = x2.astype(BF16)


def _final(seg_off, seg_cnt8, seg_src, x1, x1b, p, wt_tok, pos_tok, w_ple_up, w_ple_gate, b_pg3,
           ln_g3, ln_b3, y_rows, layer, tm):
    t = x1.shape[0]
    n_tiles = t // tm
    sorted_rows = 2 * tm + 8 * N_EXPERTS
    tile = lambda width: pl.BlockSpec((tm, width), lambda m, *_: (m, 0))
    vec = pl.BlockSpec((None, 1, D_MODEL), lambda m, *_: (layer, 0, 0))
    grid_spec = pltpu.PrefetchScalarGridSpec(
        num_scalar_prefetch=3,
        grid=(n_tiles,),
        in_specs=[tile(D_MODEL), tile(D_MODEL),
                  pl.BlockSpec((None, tm, PLE_DIM), lambda m, *_: (layer, m, 0)),
                  tile(2), tile(2),
                  pl.BlockSpec((None, PLE_DIM, D_MODEL), lambda m, *_: (layer, 0, 0)),
                  pl.BlockSpec((None, D_MODEL, D_MODEL), lambda m, *_: (layer, 0, 0)),
                  vec, vec, vec,
                  pl.BlockSpec(memory_space=pl.ANY)],
        out_specs=[tile(D_MODEL), tile(D_MODEL)],
        scratch_shapes=[pltpu.VMEM((2, sorted_rows, D_MODEL), F32),
                        pltpu.VMEM((PLE_DIM, D_MODEL), BF16), pltpu.VMEM((D_MODEL, D_MODEL), BF16),
                        pltpu.SemaphoreType.DMA((2,))],
    )
    return pl.pallas_call(
        functools.partial(_final_kernel, tm=tm, n_tiles=n_tiles),
        grid_spec=grid_spec,
        out_shape=[jax.ShapeDtypeStruct((t, D_MODEL), F32),
                   jax.ShapeDtypeStruct((t, D_MODEL), BF16)],
        compiler_params=_params(1),
        name="combine_ple_ln",
    )(seg_off, seg_cnt8, seg_src, x1, x1b, p, wt_tok, pos_tok, w_ple_up, w_ple_gate, b_pg3,
      ln_g3, ln_b3, y_rows)


def kernel(x, p, w_in, b_glu, b_branch_gate, conv_w, conv_b, conv_ln_g, conv_ln_b, w_conv_out,
           ssm_conv_w, ssm_conv_b, dt_bias, a_log, d_skip, ssm_norm_g, w_ssm_out, w_out,
           ln1_g, ln1_b, w_router, router_bias, w_exp_gate, w_exp_up, w_exp_down,
           w_ple_up, w_ple_gate, b_ple_gate, ln2_g, ln2_b):
    bsz, seqlen, d = x.shape
    depth = w_in.shape[0]
    t = bsz * seqlen
    tm_moe = min(MOE_TILE, t)

    row3 = lambda a: a.reshape(a.shape[0], 1, a.shape[1])
    pad_lanes = lambda a: jnp.pad(a, ((0, 0), (0, LANES - a.shape[1])))
    b_glu3, b_gate3 = row3(b_glu), row3(b_branch_gate)
    conv_b3, cln_g3, cln_b3 = row3(conv_b), row3(conv_ln_g), row3(conv_ln_b)
    ssm_conv_b3, ng3 = row3(ssm_conv_b), row3(ssm_norm_g)
    dtb3, alog3 = row3(pad_lanes(dt_bias)), row3(pad_lanes(a_log))
    dskip3 = row3(jnp.repeat(d_skip, SSM_HEAD_DIM, axis=1))
    ln1_g3, ln1_b3, ln2_g3, ln2_b3 = row3(ln1_g), row3(ln1_b), row3(ln2_g), row3(ln2_b)
    b_pg3 = row3(b_ple_gate)
    w_in_t = jnp.swapaxes(w_in, 1, 2)
    w_gate, w_dt = _tail_weights(w_in_t)
    wr_t = w_router.T
    rbias = router_bias.reshape(N_EXPERTS, 1)
    p2 = p.reshape(depth, t, PLE_DIM)

    xf = x.reshape(t, d)
    xb = xf.astype(BF16)
    for i in range(depth):
        c = _glu(xb, w_in_t, b_glu3, i)
        zs = _matmul(xb, w_in_t, lambda n: (i, OFF_Z // 1024 + n, 0), None, None,
                     D_INNER, 1024, "silu", BF16, w_rows_are_outputs=True)
        xbc = _matmul(xb, w_in_t, lambda n: (i, OFF_XBC // 1024 + n, 0), None, None,
                      D_XBC, 1024, "none", BF16, w_rows_are_outputs=True)
        gates = _matmul(xb, w_gate, lambda n: (i, 0, n), b_gate3, lambda n: (i, 0, n),
                        2 * D_MODEL, 1024, "sigmoid", BF16)
        y1g = _convbranch(c, gates, conv_w, conv_b3, cln_g3, cln_b3, w_conv_out, i, bsz, seqlen)
        yn = _ssd(xbc, xb, zs, w_dt, dtb3, ssm_conv_w, ssm_conv_b3, alog3, dskip3, ng3, i, bsz, seqlen)
        x1, x1b, pos, wts, counts, seg = _outproj(
            yn, y1g, gates, xf, w_ssm_out, w_out, ln1_g3, ln1_b3, wr_t, rbias, i, tm_moe)
        (blk_src, blk_exp, blk_nvalid, fill, seg_off, seg_cnt8, seg_row,
         n_blk) = _moe_plan(counts, seg, t)
        xs = _dispatch(fill, seg_off, seg_cnt8, seg_row, pos, x1b, n_blk, tm_moe)
        y_rows = _experts(blk_src, blk_exp, blk_nvalid, xs, w_exp_gate, w_exp_up, w_exp_down, i)
        xf, xb = _final(seg_off, seg_cnt8, seg_row, x1, x1b, p2, wts.T, pos.T,
                        w_ple_up, w_ple_gate, b_pg3, ln2_g3, ln2_b3, y_rows, i, tm_moe)
    return xf.reshape(bsz, seqlen, d)
```

```python
import functools

import jax
import jax.numpy as jnp
from jax import lax
from jax.experimental import pallas as pl
from jax.experimental.pallas import tpu as pltpu

F32 = jnp.float32
BF16 = jnp.bfloat16
I32 = jnp.int32

D_MODEL = 1024
D_CONV = 1024
CONV_WIDTH = 31
D_INNER = 2048
SSM_HEAD_DIM = 64
SSM_HEADS = 32
SSM_GROUPS = 8
HEADS_PER_GROUP = 4
D_STATE = 128
SSM_CONV_WIDTH = 4
D_XBC = D_INNER + 2 * SSM_GROUPS * D_STATE
GROUP_CH = HEADS_PER_GROUP * SSM_HEAD_DIM
N_EXPERTS = 16
N_EXPERT_GROUPS = 4
TOP_K = 2
EXPERTS_PER_GROUP = 4
D_EXPERT = 512
PLE_DIM = 256
DEPTH = 4
ALPHA = (2.0 * DEPTH) ** 0.25
LN_EPS = 1e-5
RMS_EPS = 1e-5

LANES = 128
CONV_HALO = 32
CONV_RC = 64
CONV_CW = 256
SSM_HALO = 8
SSD_Q = 128
SSD_STEP = 256
MOE_ROWS = 512
MOE_TILE = 512
SEGMENT_BITS = (64, 32, 16, 8, 4, 2, 1)
VMEM_LIMIT = 48 * 1024 * 1024
INPROJ_TM = 2048
INPROJ_VMEM_LIMIT = 56 * 1024 * 1024

OFF_GLU = 0
OFF_Z = 2 * D_CONV
OFF_XBC = OFF_Z + D_INNER
OFF_DT = OFF_XBC + D_XBC
OFF_GATE = OFF_DT + SSM_HEADS


def _sigmoid(x):
    return 1.0 / (1.0 + jnp.exp(-x))


def _layer_norm(x, g, b):
    mu = jnp.mean(x, axis=-1, keepdims=True)
    xc = x - mu
    var = jnp.mean(xc * xc, axis=-1, keepdims=True)
    return xc * lax.rsqrt(var + LN_EPS) * g + b


def _params(n_axes, vmem_limit=VMEM_LIMIT):
    return pltpu.CompilerParams(dimension_semantics=("arbitrary",) * n_axes,
                                vmem_limit_bytes=vmem_limit)


def _mm_kernel(x_ref, w_ref, *rest, act, has_bias, w_rows_are_outputs):
    if has_bias:
        b_ref, o_ref, wbf_ref = rest
    else:
        o_ref, wbf_ref = rest

    @pl.when(pl.program_id(1) == 0)
    def _():
        w = w_ref[...]
        wbf_ref[...] = (w.T if w_rows_are_outputs else w).astype(BF16)

    acc = jnp.dot(x_ref[...], wbf_ref[...], preferred_element_type=F32)
    if has_bias:
        acc = acc + b_ref[...]
    if act == "sigmoid":
        acc = _sigmoid(acc)
    elif act == "silu":
        acc = acc * _sigmoid(acc)
    o_ref[...] = acc.astype(o_ref.dtype)


def _matmul(x, w, w_index, bias, b_index, n_out, tn, act, out_dtype, w_rows_are_outputs=False):
    t, k = x.shape
    tm = min(INPROJ_TM, t)
    w_block = (None,) * (w.ndim - 2) + ((tn, k) if w_rows_are_outputs else (k, tn))
    in_specs = [pl.BlockSpec((tm, k), lambda n, m: (m, 0)),
                pl.BlockSpec(w_block, lambda n, m: w_index(n))]
    args = [x, w]
    if bias is not None:
        b_block = (None,) * (bias.ndim - 2) + (1, tn)
        in_specs.append(pl.BlockSpec(b_block, lambda n, m: b_index(n)))
        args.append(bias)
    return pl.pallas_call(
        functools.partial(_mm_kernel, act=act, has_bias=bias is not None,
                          w_rows_are_outputs=w_rows_are_outputs),
        grid=(n_out // tn, t // tm),
        in_specs=in_specs,
        out_specs=pl.BlockSpec((tm, tn), lambda n, m: (m, n)),
        out_shape=jax.ShapeDtypeStruct((t, n_out), out_dtype),
        scratch_shapes=[pltpu.VMEM((k, tn), BF16)],
        compiler_params=_params(2, INPROJ_VMEM_LIMIT),
        name="inproj_" + act,
    )(*args)


def _tail_weights_kernel(wt_hbm, wgate_ref, wdt_ref, buf_ref, sem):
    n_tail = buf_ref.shape[0]
    copy = pltpu.make_async_copy(wt_hbm.at[pl.program_id(0), pl.ds(OFF_DT, n_tail)], buf_ref, sem)
    copy.start()
    copy.wait()
    lane = lax.broadcasted_iota(I32, (D_MODEL, LANES), 1)
    wdt_ref[...] = jnp.where(lane < SSM_HEADS, buf_ref[0:LANES, :].T, 0.0).astype(BF16)
    step = 4 * LANES
    for lo in range(0, 2 * D_MODEL, step):
        rows = slice(SSM_HEADS + lo, SSM_HEADS + lo + step)
        wgate_ref[:, lo:lo + step] = buf_ref[rows, :].T.astype(BF16)


def _tail_weights(w_in_t):
    depth, n_all, k = w_in_t.shape
    n = 2 * D_MODEL
    return pl.pallas_call(
        _tail_weights_kernel,
        grid=(depth,),
        in_specs=[pl.BlockSpec(memory_space=pl.ANY)],
        out_specs=[pl.BlockSpec((None, k, n), lambda l: (l, 0, 0)),
                   pl.BlockSpec((None, k, LANES), lambda l: (l, 0, 0))],
        out_shape=[jax.ShapeDtypeStruct((depth, k, n), BF16),
                   jax.ShapeDtypeStruct((depth, k, LANES), BF16)],
        scratch_shapes=[pltpu.VMEM((n_all - OFF_DT, k), F32), pltpu.SemaphoreType.DMA(())],
        compiler_params=_params(1),
        name="tail_weights",
    )(w_in_t)


def _glu_kernel(x_ref, wa_ref, wg_ref, ba_ref, bg_ref, o_ref, wa_bf, wg_bf):
    @pl.when(pl.program_id(1) == 0)
    def _():
        wa_bf[...] = wa_ref[...].T.astype(BF16)
        wg_bf[...] = wg_ref[...].T.astype(BF16)

    x = x_ref[...]
    a = jnp.dot(x, wa_bf[...], preferred_element_type=F32) + ba_ref[...]
    g = jnp.dot(x, wg_bf[...], preferred_element_type=F32) + bg_ref[...]
    o_ref[...] = (a * _sigmoid(g)).astype(o_ref.dtype)


def _glu(x, w_in_t, b_glu3, layer, tn=512):
    t, k = x.shape
    tm = min(INPROJ_TM, t)
    half = D_CONV // tn
    return pl.pallas_call(
        _glu_kernel,
        grid=(half, t // tm),
        in_specs=[pl.BlockSpec((tm, k), lambda n, m: (m, 0)),
                  pl.BlockSpec((None, tn, k), lambda n, m: (layer, n, 0)),
                  pl.BlockSpec((None, tn, k), lambda n, m: (layer, n + half, 0)),
                  pl.BlockSpec((None, 1, tn), lambda n, m: (layer, 0, n)),
                  pl.BlockSpec((None, 1, tn), lambda n, m: (layer, 0, n + half))],
        out_specs=pl.BlockSpec((tm, tn), lambda n, m: (m, n)),
        out_shape=jax.ShapeDtypeStruct((t, D_CONV), BF16),
        scratch_shapes=[pltpu.VMEM((k, tn), BF16), pltpu.VMEM((k, tn), BF16)],
        compiler_params=_params(2, INPROJ_VMEM_LIMIT),
        name="inproj_glu",
    )(x, w_in_t, w_in_t, b_glu3, b_glu3)


def _convbranch_kernel(c_ref, cw_ref, cb_ref, lg_ref, lb_ref, w_ref, gate_ref, o_ref,
                       ext_ref, sh_ref, conv_ref, wbf_ref, *, tl):
    first = (pl.program_id(0) == 0) & (pl.program_id(1) == 0)

    @pl.when(first)
    def _():
        wbf_ref[...] = w_ref[...].astype(BF16)

    @pl.when(pl.program_id(1) == 0)
    def _():
        ext_ref[0:CONV_HALO, :] = jnp.zeros((CONV_HALO, D_CONV), F32)

    @pl.when(pl.program_id(1) > 0)
    def _():
        ext_ref[0:CONV_HALO, :] = ext_ref[tl:tl + CONV_HALO, :]

    ext_ref[CONV_HALO:CONV_HALO + tl, :] = c_ref[...].astype(F32)

    sh_rows = tl + CONV_HALO - 8
    for s in range(1, 8):
        sh_ref[s - 1] = ext_ref[s:s + sh_rows, :]

    base = CONV_HALO - (CONV_WIDTH - 1)

    n_groups = CONV_RC // 8
    taps_of_shift = [[(a, 8 * a + s - base) for a in range(5) if 0 <= 8 * a + s - base < CONV_WIDTH]
                     for s in range(8)]

    def conv_rows(rc, carry):
        r0 = pl.multiple_of(rc * CONV_RC, CONV_RC)
        for lo in range(0, D_CONV, LANES):
            cols = slice(lo, lo + LANES)
            w = [jnp.broadcast_to(cw_ref[k:k + 1, cols], (8, LANES)) for k in range(CONV_WIDTH)]
            acc = [jnp.broadcast_to(cb_ref[:, cols], (8, LANES))] * n_groups
            for s in range(8):
                taps = taps_of_shift[s]
                for j in range(n_groups + max(a for a, _ in taps)):
                    used = [(a, k) for a, k in taps if 0 <= j - a < n_groups]
                    if not used:
                        continue
                    rows = pl.ds(r0 + 8 * j, 8)
                    x = ext_ref[rows, cols] if s == 0 else sh_ref[s - 1, rows, cols]
                    for a, k in used:
                        acc[j - a] = acc[j - a] + w[k] * x
            for i in range(n_groups):
                conv_ref[pl.ds(r0 + 8 * i, 8), cols] = acc[i]
        return carry

    lax.fori_loop(0, tl // CONV_RC, conv_rows, 0)
    h = _layer_norm(conv_ref[...], lg_ref[...], lb_ref[...])
    h = h * _sigmoid(h)
    y = jnp.dot(h.astype(BF16), wbf_ref[...], preferred_element_type=F32)
    o_ref[...] = (y * gate_ref[...].astype(F32)).astype(o_ref.dtype)


def _convbranch(c, gates, conv_w, conv_b3, ln_g3, ln_b3, w_conv_out, layer, bsz, seqlen, tl=256):
    t = c.shape[0]
    nl = seqlen // tl
    vec = pl.BlockSpec((None, 1, D_CONV), lambda b, i: (layer, 0, 0))
    return pl.pallas_call(
        functools.partial(_convbranch_kernel, tl=tl),
        grid=(bsz, nl),
        in_specs=[pl.BlockSpec((tl, D_CONV), lambda b, i: (b * nl + i, 0)),
                  pl.BlockSpec((None, CONV_WIDTH, D_CONV), lambda b, i: (layer, 0, 0)),
                  vec, vec, vec,
                  pl.BlockSpec((None, D_CONV, D_MODEL), lambda b, i: (layer, 0, 0)),
                  pl.BlockSpec((tl, D_MODEL), lambda b, i: (b * nl + i, 0))],
        out_specs=pl.BlockSpec((tl, D_MODEL), lambda b, i: (b * nl + i, 0)),
        out_shape=jax.ShapeDtypeStruct((t, D_MODEL), BF16),
        scratch_shapes=[pltpu.VMEM((tl + CONV_HALO, D_CONV), F32),
                        pltpu.VMEM((7, tl + CONV_HALO - 8, D_CONV), F32),
                        pltpu.VMEM((tl, D_CONV), F32),
                        pltpu.VMEM((D_CONV, D_MODEL), BF16)],
        compiler_params=_params(2),
        name="conv_module",
    )(c, conv_w, conv_b3, ln_g3, ln_b3, w_conv_out, gates)


def _expand_heads(v, g, lane_in_pair):
    rows = v.shape[0]
    b = [jnp.broadcast_to(v[:, g * HEADS_PER_GROUP + r:g * HEADS_PER_GROUP + r + 1], (rows, LANES))
         for r in range(HEADS_PER_GROUP)]
    first = lane_in_pair[:rows] < SSM_HEAD_DIM
    return jnp.concatenate([jnp.where(first, b[0], b[1]), jnp.where(first, b[2], b[3])], axis=1)


def _ssd_kernel(xbc_ref, u_ref, zs_ref, wdt_ref, dtb_ref, cw_ref, cb_ref, alog_ref, dskip_ref, ng_ref,
                o_ref, ext_ref, sh_ref, act_ref, state_ref, *, q):
    @pl.when(pl.program_id(1) == 0)
    def _():
        ext_ref[0:SSM_HALO, :] = jnp.zeros((SSM_HALO, D_XBC), F32)
        state_ref[...] = jnp.zeros(state_ref.shape, F32)

    @pl.when(pl.program_id(1) > 0)
    def _():
        ext_ref[0:SSM_HALO, :] = ext_ref[q:q + SSM_HALO, :]

    ext_ref[SSM_HALO:SSM_HALO + q, :] = xbc_ref[...].astype(F32)

    base = SSM_HALO - (SSM_CONV_WIDTH - 1)
    for k in range(SSM_CONV_WIDTH - 1):
        sh_ref[k] = ext_ref[base + k:base + k + q, :]
    for lo in range(0, D_XBC, CONV_CW):
        for r0 in range(0, q, CONV_RC):
            cols = slice(lo, lo + CONV_CW)
            acc = cb_ref[:, cols] + cw_ref[SSM_CONV_WIDTH - 1:SSM_CONV_WIDTH, cols] * \
                ext_ref[SSM_HALO + r0:SSM_HALO + r0 + CONV_RC, cols]
            for k in range(SSM_CONV_WIDTH - 1):
                acc = acc + cw_ref[k:k + 1, cols] * sh_ref[k, r0:r0 + CONV_RC, cols]
            act_ref[r0:r0 + CONV_RC, cols] = acc * _sigmoid(acc)

    dt_raw = jnp.dot(u_ref[...], wdt_ref[...], preferred_element_type=F32) + dtb_ref[...]
    dt = jnp.maximum(dt_raw, 0.0) + jnp.log1p(jnp.exp(-jnp.abs(dt_raw)))
    for r0 in range(0, q, SSD_Q):
        _scan_chunk(slice(r0, r0 + SSD_Q), dt[r0:r0 + SSD_Q, :], act_ref, zs_ref, alog_ref, dskip_ref,
                    ng_ref, o_ref, state_ref)


def _scan_chunk(rows, dt, act_ref, zs_ref, alog_ref, dskip_ref, ng_ref, o_ref, state_ref):
    q = SSD_Q
    adt = dt * (-jnp.exp(alog_ref[...]))
    row = lax.broadcasted_iota(I32, (q, q), 0)
    col = lax.broadcasted_iota(I32, (q, q), 1)
    causal = row >= col
    tril = jnp.where(causal, 1.0, 0.0).astype(F32)
    acs = jnp.dot(tril, adt, preferred_element_type=F32, precision=lax.Precision.HIGHEST)
    acs_t = acs.T
    dt_t = dt.T
    last = acs[q - 1:q, :]
    exp_acs = jnp.exp(acs)
    dt_decay = dt * jnp.exp(last - acs)
    chunk_decay = jnp.exp(last)
    lane_in_pair = lax.broadcasted_iota(I32, (q, LANES), 1)
    head_of_lane = lax.broadcasted_iota(I32, (1, GROUP_CH), 1) // SSM_HEAD_DIM

    b_off = D_INNER
    c_off = D_INNER + SSM_GROUPS * D_STATE
    for g in range(SSM_GROUPS):
        ch = slice(g * GROUP_CH, (g + 1) * GROUP_CH)
        xg = act_ref[rows, ch]
        xg_bf = xg.astype(BF16)
        bg = act_ref[rows, b_off + g * D_STATE:b_off + (g + 1) * D_STATE]
        cg = act_ref[rows, c_off + g * D_STATE:c_off + (g + 1) * D_STATE].astype(BF16)
        cb = lax.dot_general(cg, bg.astype(BF16), (((1,), (1,)), ((), ())),
                             preferred_element_type=F32)
        ms, xblocks = [], []
        for r in range(HEADS_PER_GROUP):
            h = g * HEADS_PER_GROUP + r
            lmat = jnp.exp(jnp.where(causal, acs[:, h:h + 1] - acs_t[h:h + 1, :], -jnp.inf))
            ms.append((cb * lmat * dt_t[h:h + 1, :]).astype(BF16))
            head_mask = jnp.where(head_of_lane == r, 1.0, 0.0).astype(BF16)
            xblocks.append(xg_bf * head_mask)
        y_diag = jnp.dot(jnp.concatenate(ms, axis=1), jnp.concatenate(xblocks, axis=0),
                         preferred_element_type=F32)
        s_prev = state_ref[g]
        y_off = jnp.dot(cg, s_prev.astype(BF16), preferred_element_type=F32)
        yg = y_diag + y_off * _expand_heads(exp_acs, g, lane_in_pair) + xg * dskip_ref[:, ch]
        xw = (xg * _expand_heads(dt_decay, g, lane_in_pair)).astype(BF16)
        dec = _expand_heads(chunk_decay, g, lane_in_pair)
        state_ref[g] = s_prev * dec + jnp.dot(bg.T.astype(BF16), xw, preferred_element_type=F32)

        yz = yg * zs_ref[rows, g * GROUP_CH:(g + 1) * GROUP_CH].astype(F32)
        ms = jnp.mean(yz * yz, axis=-1, keepdims=True)
        yn = yz * lax.rsqrt(ms + RMS_EPS) * ng_ref[:, g * GROUP_CH:(g + 1) * GROUP_CH]
        o_ref[rows, g * GROUP_CH:(g + 1) * GROUP_CH] = yn.astype(o_ref.dtype)


def _ssd(xbc, u, zs, w_dt, dtb3, ssm_conv_w, ssm_conv_b3, alog3, dskip3, ng3, layer, bsz, seqlen):
    t = xbc.shape[0]
    q = min(SSD_STEP, seqlen)
    nq = seqlen // q
    tile = lambda width: pl.BlockSpec((q, width), lambda b, i: (b * nq + i, 0))
    vec = lambda width: pl.BlockSpec((None, 1, width), lambda b, i: (layer, 0, 0))
    return pl.pallas_call(
        functools.partial(_ssd_kernel, q=q),
        grid=(bsz, nq),
        in_specs=[tile(D_XBC), tile(D_MODEL), tile(D_INNER),
                  pl.BlockSpec((None, D_MODEL, LANES), lambda b, i: (layer, 0, 0)), vec(LANES),
                  pl.BlockSpec((None, SSM_CONV_WIDTH, D_XBC), lambda b, i: (layer, 0, 0)),
                  vec(D_XBC), vec(LANES), vec(D_INNER), vec(D_INNER)],
        out_specs=tile(D_INNER),
        out_shape=jax.ShapeDtypeStruct((t, D_INNER), BF16),
        scratch_shapes=[pltpu.VMEM((q + SSM_HALO, D_XBC), F32),
                        pltpu.VMEM((SSM_CONV_WIDTH - 1, q, D_XBC), F32),
                        pltpu.VMEM((q, D_XBC), F32),
                        pltpu.VMEM((SSM_GROUPS, D_STATE, GROUP_CH), F32)],
        compiler_params=_params(2),
        name="ssd_mixer",
    )(xbc, u, zs, w_dt, dtb3, ssm_conv_w, ssm_conv_b3, alog3, dskip3, ng3)


def _first_argmax(vals):
    best, idx = vals[0], jnp.zeros(vals[0].shape, I32)
    for j in range(1, len(vals)):
        gt = vals[j] > best
        idx = jnp.where(gt, j, idx)
        best = jnp.where(gt, vals[j], best)
    return idx, best


def _select(idx, vals):
    out = vals[len(vals) - 1]
    for j in range(len(vals) - 2, -1, -1):
        out = jnp.where(idx == j, vals[j], out)
    return out


def _outproj_kernel(yn_ref, y1_ref, gs_ref, x_ref, wso_ref, wo_ref, lg_ref, lb_ref, wr_ref, rb_ref,
                    x1_ref, x1b_ref, pos_ref, wts_ref, cnt_ref, seg_ref,
                    wso_bf, wo_bf, base_ref, *, tm):
    @pl.when(pl.program_id(0) == 0)
    def _():
        wso_bf[...] = wso_ref[...].astype(BF16)
        wo_bf[...] = wo_ref[...].astype(BF16)
        base_ref[...] = jnp.zeros(base_ref.shape, F32)

    y_ssm = jnp.dot(yn_ref[...], wso_bf[...], preferred_element_type=F32)
    merged = y1_ref[...].astype(F32) + gs_ref[...].astype(F32) * y_ssm
    mix = jnp.dot(merged.astype(BF16), wo_bf[...], preferred_element_type=F32)
    x1 = _layer_norm(ALPHA * x_ref[...] + mix, lg_ref[...], lb_ref[...])
    x1_ref[...] = x1
    x1b_ref[...] = x1.astype(BF16)

    logits = lax.dot_general(wr_ref[...], x1, (((1,), (1,)), ((), ())),
                             preferred_element_type=F32, precision=lax.Precision.HIGHEST)
    scores = _sigmoid(logits)
    sel = scores + rb_ref[...]
    sel_rows = [sel[e:e + 1, :] for e in range(N_EXPERTS)]
    sc_rows = [scores[e:e + 1, :] for e in range(N_EXPERTS)]

    gscores = []
    for gi in range(N_EXPERT_GROUPS):
        v = sel_rows[gi * EXPERTS_PER_GROUP:(gi + 1) * EXPERTS_PER_GROUP]
        best = None
        for a in range(EXPERTS_PER_GROUP):
            for b in range(a + 1, EXPERTS_PER_GROUP):
                s = v[a] + v[b]
                best = s if best is None else jnp.maximum(best, s)
        gscores.append(best)
    grp, _ = _first_argmax(gscores)

    sel_in = [_select(grp, [sel_rows[gi * EXPERTS_PER_GROUP + j] for gi in range(N_EXPERT_GROUPS)])
              for j in range(EXPERTS_PER_GROUP)]
    sc_in = [_select(grp, [sc_rows[gi * EXPERTS_PER_GROUP + j] for gi in range(N_EXPERT_GROUPS)])
             for j in range(EXPERTS_PER_GROUP)]
    i1, _ = _first_argmax(sel_in)
    neg = jnp.full(sel_in[0].shape, -jnp.inf, F32)
    i2, _ = _first_argmax([jnp.where(i1 == j, neg, sel_in[j]) for j in range(EXPERTS_PER_GROUP)])
    s1 = _select(i1, sc_in)
    s2 = _select(i2, sc_in)
    tot = s1 + s2
    e1 = grp * EXPERTS_PER_GROUP + i1
    e2 = grp * EXPERTS_PER_GROUP + i2
    wts_ref[0:1, :] = s1 / tot
    wts_ref[1:2, :] = s2 / tot

    eio = lax.broadcasted_iota(I32, (N_EXPERTS, tm), 0)
    oh1 = jnp.where(eio == e1, 1.0, 0.0).astype(F32)
    oh2 = jnp.where(eio == e2, 1.0, 0.0).astype(F32)
    both = oh1 + oh2
    srow = lax.broadcasted_iota(I32, (tm, tm), 0)
    scol = lax.broadcasted_iota(I32, (tm, tm), 1)
    before = jnp.where(srow < scol, 1.0, 0.0).astype(BF16)
    cum = jnp.dot(both.astype(BF16), before, preferred_element_type=F32)
    cnt8 = jnp.floor((jnp.sum(both, axis=1, keepdims=True) + 7.0) * 0.125) * 8.0
    cnt8_l = jnp.broadcast_to(cnt8, (N_EXPERTS, LANES))
    erow = lax.broadcasted_iota(I32, (N_EXPERTS, N_EXPERTS), 0)
    ecol = lax.broadcasted_iota(I32, (N_EXPERTS, N_EXPERTS), 1)
    seg_off = jnp.dot(jnp.where(ecol < erow, 1.0, 0.0).astype(F32), cnt8_l,
                      preferred_element_type=F32, precision=lax.Precision.HIGHEST)
    base = base_ref[...]
    in_tile = seg_off[:, 0:1] + cum
    pos_ref[0:1, :] = jnp.sum(oh1 * in_tile, axis=0, keepdims=True).astype(I32)
    pos_ref[1:2, :] = jnp.sum(oh2 * in_tile, axis=0, keepdims=True).astype(I32)
    lane = lax.broadcasted_iota(I32, (N_EXPERTS, LANES), 1)
    seg_ref[...] = jnp.where(lane == 0, seg_off, jnp.where(lane == 1, cnt8_l, base)).astype(I32)
    base_ref[...] = base + cnt8_l
    cnt_ref[...] = base_ref[...].astype(I32)


def _outproj(yn, y1g, gates, x, w_ssm_out, w_out, ln_g3, ln_b3, wr_t, rbias, layer, tm):
    t = x.shape[0]
    tile = lambda width: pl.BlockSpec((tm, width), lambda m: (m, 0))
    vec = pl.BlockSpec((None, 1, D_MODEL), lambda m: (layer, 0, 0))
    pair = pl.BlockSpec((2, tm), lambda m: (0, m))
    return pl.pallas_call(
        functools.partial(_outproj_kernel, tm=tm),
        grid=(t // tm,),
        in_specs=[tile(D_INNER), tile(D_MODEL),
                  pl.BlockSpec((tm, D_MODEL), lambda m: (m, 1)),
                  tile(D_MODEL),
                  pl.BlockSpec((None, D_INNER, D_MODEL), lambda m: (layer, 0, 0)),
                  pl.BlockSpec((None, D_MODEL, D_MODEL), lambda m: (layer, 0, 0)),
                  vec, vec,
                  pl.BlockSpec((N_EXPERTS, D_MODEL), lambda m: (0, 0)),
                  pl.BlockSpec((N_EXPERTS, 1), lambda m: (0, 0))],
        out_specs=[tile(D_MODEL), tile(D_MODEL), pair, pair,
                   pl.BlockSpec((N_EXPERTS, LANES), lambda m: (0, 0)),
                   pl.BlockSpec((None, N_EXPERTS, LANES), lambda m: (m, 0, 0))],
        out_shape=[jax.ShapeDtypeStruct((t, D_MODEL), F32),
                   jax.ShapeDtypeStruct((t, D_MODEL), BF16),
                   jax.ShapeDtypeStruct((2, t), I32),
                   jax.ShapeDtypeStruct((2, t), F32),
                   jax.ShapeDtypeStruct((N_EXPERTS, LANES), I32),
                   jax.ShapeDtypeStruct((t // tm, N_EXPERTS, LANES), I32)],
        scratch_shapes=[pltpu.VMEM((D_INNER, D_MODEL), BF16),
                        pltpu.VMEM((D_MODEL, D_MODEL), BF16),
                        pltpu.VMEM((N_EXPERTS, LANES), F32)],
        compiler_params=_params(1),
        name="outproj_ln_router",
    )(yn, y1g, gates, x, w_ssm_out, w_out, ln_g3, ln_b3, wr_t, rbias)


def _segment_blocks(tile, segoff_ref, cnt8_ref, gdst_ref, make_copy, act):
    for e in range(N_EXPERTS):
        idx = tile * N_EXPERTS + e
        n8 = lax.shift_right_logical(cnt8_ref[idx], 3)
        local = segoff_ref[idx]
        glob = gdst_ref[idx]
        for bit in SEGMENT_BITS:
            rows = 8 * bit
            hit = (n8 & bit) != 0

            @pl.when(hit)
            def _(local=local, glob=glob, rows=rows):
                act(make_copy(pl.multiple_of(local, 8), pl.multiple_of(glob, 8), rows))

            step = jnp.where(hit, rows, 0)
            local = local + step
            glob = glob + step


def _dispatch_kernel(fill_ref, segoff_ref, cnt8_ref, gdst_ref, pos_ref, x_ref, xs_hbm,
                     sorted_ref, zeros_ref, sem, blk_sem, seg_sems, *, tm, n_blk, n_tiles):
    m = pl.program_id(0)
    slot = lax.rem(m, 2)

    def zero_rows(row):
        return pltpu.make_async_copy(zeros_ref.at[pl.ds(0, 8)],
                                     xs_hbm.at[pl.ds(pl.multiple_of(row, 8), 8)], sem)

    def zero_block(b):
        return pltpu.make_async_copy(
            zeros_ref, xs_hbm.at[pl.ds(pl.multiple_of(b * MOE_ROWS, MOE_ROWS), MOE_ROWS)], blk_sem)

    @pl.when(m == 0)
    def _():
        zeros_ref[...] = jnp.zeros(zeros_ref.shape, F32)
        n_active = fill_ref[2 * N_EXPERTS]
        for e in range(N_EXPERTS):
            pad_start = fill_ref[e]
            lax.fori_loop(0, fill_ref[N_EXPERTS + e],
                          lambda j, c: (zero_rows(pad_start + 8 * j).start(), c)[1], 0)
        lax.fori_loop(n_active, n_blk, lambda b, c: (zero_block(b).start(), c)[1], 0)
        lax.fori_loop(0, fill_ref[2 * N_EXPERTS + 1], lambda j, c: (zero_rows(0).wait(), c)[1], 0)
        lax.fori_loop(n_active, n_blk, lambda b, c: (zero_block(b).wait(), c)[1], 0)

    def for_each_segment_block(tile, s, act):
        copy = lambda local, glob, rows: pltpu.make_async_copy(
            sorted_ref.at[s, pl.ds(local, rows)], xs_hbm.at[pl.ds(glob, rows)], seg_sems.at[s])
        _segment_blocks(tile, segoff_ref, cnt8_ref, gdst_ref, copy, act)

    start = lambda copy: copy.start()
    wait = lambda copy: copy.wait()

    @pl.when(m >= 2)
    def _():
        for_each_segment_block(m - 2, slot, wait)

    pos = pos_ref[...]
    srow = lax.broadcasted_iota(I32, (sorted_ref.shape[1], tm), 0)
    onehot = jnp.where(srow == pos[0:1, :], 1.0, jnp.where(srow == pos[1:2, :], 1.0, 0.0)).astype(BF16)
    sorted_ref[slot] = jnp.dot(onehot, x_ref[...], preferred_element_type=F32)
    for_each_segment_block(m, slot, start)

    @pl.when(m == n_tiles - 1)
    def _():
        if n_tiles > 1:
            for_each_segment_block(m - 1, 1 - slot, wait)
        for_each_segment_block(m, slot, wait)


def _dispatch(fill, seg_off, seg_cnt8, seg_dst, pos, x1b, n_blk, tm):
    t = x1b.shape[0]
    n_tiles = t // tm
    sorted_rows = 2 * tm + 8 * N_EXPERTS
    grid_spec = pltpu.PrefetchScalarGridSpec(
        num_scalar_prefetch=4,
        grid=(n_tiles,),
        in_specs=[pl.BlockSpec((2, tm), lambda m, *_: (0, m)),
                  pl.BlockSpec((tm, D_MODEL), lambda m, *_: (m, 0))],
        out_specs=pl.BlockSpec(memory_space=pl.ANY),
        scratch_shapes=[pltpu.VMEM((2, sorted_rows, D_MODEL), F32),
                        pltpu.VMEM((MOE_ROWS, D_MODEL), F32),
                        pltpu.SemaphoreType.DMA(()), pltpu.SemaphoreType.DMA(()),
                        pltpu.SemaphoreType.DMA((2,))],
    )
    return pl.pallas_call(
        functools.partial(_dispatch_kernel, tm=tm, n_blk=n_blk, n_tiles=n_tiles),
        grid_spec=grid_spec,
        out_shape=jax.ShapeDtypeStruct((n_blk * MOE_ROWS, D_MODEL), F32),
        compiler_params=_params(1),
        name="moe_dispatch",
    )(fill, seg_off, seg_cnt8, seg_dst, pos, x1b)


def _expert_kernel(src_ref, exp_ref, nvalid_ref, xs_ref, wg_ref, wu_ref, wd_ref, y_ref,
                   wgu_bf, wd_bf):
    i = pl.program_id(0)
    nvalid = nvalid_ref[i]
    changed = (i == 0) | (exp_ref[i] != exp_ref[jnp.maximum(i - 1, 0)])

    @pl.when((nvalid > 0) & changed)
    def _():
        wgu_bf[:, :D_EXPERT] = wg_ref[...].astype(BF16)
        wgu_bf[:, D_EXPERT:] = wu_ref[...].astype(BF16)
        wd_bf[...] = wd_ref[...].astype(BF16)

    @pl.when(nvalid > 0)
    def _():
        rows = lax.broadcasted_iota(I32, (MOE_ROWS, 1), 0)
        x = jnp.where(rows < nvalid, xs_ref[...], 0.0).astype(BF16)
        gu = jnp.dot(x, wgu_bf[...], preferred_element_type=F32)
        hg = gu[:, :D_EXPERT]
        h = hg * _sigmoid(hg) * gu[:, D_EXPERT:]
        y_ref[...] = jnp.dot(h.astype(BF16), wd_bf[...], preferred_element_type=F32)

    @pl.when(nvalid == 0)
    def _():
        y_ref[...] = jnp.zeros(y_ref.shape, F32)


def _experts(blk_src, blk_exp, blk_nvalid, xs, wg, wu, wd, layer):
    n_rows = xs.shape[0]
    n_blk = n_rows // MOE_ROWS
    grid_spec = pltpu.PrefetchScalarGridSpec(
        num_scalar_prefetch=3,
        grid=(n_blk,),
        in_specs=[pl.BlockSpec((MOE_ROWS, D_MODEL), lambda i, src, exp, nv: (src[i], 0)),
                  pl.BlockSpec((None, None, D_MODEL, D_EXPERT),
                               lambda i, src, exp, nv: (layer, exp[i], 0, 0)),
                  pl.BlockSpec((None, None, D_MODEL, D_EXPERT),
                               lambda i, src, exp, nv: (layer, exp[i], 0, 0)),
                  pl.BlockSpec((None, None, D_EXPERT, D_MODEL),
                               lambda i, src, exp, nv: (layer, exp[i], 0, 0))],
        out_specs=pl.BlockSpec((MOE_ROWS, D_MODEL), lambda i, src, exp, nv: (i, 0)),
        scratch_shapes=[pltpu.VMEM((D_MODEL, 2 * D_EXPERT), BF16),
                        pltpu.VMEM((D_EXPERT, D_MODEL), BF16)],
    )
    return pl.pallas_call(
        _expert_kernel,
        grid_spec=grid_spec,
        out_shape=jax.ShapeDtypeStruct((n_rows, D_MODEL), F32),
        compiler_params=_params(1),
        name="moe_experts",
    )(blk_src, blk_exp, blk_nvalid, xs, wg, wu, wd)


def _moe_plan(counts, seg, t):
    cnt = counts[:, 0]
    nblk_e = (cnt + MOE_ROWS - 1) // MOE_ROWS
    blk_end = jnp.cumsum(nblk_e)
    blk_start = blk_end - nblk_e
    n_active = blk_end[N_EXPERTS - 1]
    experts = jnp.arange(N_EXPERTS, dtype=I32)
    row_start = blk_start * MOE_ROWS
    n_tiles = seg.shape[0]
    n_blk = -(-(2 * t + 8 * N_EXPERTS * n_tiles) // MOE_ROWS) + N_EXPERTS
    ids = jnp.arange(n_blk, dtype=I32)
    src = jnp.minimum(ids, n_active - 1)
    exp = jnp.minimum(jnp.sum((blk_end[None, :] <= src[:, None]).astype(I32), axis=1), N_EXPERTS - 1)
    onehot = (exp[:, None] == experts[None, :]).astype(I32)
    pick = lambda table: jnp.sum(onehot * table[None, :], axis=1)
    left = pick(cnt) - (src - pick(blk_start)) * MOE_ROWS
    nvalid = jnp.where(ids < n_active, jnp.clip(left, 0, MOE_ROWS), 0)
    pad8 = (nblk_e * MOE_ROWS - cnt) // 8
    fill = jnp.concatenate([row_start + cnt, pad8, jnp.stack([n_active, jnp.sum(pad8)])]).astype(I32)
    seg_off = seg[:, :, 0].reshape(-1)
    seg_cnt8 = seg[:, :, 1].reshape(-1)
    seg_dst = (row_start[None, :] + seg[:, :, 2]).reshape(-1)
    return src.astype(I32), exp, nvalid.astype(I32), fill, seg_off, seg_cnt8, seg_dst, n_blk


def _final_kernel(segoff_ref, cnt8_ref, gsrc_ref, x1_ref, x1b_ref, p_ref, wt_ref, pos_ref, wpu_ref,
                  wpg_ref, bpg_ref, lg_ref, lb_ref, y_hbm, x2_ref, x2b_ref, ys_ref, wpu_bf, wpg_bf, sems,
                  *, tm, n_tiles):
    m = pl.program_id(0)
    slot = lax.rem(m, 2)

    def for_each_segment_block(tile, s, act):
        copy = lambda local, glob, rows: pltpu.make_async_copy(
            y_hbm.at[pl.ds(glob, rows)], ys_ref.at[s, pl.ds(local, rows)], sems.at[s])
        _segment_blocks(tile, segoff_ref, cnt8_ref, gsrc_ref, copy, act)

    @pl.when(m == 0)
    def _():
        wpu_bf[...] = wpu_ref[...].astype(BF16)
        wpg_bf[...] = wpg_ref[...].astype(BF16)
        for_each_segment_block(0, 0, lambda copy: copy.start())

    @pl.when(m + 1 < n_tiles)
    def _():
        for_each_segment_block(m + 1, 1 - slot, lambda copy: copy.start())

    up = jnp.dot(p_ref[...].astype(BF16), wpu_bf[...], preferred_element_type=F32)
    gate = _sigmoid(jnp.dot(x1b_ref[...], wpg_bf[...], preferred_element_type=F32) + bpg_ref[...])
    resid = ALPHA * x1_ref[...] + up * gate

    for_each_segment_block(m, slot, lambda copy: copy.wait())

    n_sorted = ys_ref.shape[1]
    used = segoff_ref[m * N_EXPERTS + N_EXPERTS - 1] + cnt8_ref[m * N_EXPERTS + N_EXPERTS - 1]
    srow = lax.broadcasted_iota(I32, (n_sorted, 1), 0)
    ys = jnp.where(srow < used, ys_ref[slot], 0.0).astype(BF16)
    pos = pos_ref[...]
    scol = lax.broadcasted_iota(I32, (tm, n_sorted), 1)
    wt = wt_ref[...]
    moe = jnp.zeros((tm, D_MODEL), F32)
    for k in range(TOP_K):
        pick = jnp.where(scol == pos[:, k:k + 1], 1.0, 0.0).astype(BF16)
        moe = moe + wt[:, k:k + 1] * jnp.dot(pick, ys, preferred_element_type=F32)
    x2 = _layer_norm(resid + moe, lg_ref[...], lb_ref[...])
    x2_ref[...] = x2
    x2b_ref[...] = x2.astype(BF16)


def _final(seg_off, seg_cnt8, seg_src, x1, x1b, p, wt_tok, pos_tok, w_ple_up, w_ple_gate, b_pg3,
           ln_g3, ln_b3, y_rows, layer, tm):
    t = x1.shape[0]
    n_tiles = t // tm
    sorted_rows = 2 * tm + 8 * N_EXPERTS
    tile = lambda width: pl.BlockSpec((tm, width), lambda m, *_: (m, 0))
    vec = pl.BlockSpec((None, 1, D_MODEL), lambda m, *_: (layer, 0, 0))
    grid_spec = pltpu.PrefetchScalarGridSpec(
        num_scalar_prefetch=3,
        grid=(n_tiles,),
        in_specs=[tile(D_MODEL), tile(D_MODEL),
                  pl.BlockSpec((None, tm, PLE_DIM), lambda m, *_: (layer, m, 0)),
                  tile(2), tile(2),
                  pl.BlockSpec((None, PLE_DIM, D_MODEL), lambda m, *_: (layer, 0, 0)),
                  pl.BlockSpec((None, D_MODEL, D_MODEL), lambda m, *_: (layer, 0, 0)),
                  vec, vec, vec,
                  pl.BlockSpec(memory_space=pl.ANY)],
        out_specs=[tile(D_MODEL), tile(D_MODEL)],
        scratch_shapes=[pltpu.VMEM((2, sorted_rows, D_MODEL), F32),
                        pltpu.VMEM((PLE_DIM, D_MODEL), BF16), pltpu.VMEM((D_MODEL, D_MODEL), BF16),
                        pltpu.SemaphoreType.DMA((2,))],
    )
    return pl.pallas_call(
        functools.partial(_final_kernel, tm=tm, n_tiles=n_tiles),
        grid_spec=grid_spec,
        out_shape=[jax.ShapeDtypeStruct((t, D_MODEL), F32),
                   jax.ShapeDtypeStruct((t, D_MODEL), BF16)],
        compiler_params=_params(1),
        name="combine_ple_ln",
    )(seg_off, seg_cnt8, seg_src, x1, x1b, p, wt_tok, pos_tok, w_ple_up, w_ple_gate, b_pg3,
      ln_g3, ln_b3, y_rows)


def kernel(x, p, w_in, b_glu, b_branch_gate, conv_w, conv_b, conv_ln_g, conv_ln_b, w_conv_out,
           ssm_conv_w, ssm_conv_b, dt_bias, a_log, d_skip, ssm_norm_g, w_ssm_out, w_out,
           ln1_g, ln1_b, w_router, router_bias, w_exp_gate, w_exp_up, w_exp_down,
           w_ple_up, w_ple_gate, b_ple_gate, ln2_g, ln2_b):
    bsz, seqlen, d = x.shape
    depth = w_in.shape[0]
    t = bsz * seqlen
    tm_moe = min(MOE_TILE, t)

    row3 = lambda a: a.reshape(a.shape[0], 1, a.shape[1])
    pad_lanes = lambda a: jnp.pad(a, ((0, 0), (0, LANES - a.shape[1])))
    b_glu3, b_gate3 = row3(b_glu), row3(b_branch_gate)
    conv_b3, cln_g3, cln_b3 = row3(conv_b), row3(conv_ln_g), row3(conv_ln_b)
    ssm_conv_b3, ng3 = row3(ssm_conv_b), row3(ssm_norm_g)
    dtb3, alog3 = row3(pad_lanes(dt_bias)), row3(pad_lanes(a_log))
    dskip3 = row3(jnp.repeat(d_skip, SSM_HEAD_DIM, axis=1))
    ln1_g3, ln1_b3, ln2_g3, ln2_b3 = row3(ln1_g), row3(ln1_b), row3(ln2_g), row3(ln2_b)
    b_pg3 = row3(b_ple_gate)
    w_in_t = jnp.swapaxes(w_in, 1, 2)
    w_gate, w_dt = _tail_weights(w_in_t)
    wr_t = w_router.T
    rbias = router_bias.reshape(N_EXPERTS, 1)
    p2 = p.reshape(depth, t, PLE_DIM)

    xf = x.reshape(t, d)
    xb = xf.astype(BF16)
    for i in range(depth):
        c = _glu(xb, w_in_t, b_glu3, i)
        zs = _matmul(xb, w_in_t, lambda n: (i, OFF_Z // 1024 + n, 0), None, None,
                     D_INNER, 1024, "silu", BF16, w_rows_are_outputs=True)
        xbc = _matmul(xb, w_in_t, lambda n: (i, OFF_XBC // 1024 + n, 0), None, None,
                      D_XBC, 1024, "none", BF16, w_rows_are_outputs=True)
        gates = _matmul(xb, w_gate, lambda n: (i, 0, n), b_gate3, lambda n: (i, 0, n),
                        2 * D_MODEL, 1024, "sigmoid", BF16)
        y1g = _convbranch(c, gates, conv_w, conv_b3, cln_g3, cln_b3, w_conv_out, i, bsz, seqlen)
        yn = _ssd(xbc, xb, zs, w_dt, dtb3, ssm_conv_w, ssm_conv_b3, alog3, dskip3, ng3, i, bsz, seqlen)
        x1, x1b, pos, wts, counts, seg = _outproj(
            yn, y1g, gates, xf, w_ssm_out, w_out, ln1_g3, ln1_b3, wr_t, rbias, i, tm_moe)
        (blk_src, blk_exp, blk_nvalid, fill, seg_off, seg_cnt8, seg_row,
         n_blk) = _moe_plan(counts, seg, t)
        xs = _dispatch(fill, seg_off, seg_cnt8, seg_row, pos, x1b, n_blk, tm_moe)
        y_rows = _experts(blk_src, blk_exp, blk_nvalid, xs, w_exp_gate, w_exp_up, w_exp_down, i)
        xf, xb = _final(seg_off, seg_cnt8, seg_row, x1, x1b, p2, wts.T, pos.T,
                        w_ple_up, w_ple_gate, b_pg3, ln2_g3, ln2_b3, y_rows, i, tm_moe)
    return xf.reshape(bsz, seqlen, d)
```

```python
import functools

import jax
import jax.numpy as jnp
from jax import lax
from jax.experimental import pallas as pl
from jax.experimental.pallas import tpu as pltpu

F32 = jnp.float32
BF16 = jnp.bfloat16
I32 = jnp.int32

D_MODEL = 1024
D_CONV = 1024
CONV_WIDTH = 31
D_INNER = 2048
SSM_HEAD_DIM = 64
SSM_HEADS = 32
SSM_GROUPS = 8
HEADS_PER_GROUP = 4
D_STATE = 128
SSM_CONV_WIDTH = 4
D_XBC = D_INNER + 2 * SSM_GROUPS * D_STATE
GROUP_CH = HEADS_PER_GROUP * SSM_HEAD_DIM
N_EXPERTS = 16
N_EXPERT_GROUPS = 4
TOP_K = 2
EXPERTS_PER_GROUP = 4
D_EXPERT = 512
PLE_DIM = 256
DEPTH = 4
ALPHA = (2.0 * DEPTH) ** 0.25
LN_EPS = 1e-5
RMS_EPS = 1e-5

LANES = 128
CONV_HALO = 32
CONV_RC = 64
CONV_CW = 256
SSM_HALO = 8
SSD_Q = 128
SSD_STEP = 256
MOE_ROWS = 512
MOE_TILE = 512
SEGMENT_BITS = (64, 32, 16, 8, 4, 2, 1)
VMEM_LIMIT = 48 * 1024 * 1024
INPROJ_TM = 2048
INPROJ_VMEM_LIMIT = 56 * 1024 * 1024

OFF_GLU = 0
OFF_Z = 2 * D_CONV
OFF_XBC = OFF_Z + D_INNER
OFF_DT = OFF_XBC + D_XBC
OFF_GATE = OFF_DT + SSM_HEADS


def _sigmoid(x):
    return 1.0 / (1.0 + jnp.exp(-x))


def _layer_norm(x, g, b):
    mu = jnp.mean(x, axis=-1, keepdims=True)
    xc = x - mu
    var = jnp.mean(xc * xc, axis=-1, keepdims=True)
    return xc * lax.rsqrt(var + LN_EPS) * g + b


def _params(n_axes, vmem_limit=VMEM_LIMIT):
    return pltpu.CompilerParams(dimension_semantics=("arbitrary",) * n_axes,
                                vmem_limit_bytes=vmem_limit)


def _mm_kernel(x_ref, w_ref, *rest, act, has_bias, w_rows_are_outputs):
    if has_bias:
        b_ref, o_ref, wbf_ref = rest
    else:
        o_ref, wbf_ref = rest

    @pl.when(pl.program_id(1) == 0)
    def _():
        w = w_ref[...]
        wbf_ref[...] = (w.T if w_rows_are_outputs else w).astype(BF16)

    acc = jnp.dot(x_ref[...], wbf_ref[...], preferred_element_type=F32)
    if has_bias:
        acc = acc + b_ref[...]
    if act == "sigmoid":
        acc = _sigmoid(acc)
    o_ref[...] = acc.astype(o_ref.dtype)


def _matmul(x, w, w_index, bias, b_index, n_out, tn, act, out_dtype, w_rows_are_outputs=False):
    t, k = x.shape
    tm = min(INPROJ_TM, t)
    w_block = (None,) * (w.ndim - 2) + ((tn, k) if w_rows_are_outputs else (k, tn))
    in_specs = [pl.BlockSpec((tm, k), lambda n, m: (m, 0)),
                pl.BlockSpec(w_block, lambda n, m: w_index(n))]
    args = [x, w]
    if bias is not None:
        b_block = (None,) * (bias.ndim - 2) + (1, tn)
        in_specs.append(pl.BlockSpec(b_block, lambda n, m: b_index(n)))
        args.append(bias)
    return pl.pallas_call(
        functools.partial(_mm_kernel, act=act, has_bias=bias is not None,
                          w_rows_are_outputs=w_rows_are_outputs),
        grid=(n_out // tn, t // tm),
        in_specs=in_specs,
        out_specs=pl.BlockSpec((tm, tn), lambda n, m: (m, n)),
        out_shape=jax.ShapeDtypeStruct((t, n_out), out_dtype),
        scratch_shapes=[pltpu.VMEM((k, tn), BF16)],
        compiler_params=_params(2, INPROJ_VMEM_LIMIT),
        name="inproj_" + act,
    )(*args)


def _tail_weights_kernel(wt_hbm, wgate_ref, wdt_ref, buf_ref, sem):
    n_tail = buf_ref.shape[0]
    copy = pltpu.make_async_copy(wt_hbm.at[pl.program_id(0), pl.ds(OFF_DT, n_tail)], buf_ref, sem)
    copy.start()
    copy.wait()
    lane = lax.broadcasted_iota(I32, (D_MODEL, LANES), 1)
    wdt_ref[...] = jnp.where(lane < SSM_HEADS, buf_ref[0:LANES, :].T, 0.0).astype(BF16)
    step = 4 * LANES
    for lo in range(0, 2 * D_MODEL, step):
        rows = slice(SSM_HEADS + lo, SSM_HEADS + lo + step)
        wgate_ref[:, lo:lo + step] = buf_ref[rows, :].T.astype(BF16)


def _tail_weights(w_in_t):
    depth, n_all, k = w_in_t.shape
    n = 2 * D_MODEL
    return pl.pallas_call(
        _tail_weights_kernel,
        grid=(depth,),
        in_specs=[pl.BlockSpec(memory_space=pl.ANY)],
        out_specs=[pl.BlockSpec((None, k, n), lambda l: (l, 0, 0)),
                   pl.BlockSpec((None, k, LANES), lambda l: (l, 0, 0))],
        out_shape=[jax.ShapeDtypeStruct((depth, k, n), BF16),
                   jax.ShapeDtypeStruct((depth, k, LANES), BF16)],
        scratch_shapes=[pltpu.VMEM((n_all - OFF_DT, k), F32), pltpu.SemaphoreType.DMA(())],
        compiler_params=_params(1),
        name="tail_weights",
    )(w_in_t)


def _glu_kernel(x_ref, wa_ref, wg_ref, ba_ref, bg_ref, o_ref, wa_bf, wg_bf):
    @pl.when(pl.program_id(1) == 0)
    def _():
        wa_bf[...] = wa_ref[...].T.astype(BF16)
        wg_bf[...] = wg_ref[...].T.astype(BF16)

    x = x_ref[...]
    a = jnp.dot(x, wa_bf[...], preferred_element_type=F32) + ba_ref[...]
    g = jnp.dot(x, wg_bf[...], preferred_element_type=F32) + bg_ref[...]
    o_ref[...] = (a * _sigmoid(g)).astype(o_ref.dtype)


def _glu(x, w_in_t, b_glu3, layer, tn=512):
    t, k = x.shape
    tm = min(INPROJ_TM, t)
    half = D_CONV // tn
    return pl.pallas_call(
        _glu_kernel,
        grid=(half, t // tm),
        in_specs=[pl.BlockSpec((tm, k), lambda n, m: (m, 0)),
                  pl.BlockSpec((None, tn, k), lambda n, m: (layer, n, 0)),
                  pl.BlockSpec((None, tn, k), lambda n, m: (layer, n + half, 0)),
                  pl.BlockSpec((None, 1, tn), lambda n, m: (layer, 0, n)),
                  pl.BlockSpec((None, 1, tn), lambda n, m: (layer, 0, n + half))],
        out_specs=pl.BlockSpec((tm, tn), lambda n, m: (m, n)),
        out_shape=jax.ShapeDtypeStruct((t, D_CONV), BF16),
        scratch_shapes=[pltpu.VMEM((k, tn), BF16), pltpu.VMEM((k, tn), BF16)],
        compiler_params=_params(2, INPROJ_VMEM_LIMIT),
        name="inproj_glu",
    )(x, w_in_t, w_in_t, b_glu3, b_glu3)


def _convbranch_kernel(c_ref, cw_ref, cb_ref, lg_ref, lb_ref, w_ref, gate_ref, u_ref, wz_ref,
                       o_ref, zs_ref, ext_ref, sh_ref, conv_ref, wbf_ref, wz_bf, *, tl):
    first = (pl.program_id(0) == 0) & (pl.program_id(1) == 0)

    @pl.when(first)
    def _():
        wbf_ref[...] = w_ref[...].astype(BF16)
        wz_bf[...] = wz_ref[...].T.astype(BF16)

    half = D_INNER // 2
    z = jnp.dot(u_ref[...], wz_bf[:, :half], preferred_element_type=F32)
    zs_ref[:, :half] = (z * _sigmoid(z)).astype(zs_ref.dtype)

    @pl.when(pl.program_id(1) == 0)
    def _():
        ext_ref[0:CONV_HALO, :] = jnp.zeros((CONV_HALO, D_CONV), F32)

    @pl.when(pl.program_id(1) > 0)
    def _():
        ext_ref[0:CONV_HALO, :] = ext_ref[tl:tl + CONV_HALO, :]

    ext_ref[CONV_HALO:CONV_HALO + tl, :] = c_ref[...].astype(F32)

    sh_rows = tl + CONV_HALO - 8
    for s in range(1, 8):
        sh_ref[s - 1] = ext_ref[s:s + sh_rows, :]

    base = CONV_HALO - (CONV_WIDTH - 1)

    n_groups = CONV_RC // 8
    taps_of_shift = [[(a, 8 * a + s - base) for a in range(5) if 0 <= 8 * a + s - base < CONV_WIDTH]
                     for s in range(8)]

    def conv_rows(rc, carry):
        r0 = pl.multiple_of(rc * CONV_RC, CONV_RC)
        for lo in range(0, D_CONV, LANES):
            cols = slice(lo, lo + LANES)
            w = [jnp.broadcast_to(cw_ref[k:k + 1, cols], (8, LANES)) for k in range(CONV_WIDTH)]
            acc = [jnp.broadcast_to(cb_ref[:, cols], (8, LANES))] * n_groups
            for s in range(8):
                taps = taps_of_shift[s]
                for j in range(n_groups + max(a for a, _ in taps)):
                    used = [(a, k) for a, k in taps if 0 <= j - a < n_groups]
                    if not used:
                        continue
                    rows = pl.ds(r0 + 8 * j, 8)
                    x = ext_ref[rows, cols] if s == 0 else sh_ref[s - 1, rows, cols]
                    for a, k in used:
                        acc[j - a] = acc[j - a] + w[k] * x
            for i in range(n_groups):
                conv_ref[pl.ds(r0 + 8 * i, 8), cols] = acc[i]
        return carry

    lax.fori_loop(0, tl // CONV_RC, conv_rows, 0)
    z = jnp.dot(u_ref[...], wz_bf[:, half:], preferred_element_type=F32)
    h = _layer_norm(conv_ref[...], lg_ref[...], lb_ref[...])
    h = h * _sigmoid(h)
    y = jnp.dot(h.astype(BF16), wbf_ref[...], preferred_element_type=F32)
    o_ref[...] = (y * gate_ref[...].astype(F32)).astype(o_ref.dtype)
    zs_ref[:, half:] = (z * _sigmoid(z)).astype(zs_ref.dtype)


def _convbranch(c, gates, u, w_in_t, conv_w, conv_b3, ln_g3, ln_b3, w_conv_out, layer, bsz, seqlen,
                tl=256):
    t, k = u.shape
    nl = seqlen // tl
    vec = pl.BlockSpec((None, 1, D_CONV), lambda b, i: (layer, 0, 0))
    tile = lambda width: pl.BlockSpec((tl, width), lambda b, i: (b * nl + i, 0))
    return pl.pallas_call(
        functools.partial(_convbranch_kernel, tl=tl),
        grid=(bsz, nl),
        in_specs=[tile(D_CONV),
                  pl.BlockSpec((None, CONV_WIDTH, D_CONV), lambda b, i: (layer, 0, 0)),
                  vec, vec, vec,
                  pl.BlockSpec((None, D_CONV, D_MODEL), lambda b, i: (layer, 0, 0)),
                  tile(D_MODEL), tile(k),
                  pl.BlockSpec((None, D_INNER, k), lambda b, i: (layer, OFF_Z // D_INNER, 0))],
        out_specs=[tile(D_MODEL), tile(D_INNER)],
        out_shape=[jax.ShapeDtypeStruct((t, D_MODEL), BF16),
                   jax.ShapeDtypeStruct((t, D_INNER), BF16)],
        scratch_shapes=[pltpu.VMEM((tl + CONV_HALO, D_CONV), F32),
                        pltpu.VMEM((7, tl + CONV_HALO - 8, D_CONV), F32),
                        pltpu.VMEM((tl, D_CONV), F32),
                        pltpu.VMEM((D_CONV, D_MODEL), BF16),
                        pltpu.VMEM((k, D_INNER), BF16)],
        compiler_params=_params(2, INPROJ_VMEM_LIMIT),
        name="conv_module",
    )(c, conv_w, conv_b3, ln_g3, ln_b3, w_conv_out, gates, u, w_in_t)


def _expand_heads(v, g, lane_in_pair):
    rows = v.shape[0]
    b = [jnp.broadcast_to(v[:, g * HEADS_PER_GROUP + r:g * HEADS_PER_GROUP + r + 1], (rows, LANES))
         for r in range(HEADS_PER_GROUP)]
    first = lane_in_pair[:rows] < SSM_HEAD_DIM
    return jnp.concatenate([jnp.where(first, b[0], b[1]), jnp.where(first, b[2], b[3])], axis=1)


def _ssd_kernel(xbc_ref, u_ref, zs_ref, wdt_ref, dtb_ref, cw_ref, cb_ref, alog_ref, dskip_ref, ng_ref,
                o_ref, ext_ref, sh_ref, act_ref, state_ref, *, q):
    @pl.when(pl.program_id(1) == 0)
    def _():
        ext_ref[0:SSM_HALO, :] = jnp.zeros((SSM_HALO, D_XBC), F32)
        state_ref[...] = jnp.zeros(state_ref.shape, F32)

    @pl.when(pl.program_id(1) > 0)
    def _():
        ext_ref[0:SSM_HALO, :] = ext_ref[q:q + SSM_HALO, :]

    ext_ref[SSM_HALO:SSM_HALO + q, :] = xbc_ref[...].astype(F32)

    base = SSM_HALO - (SSM_CONV_WIDTH - 1)
    for k in range(SSM_CONV_WIDTH - 1):
        sh_ref[k] = ext_ref[base + k:base + k + q, :]
    for lo in range(0, D_XBC, CONV_CW):
        for r0 in range(0, q, CONV_RC):
            cols = slice(lo, lo + CONV_CW)
            acc = cb_ref[:, cols] + cw_ref[SSM_CONV_WIDTH - 1:SSM_CONV_WIDTH, cols] * \
                ext_ref[SSM_HALO + r0:SSM_HALO + r0 + CONV_RC, cols]
            for k in range(SSM_CONV_WIDTH - 1):
                acc = acc + cw_ref[k:k + 1, cols] * sh_ref[k, r0:r0 + CONV_RC, cols]
            act_ref[r0:r0 + CONV_RC, cols] = acc * _sigmoid(acc)

    dt_raw = jnp.dot(u_ref[...], wdt_ref[...], preferred_element_type=F32) + dtb_ref[...]
    dt = jnp.maximum(dt_raw, 0.0) + jnp.log1p(jnp.exp(-jnp.abs(dt_raw)))
    for r0 in range(0, q, SSD_Q):
        _scan_chunk(slice(r0, r0 + SSD_Q), dt[r0:r0 + SSD_Q, :], act_ref, zs_ref, alog_ref, dskip_ref,
                    ng_ref, o_ref, state_ref)


def _scan_chunk(rows, dt, act_ref, zs_ref, alog_ref, dskip_ref, ng_ref, o_ref, state_ref):
    q = SSD_Q
    adt = dt * (-jnp.exp(alog_ref[...]))
    row = lax.broadcasted_iota(I32, (q, q), 0)
    col = lax.broadcasted_iota(I32, (q, q), 1)
    causal = row >= col
    tril = jnp.where(causal, 1.0, 0.0).astype(F32)
    acs = jnp.dot(tril, adt, preferred_element_type=F32, precision=lax.Precision.HIGHEST)
    acs_t = acs.T
    dt_t = dt.T
    last = acs[q - 1:q, :]
    exp_acs = jnp.exp(acs)
    dt_decay = dt * jnp.exp(last - acs)
    chunk_decay = jnp.exp(last)
    lane_in_pair = lax.broadcasted_iota(I32, (q, LANES), 1)
    head_of_lane = lax.broadcasted_iota(I32, (1, GROUP_CH), 1) // SSM_HEAD_DIM

    b_off = D_INNER
    c_off = D_INNER + SSM_GROUPS * D_STATE
    for g in range(SSM_GROUPS):
        ch = slice(g * GROUP_CH, (g + 1) * GROUP_CH)
        xg = act_ref[rows, ch]
        xg_bf = xg.astype(BF16)
        bg = act_ref[rows, b_off + g * D_STATE:b_off + (g + 1) * D_STATE]
        cg = act_ref[rows, c_off + g * D_STATE:c_off + (g + 1) * D_STATE].astype(BF16)
        cb = lax.dot_general(cg, bg.astype(BF16), (((1,), (1,)), ((), ())),
                             preferred_element_type=F32)
        ms, xblocks = [], []
        for r in range(HEADS_PER_GROUP):
            h = g * HEADS_PER_GROUP + r
            lmat = jnp.exp(jnp.where(causal, acs[:, h:h + 1] - acs_t[h:h + 1, :], -jnp.inf))
            ms.append((cb * lmat * dt_t[h:h + 1, :]).astype(BF16))
            head_mask = jnp.where(head_of_lane == r, 1.0, 0.0).astype(BF16)
            xblocks.append(xg_bf * head_mask)
        y_diag = jnp.dot(jnp.concatenate(ms, axis=1), jnp.concatenate(xblocks, axis=0),
                         preferred_element_type=F32)
        s_prev = state_ref[g]
        y_off = jnp.dot(cg, s_prev.astype(BF16), preferred_element_type=F32)
        yg = y_diag + y_off * _expand_heads(exp_acs, g, lane_in_pair) + xg * dskip_ref[:, ch]
        xw = (xg * _expand_heads(dt_decay, g, lane_in_pair)).astype(BF16)
        dec = _expand_heads(chunk_decay, g, lane_in_pair)
        state_ref[g] = s_prev * dec + jnp.dot(bg.T.astype(BF16), xw, preferred_element_type=F32)

        yz = yg * zs_ref[rows, g * GROUP_CH:(g + 1) * GROUP_CH].astype(F32)
        ms = jnp.mean(yz * yz, axis=-1, keepdims=True)
        yn = yz * lax.rsqrt(ms + RMS_EPS) * ng_ref[:, g * GROUP_CH:(g + 1) * GROUP_CH]
        o_ref[rows, g * GROUP_CH:(g + 1) * GROUP_CH] = yn.astype(o_ref.dtype)


def _ssd(xbc, u, zs, w_dt, dtb3, ssm_conv_w, ssm_conv_b3, alog3, dskip3, ng3, layer, bsz, seqlen):
    t = xbc.shape[0]
    q = min(SSD_STEP, seqlen)
    nq = seqlen // q
    tile = lambda width: pl.BlockSpec((q, width), lambda b, i: (b * nq + i, 0))
    vec = lambda width: pl.BlockSpec((None, 1, width), lambda b, i: (layer, 0, 0))
    return pl.pallas_call(
        functools.partial(_ssd_kernel, q=q),
        grid=(bsz, nq),
        in_specs=[tile(D_XBC), tile(D_MODEL), tile(D_INNER),
                  pl.BlockSpec((None, D_MODEL, LANES), lambda b, i: (layer, 0, 0)), vec(LANES),
                  pl.BlockSpec((None, SSM_CONV_WIDTH, D_XBC), lambda b, i: (layer, 0, 0)),
                  vec(D_XBC), vec(LANES), vec(D_INNER), vec(D_INNER)],
        out_specs=tile(D_INNER),
        out_shape=jax.ShapeDtypeStruct((t, D_INNER), BF16),
        scratch_shapes=[pltpu.VMEM((q + SSM_HALO, D_XBC), F32),
                        pltpu.VMEM((SSM_CONV_WIDTH - 1, q, D_XBC), F32),
                        pltpu.VMEM((q, D_XBC), F32),
                        pltpu.VMEM((SSM_GROUPS, D_STATE, GROUP_CH), F32)],
        compiler_params=_params(2),
        name="ssd_mixer",
    )(xbc, u, zs, w_dt, dtb3, ssm_conv_w, ssm_conv_b3, alog3, dskip3, ng3)


def _first_argmax(vals):
    best, idx = vals[0], jnp.zeros(vals[0].shape, I32)
    for j in range(1, len(vals)):
        gt = vals[j] > best
        idx = jnp.where(gt, j, idx)
        best = jnp.where(gt, vals[j], best)
    return idx, best


def _select(idx, vals):
    out = vals[len(vals) - 1]
    for j in range(len(vals) - 2, -1, -1):
        out = jnp.where(idx == j, vals[j], out)
    return out


def _outproj_kernel(yn_ref, y1_ref, gs_ref, x_ref, wso_ref, wo_ref, lg_ref, lb_ref, wr_ref, rb_ref,
                    x1_ref, x1b_ref, pos_ref, wts_ref, cnt_ref, seg_ref,
                    wso_bf, wo_bf, base_ref, *, tm):
    @pl.when(pl.program_id(0) == 0)
    def _():
        wso_bf[...] = wso_ref[...].astype(BF16)
        wo_bf[...] = wo_ref[...].astype(BF16)
        base_ref[...] = jnp.zeros(base_ref.shape, F32)

    y_ssm = jnp.dot(yn_ref[...], wso_bf[...], preferred_element_type=F32)
    merged = y1_ref[...].astype(F32) + gs_ref[...].astype(F32) * y_ssm
    mix = jnp.dot(merged.astype(BF16), wo_bf[...], preferred_element_type=F32)
    x1 = _layer_norm(ALPHA * x_ref[...] + mix, lg_ref[...], lb_ref[...])
    x1_ref[...] = x1
    x1b_ref[...] = x1.astype(BF16)

    logits = lax.dot_general(wr_ref[...], x1, (((1,), (1,)), ((), ())),
                             preferred_element_type=F32, precision=lax.Precision.HIGHEST)
    scores = _sigmoid(logits)
    sel = scores + rb_ref[...]
    sel_rows = [sel[e:e + 1, :] for e in range(N_EXPERTS)]
    sc_rows = [scores[e:e + 1, :] for e in range(N_EXPERTS)]

    gscores = []
    for gi in range(N_EXPERT_GROUPS):
        v = sel_rows[gi * EXPERTS_PER_GROUP:(gi + 1) * EXPERTS_PER_GROUP]
        best = None
        for a in range(EXPERTS_PER_GROUP):
            for b in range(a + 1, EXPERTS_PER_GROUP):
                s = v[a] + v[b]
                best = s if best is None else jnp.maximum(best, s)
        gscores.append(best)
    grp, _ = _first_argmax(gscores)

    sel_in = [_select(grp, [sel_rows[gi * EXPERTS_PER_GROUP + j] for gi in range(N_EXPERT_GROUPS)])
              for j in range(EXPERTS_PER_GROUP)]
    sc_in = [_select(grp, [sc_rows[gi * EXPERTS_PER_GROUP + j] for gi in range(N_EXPERT_GROUPS)])
             for j in range(EXPERTS_PER_GROUP)]
    i1, _ = _first_argmax(sel_in)
    neg = jnp.full(sel_in[0].shape, -jnp.inf, F32)
    i2, _ = _first_argmax([jnp.where(i1 == j, neg, sel_in[j]) for j in range(EXPERTS_PER_GROUP)])
    s1 = _select(i1, sc_in)
    s2 = _select(i2, sc_in)
    tot = s1 + s2
    e1 = grp * EXPERTS_PER_GROUP + i1
    e2 = grp * EXPERTS_PER_GROUP + i2
    wts_ref[0:1, :] = s1 / tot
    wts_ref[1:2, :] = s2 / tot

    eio = lax.broadcasted_iota(I32, (N_EXPERTS, tm), 0)
    oh1 = jnp.where(eio == e1, 1.0, 0.0).astype(F32)
    oh2 = jnp.where(eio == e2, 1.0, 0.0).astype(F32)
    both = oh1 + oh2
    srow = lax.broadcasted_iota(I32, (tm, tm), 0)
    scol = lax.broadcasted_iota(I32, (tm, tm), 1)
    before = jnp.where(srow < scol, 1.0, 0.0).astype(BF16)
    cum = jnp.dot(both.astype(BF16), before, preferred_element_type=F32)
    cnt8 = jnp.floor((jnp.sum(both, axis=1, keepdims=True) + 7.0) * 0.125) * 8.0
    cnt8_l = jnp.broadcast_to(cnt8, (N_EXPERTS, LANES))
    erow = lax.broadcasted_iota(I32, (N_EXPERTS, N_EXPERTS), 0)
    ecol = lax.broadcasted_iota(I32, (N_EXPERTS, N_EXPERTS), 1)
    seg_off = jnp.dot(jnp.where(ecol < erow, 1.0, 0.0).astype(F32), cnt8_l,
                      preferred_element_type=F32, precision=lax.Precision.HIGHEST)
    base = base_ref[...]
    in_tile = seg_off[:, 0:1] + cum
    pos_ref[0:1, :] = jnp.sum(oh1 * in_tile, axis=0, keepdims=True).astype(I32)
    pos_ref[1:2, :] = jnp.sum(oh2 * in_tile, axis=0, keepdims=True).astype(I32)
    lane = lax.broadcasted_iota(I32, (N_EXPERTS, LANES), 1)
    seg_ref[...] = jnp.where(lane == 0, seg_off, jnp.where(lane == 1, cnt8_l, base)).astype(I32)
    base_ref[...] = base + cnt8_l
    cnt_ref[...] = base_ref[...].astype(I32)


def _outproj(yn, y1g, gates, x, w_ssm_out, w_out, ln_g3, ln_b3, wr_t, rbias, layer, tm):
    t = x.shape[0]
    tile = lambda width: pl.BlockSpec((tm, width), lambda m: (m, 0))
    vec = pl.BlockSpec((None, 1, D_MODEL), lambda m: (layer, 0, 0))
    pair = pl.BlockSpec((2, tm), lambda m: (0, m))
    return pl.pallas_call(
        functools.partial(_outproj_kernel, tm=tm),
        grid=(t // tm,),
        in_specs=[tile(D_INNER), tile(D_MODEL),
                  pl.BlockSpec((tm, D_MODEL), lambda m: (m, 1)),
                  tile(D_MODEL),
                  pl.BlockSpec((None, D_INNER, D_MODEL), lambda m: (layer, 0, 0)),
                  pl.BlockSpec((None, D_MODEL, D_MODEL), lambda m: (layer, 0, 0)),
                  vec, vec,
                  pl.BlockSpec((N_EXPERTS, D_MODEL), lambda m: (0, 0)),
                  pl.BlockSpec((N_EXPERTS, 1), lambda m: (0, 0))],
        out_specs=[tile(D_MODEL), tile(D_MODEL), pair, pair,
                   pl.BlockSpec((N_EXPERTS, LANES), lambda m: (0, 0)),
                   pl.BlockSpec((None, N_EXPERTS, LANES), lambda m: (m, 0, 0))],
        out_shape=[jax.ShapeDtypeStruct((t, D_MODEL), F32),
                   jax.ShapeDtypeStruct((t, D_MODEL), BF16),
                   jax.ShapeDtypeStruct((2, t), I32),
                   jax.ShapeDtypeStruct((2, t), F32),
                   jax.ShapeDtypeStruct((N_EXPERTS, LANES), I32),
                   jax.ShapeDtypeStruct((t // tm, N_EXPERTS, LANES), I32)],
        scratch_shapes=[pltpu.VMEM((D_INNER, D_MODEL), BF16),
                        pltpu.VMEM((D_MODEL, D_MODEL), BF16),
                        pltpu.VMEM((N_EXPERTS, LANES), F32)],
        compiler_params=_params(1),
        name="outproj_ln_router",
    )(yn, y1g, gates, x, w_ssm_out, w_out, ln_g3, ln_b3, wr_t, rbias)


def _segment_blocks(tile, segoff_ref, cnt8_ref, gdst_ref, make_copy, act):
    for e in range(N_EXPERTS):
        idx = tile * N_EXPERTS + e
        n8 = lax.shift_right_logical(cnt8_ref[idx], 3)
        local = segoff_ref[idx]
        glob = gdst_ref[idx]
        for bit in SEGMENT_BITS:
            rows = 8 * bit
            hit = (n8 & bit) != 0

            @pl.when(hit)
            def _(local=local, glob=glob, rows=rows):
                act(make_copy(pl.multiple_of(local, 8), pl.multiple_of(glob, 8), rows))

            step = jnp.where(hit, rows, 0)
            local = local + step
            glob = glob + step


def _dispatch_kernel(fill_ref, segoff_ref, cnt8_ref, gdst_ref, pos_ref, x_ref, xs_hbm,
                     sorted_ref, zeros_ref, sem, blk_sem, seg_sems, *, tm, n_blk, n_tiles):
    m = pl.program_id(0)
    slot = lax.rem(m, 2)

    def zero_rows(row):
        return pltpu.make_async_copy(zeros_ref.at[pl.ds(0, 8)],
                                     xs_hbm.at[pl.ds(pl.multiple_of(row, 8), 8)], sem)

    def zero_block(b):
        return pltpu.make_async_copy(
            zeros_ref, xs_hbm.at[pl.ds(pl.multiple_of(b * MOE_ROWS, MOE_ROWS), MOE_ROWS)], blk_sem)

    @pl.when(m == 0)
    def _():
        zeros_ref[...] = jnp.zeros(zeros_ref.shape, F32)
        n_active = fill_ref[2 * N_EXPERTS]
        for e in range(N_EXPERTS):
            pad_start = fill_ref[e]
            lax.fori_loop(0, fill_ref[N_EXPERTS + e],
                          lambda j, c: (zero_rows(pad_start + 8 * j).start(), c)[1], 0)
        lax.fori_loop(n_active, n_blk, lambda b, c: (zero_block(b).start(), c)[1], 0)
        lax.fori_loop(0, fill_ref[2 * N_EXPERTS + 1], lambda j, c: (zero_rows(0).wait(), c)[1], 0)
        lax.fori_loop(n_active, n_blk, lambda b, c: (zero_block(b).wait(), c)[1], 0)

    def for_each_segment_block(tile, s, act):
        copy = lambda local, glob, rows: pltpu.make_async_copy(
            sorted_ref.at[s, pl.ds(local, rows)], xs_hbm.at[pl.ds(glob, rows)], seg_sems.at[s])
        _segment_blocks(tile, segoff_ref, cnt8_ref, gdst_ref, copy, act)

    start = lambda copy: copy.start()
    wait = lambda copy: copy.wait()

    @pl.when(m >= 2)
    def _():
        for_each_segment_block(m - 2, slot, wait)

    pos = pos_ref[...]
    srow = lax.broadcasted_iota(I32, (sorted_ref.shape[1], tm), 0)
    onehot = jnp.where(srow == pos[0:1, :], 1.0, jnp.where(srow == pos[1:2, :], 1.0, 0.0)).astype(BF16)
    sorted_ref[slot] = jnp.dot(onehot, x_ref[...], preferred_element_type=F32)
    for_each_segment_block(m, slot, start)

    @pl.when(m == n_tiles - 1)
    def _():
        if n_tiles > 1:
            for_each_segment_block(m - 1, 1 - slot, wait)
        for_each_segment_block(m, slot, wait)


def _dispatch(fill, seg_off, seg_cnt8, seg_dst, pos, x1b, n_blk, tm):
    t = x1b.shape[0]
    n_tiles = t // tm
    sorted_rows = 2 * tm + 8 * N_EXPERTS
    grid_spec = pltpu.PrefetchScalarGridSpec(
        num_scalar_prefetch=4,
        grid=(n_tiles,),
        in_specs=[pl.BlockSpec((2, tm), lambda m, *_: (0, m)),
                  pl.BlockSpec((tm, D_MODEL), lambda m, *_: (m, 0))],
        out_specs=pl.BlockSpec(memory_space=pl.ANY),
        scratch_shapes=[pltpu.VMEM((2, sorted_rows, D_MODEL), F32),
                        pltpu.VMEM((MOE_ROWS, D_MODEL), F32),
                        pltpu.SemaphoreType.DMA(()), pltpu.SemaphoreType.DMA(()),
                        pltpu.SemaphoreType.DMA((2,))],
    )
    return pl.pallas_call(
        functools.partial(_dispatch_kernel, tm=tm, n_blk=n_blk, n_tiles=n_tiles),
        grid_spec=grid_spec,
        out_shape=jax.ShapeDtypeStruct((n_blk * MOE_ROWS, D_MODEL), F32),
        compiler_params=_params(1),
        name="moe_dispatch",
    )(fill, seg_off, seg_cnt8, seg_dst, pos, x1b)


def _expert_kernel(src_ref, exp_ref, nvalid_ref, xs_ref, wg_ref, wu_ref, wd_ref, y_ref,
                   wgu_bf, wd_bf):
    i = pl.program_id(0)
    nvalid = nvalid_ref[i]
    changed = (i == 0) | (exp_ref[i] != exp_ref[jnp.maximum(i - 1, 0)])

    @pl.when((nvalid > 0) & changed)
    def _():
        wgu_bf[:, :D_EXPERT] = wg_ref[...].astype(BF16)
        wgu_bf[:, D_EXPERT:] = wu_ref[...].astype(BF16)
        wd_bf[...] = wd_ref[...].astype(BF16)

    @pl.when(nvalid > 0)
    def _():
        rows = lax.broadcasted_iota(I32, (MOE_ROWS, 1), 0)
        x = jnp.where(rows < nvalid, xs_ref[...], 0.0).astype(BF16)
        gu = jnp.dot(x, wgu_bf[...], preferred_element_type=F32)
        hg = gu[:, :D_EXPERT]
        h = hg * _sigmoid(hg) * gu[:, D_EXPERT:]
        y_ref[...] = jnp.dot(h.astype(BF16), wd_bf[...], preferred_element_type=F32)

    @pl.when(nvalid == 0)
    def _():
        y_ref[...] = jnp.zeros(y_ref.shape, F32)


def _experts(blk_src, blk_exp, blk_nvalid, xs, wg, wu, wd, layer):
    n_rows = xs.shape[0]
    n_blk = n_rows // MOE_ROWS
    grid_spec = pltpu.PrefetchScalarGridSpec(
        num_scalar_prefetch=3,
        grid=(n_blk,),
        in_specs=[pl.BlockSpec((MOE_ROWS, D_MODEL), lambda i, src, exp, nv: (src[i], 0)),
                  pl.BlockSpec((None, None, D_MODEL, D_EXPERT),
                               lambda i, src, exp, nv: (layer, exp[i], 0, 0)),
                  pl.BlockSpec((None, None, D_MODEL, D_EXPERT),
                               lambda i, src, exp, nv: (layer, exp[i], 0, 0)),
                  pl.BlockSpec((None, None, D_EXPERT, D_MODEL),
                               lambda i, src, exp, nv: (layer, exp[i], 0, 0))],
        out_specs=pl.BlockSpec((MOE_ROWS, D_MODEL), lambda i, src, exp, nv: (i, 0)),
        scratch_shapes=[pltpu.VMEM((D_MODEL, 2 * D_EXPERT), BF16),
                        pltpu.VMEM((D_EXPERT, D_MODEL), BF16)],
    )
    return pl.pallas_call(
        _expert_kernel,
        grid_spec=grid_spec,
        out_shape=jax.ShapeDtypeStruct((n_rows, D_MODEL), F32),
        compiler_params=_params(1),
        name="moe_experts",
    )(blk_src, blk_exp, blk_nvalid, xs, wg, wu, wd)


def _moe_plan(counts, seg, t):
    cnt = counts[:, 0]
    nblk_e = (cnt + MOE_ROWS - 1) // MOE_ROWS
    blk_end = jnp.cumsum(nblk_e)
    blk_start = blk_end - nblk_e
    n_active = blk_end[N_EXPERTS - 1]
    experts = jnp.arange(N_EXPERTS, dtype=I32)
    row_start = blk_start * MOE_ROWS
    n_tiles = seg.shape[0]
    n_blk = -(-(2 * t + 8 * N_EXPERTS * n_tiles) // MOE_ROWS) + N_EXPERTS
    ids = jnp.arange(n_blk, dtype=I32)
    src = jnp.minimum(ids, n_active - 1)
    exp = jnp.minimum(jnp.sum((blk_end[None, :] <= src[:, None]).astype(I32), axis=1), N_EXPERTS - 1)
    onehot = (exp[:, None] == experts[None, :]).astype(I32)
    pick = lambda table: jnp.sum(onehot * table[None, :], axis=1)
    left = pick(cnt) - (src - pick(blk_start)) * MOE_ROWS
    nvalid = jnp.where(ids < n_active, jnp.clip(left, 0, MOE_ROWS), 0)
    pad8 = (nblk_e * MOE_ROWS - cnt) // 8
    fill = jnp.concatenate([row_start + cnt, pad8, jnp.stack([n_active, jnp.sum(pad8)])]).astype(I32)
    seg_off = seg[:, :, 0].reshape(-1)
    seg_cnt8 = seg[:, :, 1].reshape(-1)
    seg_dst = (row_start[None, :] + seg[:, :, 2]).reshape(-1)
    return src.astype(I32), exp, nvalid.astype(I32), fill, seg_off, seg_cnt8, seg_dst, n_blk


def _final_kernel(segoff_ref, cnt8_ref, gsrc_ref, x1_ref, x1b_ref, p_ref, wt_ref, pos_ref, wpu_ref,
                  wpg_ref, bpg_ref, lg_ref, lb_ref, y_hbm, x2_ref, x2b_ref, ys_ref, wpu_bf, wpg_bf, sems,
                  *, tm, n_tiles):
    m = pl.program_id(0)
    slot = lax.rem(m, 2)

    def for_each_segment_block(tile, s, act):
        copy = lambda local, glob, rows: pltpu.make_async_copy(
            y_hbm.at[pl.ds(glob, rows)], ys_ref.at[s, pl.ds(local, rows)], sems.at[s])
        _segment_blocks(tile, segoff_ref, cnt8_ref, gsrc_ref, copy, act)

    @pl.when(m == 0)
    def _():
        wpu_bf[...] = wpu_ref[...].astype(BF16)
        wpg_bf[...] = wpg_ref[...].astype(BF16)
        for_each_segment_block(0, 0, lambda copy: copy.start())

    @pl.when(m + 1 < n_tiles)
    def _():
        for_each_segment_block(m + 1, 1 - slot, lambda copy: copy.start())

    up = jnp.dot(p_ref[...].astype(BF16), wpu_bf[...], preferred_element_type=F32)
    gate = _sigmoid(jnp.dot(x1b_ref[...], wpg_bf[...], preferred_element_type=F32) + bpg_ref[...])
    resid = ALPHA * x1_ref[...] + up * gate

    for_each_segment_block(m, slot, lambda copy: copy.wait())

    n_sorted = ys_ref.shape[1]
    used = segoff_ref[m * N_EXPERTS + N_EXPERTS - 1] + cnt8_ref[m * N_EXPERTS + N_EXPERTS - 1]
    srow = lax.broadcasted_iota(I32, (n_sorted, 1), 0)
    ys = jnp.where(srow < used, ys_ref[slot], 0.0).astype(BF16)
    pos = pos_ref[...]
    scol = lax.broadcasted_iota(I32, (tm, n_sorted), 1)
    wt = wt_ref[...]
    moe = jnp.zeros((tm, D_MODEL), F32)
    for k in range(TOP_K):
        pick = jnp.where(scol == pos[:, k:k + 1], 1.0, 0.0).astype(BF16)
        moe = moe + wt[:, k:k + 1] * jnp.dot(pick, ys, preferred_element_type=F32)
    x2 = _layer_norm(resid + moe, lg_ref[...], lb_ref[...])
    x2_ref[...] = x2
    x2b_ref[...] = x2.astype(BF16)


def _final(seg_off, seg_cnt8, seg_src, x1, x1b, p, wt_tok, pos_tok, w_ple_up, w_ple_gate, b_pg3,
           ln_g3, ln_b3, y_rows, layer, tm):
    t = x1.shape[0]
    n_tiles = t // tm
    sorted_rows = 2 * tm + 8 * N_EXPERTS
    tile = lambda width: pl.BlockSpec((tm, width), lambda m, *_: (m, 0))
    vec = pl.BlockSpec((None, 1, D_MODEL), lambda m, *_: (layer, 0, 0))
    grid_spec = pltpu.PrefetchScalarGridSpec(
        num_scalar_prefetch=3,
        grid=(n_tiles,),
        in_specs=[tile(D_MODEL), tile(D_MODEL),
                  pl.BlockSpec((None, tm, PLE_DIM), lambda m, *_: (layer, m, 0)),
                  tile(2), tile(2),
                  pl.BlockSpec((None, PLE_DIM, D_MODEL), lambda m, *_: (layer, 0, 0)),
                  pl.BlockSpec((None, D_MODEL, D_MODEL), lambda m, *_: (layer, 0, 0)),
                  vec, vec, vec,
                  pl.BlockSpec(memory_space=pl.ANY)],
        out_specs=[tile(D_MODEL), tile(D_MODEL)],
        scratch_shapes=[pltpu.VMEM((2, sorted_rows, D_MODEL), F32),
                        pltpu.VMEM((PLE_DIM, D_MODEL), BF16), pltpu.VMEM((D_MODEL, D_MODEL), BF16),
                        pltpu.SemaphoreType.DMA((2,))],
    )
    return pl.pallas_call(
        functools.partial(_final_kernel, tm=tm, n_tiles=n_tiles),
        grid_spec=grid_spec,
        out_shape=[jax.ShapeDtypeStruct((t, D_MODEL), F32),
                   jax.ShapeDtypeStruct((t, D_MODEL), BF16)],
        compiler_params=_params(1),
        name="combine_ple_ln",
    )(seg_off, seg_cnt8, seg_src, x1, x1b, p, wt_tok, pos_tok, w_ple_up, w_ple_gate, b_pg3,
      ln_g3, ln_b3, y_rows)


def kernel(x, p, w_in, b_glu, b_branch_gate, conv_w, conv_b, conv_ln_g, conv_ln_b, w_conv_out,
           ssm_conv_w, ssm_conv_b, dt_bias, a_log, d_skip, ssm_norm_g, w_ssm_out, w_out,
           ln1_g, ln1_b, w_router, router_bias, w_exp_gate, w_exp_up, w_exp_down,
           w_ple_up, w_ple_gate, b_ple_gate, ln2_g, ln2_b):
    bsz, seqlen, d = x.shape
    depth = w_in.shape[0]
    t = bsz * seqlen
    tm_moe = min(MOE_TILE, t)

    row3 = lambda a: a.reshape(a.shape[0], 1, a.shape[1])
    pad_lanes = lambda a: jnp.pad(a, ((0, 0), (0, LANES - a.shape[1])))
    b_glu3, b_gate3 = row3(b_glu), row3(b_branch_gate)
    conv_b3, cln_g3, cln_b3 = row3(conv_b), row3(conv_ln_g), row3(conv_ln_b)
    ssm_conv_b3, ng3 = row3(ssm_conv_b), row3(ssm_norm_g)
    dtb3, alog3 = row3(pad_lanes(dt_bias)), row3(pad_lanes(a_log))
    dskip3 = row3(jnp.repeat(d_skip, SSM_HEAD_DIM, axis=1))
    ln1_g3, ln1_b3, ln2_g3, ln2_b3 = row3(ln1_g), row3(ln1_b), row3(ln2_g), row3(ln2_b)
    b_pg3 = row3(b_ple_gate)
    w_in_t = jnp.swapaxes(w_in, 1, 2)
    w_gate, w_dt = _tail_weights(w_in_t)
    wr_t = w_router.T
    rbias = router_bias.reshape(N_EXPERTS, 1)
    p2 = p.reshape(depth, t, PLE_DIM)

    xf = x.reshape(t, d)
    xb = xf.astype(BF16)
    for i in range(depth):
        c = _glu(xb, w_in_t, b_glu3, i)
        xbc = _matmul(xb, w_in_t, lambda n: (i, OFF_XBC // 1024 + n, 0), None, None,
                      D_XBC, 1024, "none", BF16, w_rows_are_outputs=True)
        gates = _matmul(xb, w_gate, lambda n: (i, 0, n), b_gate3, lambda n: (i, 0, n),
                        2 * D_MODEL, 1024, "sigmoid", BF16)
        y1g, zs = _convbranch(c, gates, xb, w_in_t, conv_w, conv_b3, cln_g3, cln_b3, w_conv_out,
                              i, bsz, seqlen)
        yn = _ssd(xbc, xb, zs, w_dt, dtb3, ssm_conv_w, ssm_conv_b3, alog3, dskip3, ng3, i, bsz, seqlen)
        x1, x1b, pos, wts, counts, seg = _outproj(
            yn, y1g, gates, xf, w_ssm_out, w_out, ln1_g3, ln1_b3, wr_t, rbias, i, tm_moe)
        (blk_src, blk_exp, blk_nvalid, fill, seg_off, seg_cnt8, seg_row,
         n_blk) = _moe_plan(counts, seg, t)
        xs = _dispatch(fill, seg_off, seg_cnt8, seg_row, pos, x1b, n_blk, tm_moe)
        y_rows = _experts(blk_src, blk_exp, blk_nvalid, xs, w_exp_gate, w_exp_up, w_exp_down, i)
        xf, xb = _final(seg_off, seg_cnt8, seg_row, x1, x1b, p2, wts.T, pos.T,
                        w_ple_up, w_ple_gate, b_pg3, ln2_g3, ln2_b3, y_rows, i, tm_moe)
    return xf.reshape(bsz, seqlen, d)
```

```python
import functools

import jax
import jax.numpy as jnp
from jax import lax
from jax.experimental import pallas as pl
from jax.experimental.pallas import tpu as pltpu

F32 = jnp.float32
BF16 = jnp.bfloat16
I32 = jnp.int32

D_MODEL = 1024
D_CONV = 1024
CONV_WIDTH = 31
D_INNER = 2048
SSM_HEAD_DIM = 64
SSM_HEADS = 32
SSM_GROUPS = 8
HEADS_PER_GROUP = 4
D_STATE = 128
SSM_CONV_WIDTH = 4
D_XBC = D_INNER + 2 * SSM_GROUPS * D_STATE
GROUP_CH = HEADS_PER_GROUP * SSM_HEAD_DIM
N_EXPERTS = 16
N_EXPERT_GROUPS = 4
TOP_K = 2
EXPERTS_PER_GROUP = 4
D_EXPERT = 512
PLE_DIM = 256
DEPTH = 4
ALPHA = (2.0 * DEPTH) ** 0.25
LN_EPS = 1e-5
RMS_EPS = 1e-5

LANES = 128
CONV_HALO = 32
CONV_TILE = 512
CONV_RC = 64
CONV_CW = 256
SSM_HALO = 8
SSD_Q = 128
SSD_STEP = 256
MOE_ROWS = 512
MOE_TILE = 512
SEGMENT_BITS = (64, 32, 16, 8, 4, 2, 1)
VMEM_LIMIT = 48 * 1024 * 1024
INPROJ_TM = 2048
INPROJ_VMEM_LIMIT = 56 * 1024 * 1024

OFF_Z = 2 * D_CONV
OFF_XBC = OFF_Z + D_INNER
OFF_DT = OFF_XBC + D_XBC


def _sigmoid(x):
    return 1.0 / (1.0 + jnp.exp(-x))


def _layer_norm(x, g, b):
    mu = jnp.mean(x, axis=-1, keepdims=True)
    xc = x - mu
    var = jnp.mean(xc * xc, axis=-1, keepdims=True)
    return xc * lax.rsqrt(var + LN_EPS) * g + b


def _params(n_axes, vmem_limit=VMEM_LIMIT):
    return pltpu.CompilerParams(dimension_semantics=("arbitrary",) * n_axes,
                                vmem_limit_bytes=vmem_limit)


def _mm_kernel(x_ref, w_ref, *rest, act, has_bias, w_rows_are_outputs):
    if has_bias:
        b_ref, o_ref, wbf_ref = rest
    else:
        o_ref, wbf_ref = rest

    @pl.when(pl.program_id(1) == 0)
    def _():
        w = w_ref[...]
        wbf_ref[...] = (w.T if w_rows_are_outputs else w).astype(BF16)

    acc = jnp.dot(x_ref[...], wbf_ref[...], preferred_element_type=F32)
    if has_bias:
        acc = acc + b_ref[...]
    if act == "sigmoid":
        acc = _sigmoid(acc)
    elif act == "silu":
        acc = acc * _sigmoid(acc)
    o_ref[...] = acc.astype(o_ref.dtype)


def _matmul(x, w, w_index, bias, b_index, n_out, tn, act, out_dtype, w_rows_are_outputs=False):
    t, k = x.shape
    tm = min(INPROJ_TM, t)
    w_block = (None,) * (w.ndim - 2) + ((tn, k) if w_rows_are_outputs else (k, tn))
    in_specs = [pl.BlockSpec((tm, k), lambda n, m: (m, 0)),
                pl.BlockSpec(w_block, lambda n, m: w_index(n))]
    args = [x, w]
    if bias is not None:
        b_block = (None,) * (bias.ndim - 2) + (1, tn)
        in_specs.append(pl.BlockSpec(b_block, lambda n, m: b_index(n)))
        args.append(bias)
    return pl.pallas_call(
        functools.partial(_mm_kernel, act=act, has_bias=bias is not None,
                          w_rows_are_outputs=w_rows_are_outputs),
        grid=(n_out // tn, t // tm),
        in_specs=in_specs,
        out_specs=pl.BlockSpec((tm, tn), lambda n, m: (m, n)),
        out_shape=jax.ShapeDtypeStruct((t, n_out), out_dtype),
        scratch_shapes=[pltpu.VMEM((k, tn), BF16)],
        compiler_params=_params(2, INPROJ_VMEM_LIMIT),
        name="inproj_" + act,
    )(*args)


def _tail_weights_kernel(wt_hbm, wgate_ref, wdt_ref, buf_ref, sem):
    n_tail = buf_ref.shape[0]
    copy = pltpu.make_async_copy(wt_hbm.at[pl.program_id(0), pl.ds(OFF_DT, n_tail)], buf_ref, sem)
    copy.start()
    copy.wait()
    lane = lax.broadcasted_iota(I32, (D_MODEL, LANES), 1)
    wdt_ref[...] = jnp.where(lane < SSM_HEADS, buf_ref[0:LANES, :].T, 0.0).astype(BF16)
    step = 4 * LANES
    for lo in range(0, 2 * D_MODEL, step):
        rows = slice(SSM_HEADS + lo, SSM_HEADS + lo + step)
        wgate_ref[:, lo:lo + step] = buf_ref[rows, :].T.astype(BF16)


def _tail_weights(w_in_t):
    depth, n_all, k = w_in_t.shape
    n = 2 * D_MODEL
    return pl.pallas_call(
        _tail_weights_kernel,
        grid=(depth,),
        in_specs=[pl.BlockSpec(memory_space=pl.ANY)],
        out_specs=[pl.BlockSpec((None, k, n), lambda l: (l, 0, 0)),
                   pl.BlockSpec((None, k, LANES), lambda l: (l, 0, 0))],
        out_shape=[jax.ShapeDtypeStruct((depth, k, n), BF16),
                   jax.ShapeDtypeStruct((depth, k, LANES), BF16)],
        scratch_shapes=[pltpu.VMEM((n_all - OFF_DT, k), F32), pltpu.SemaphoreType.DMA(())],
        compiler_params=_params(1),
        name="tail_weights",
    )(w_in_t)


def _glu_kernel(x_ref, wa_ref, wg_ref, ba_ref, bg_ref, o_ref, wa_bf, wg_bf):
    @pl.when(pl.program_id(1) == 0)
    def _():
        wa_bf[...] = wa_ref[...].T.astype(BF16)
        wg_bf[...] = wg_ref[...].T.astype(BF16)

    x = x_ref[...]
    a = jnp.dot(x, wa_bf[...], preferred_element_type=F32) + ba_ref[...]
    g = jnp.dot(x, wg_bf[...], preferred_element_type=F32) + bg_ref[...]
    o_ref[...] = (a * _sigmoid(g)).astype(o_ref.dtype)


def _glu(x, w_in_t, b_glu3, layer, tn=512):
    t, k = x.shape
    tm = min(INPROJ_TM, t)
    half = D_CONV // tn
    return pl.pallas_call(
        _glu_kernel,
        grid=(half, t // tm),
        in_specs=[pl.BlockSpec((tm, k), lambda n, m: (m, 0)),
                  pl.BlockSpec((None, tn, k), lambda n, m: (layer, n, 0)),
                  pl.BlockSpec((None, tn, k), lambda n, m: (layer, n + half, 0)),
                  pl.BlockSpec((None, 1, tn), lambda n, m: (layer, 0, n)),
                  pl.BlockSpec((None, 1, tn), lambda n, m: (layer, 0, n + half))],
        out_specs=pl.BlockSpec((tm, tn), lambda n, m: (m, n)),
        out_shape=jax.ShapeDtypeStruct((t, D_CONV), BF16),
        scratch_shapes=[pltpu.VMEM((k, tn), BF16), pltpu.VMEM((k, tn), BF16)],
        compiler_params=_params(2, INPROJ_VMEM_LIMIT),
        name="inproj_glu",
    )(x, w_in_t, w_in_t, b_glu3, b_glu3)


def _convbranch_kernel(c_ref, cw_ref, cb_ref, lg_ref, lb_ref, w_ref, gate_ref, o_ref,
                       ext_ref, sh_ref, conv_ref, wbf_ref, *, tl):
    first = (pl.program_id(0) == 0) & (pl.program_id(1) == 0)

    @pl.when(first)
    def _():
        wbf_ref[...] = w_ref[...].astype(BF16)

    @pl.when(pl.program_id(1) == 0)
    def _():
        ext_ref[0:CONV_HALO, :] = jnp.zeros((CONV_HALO, D_CONV), F32)

    @pl.when(pl.program_id(1) > 0)
    def _():
        ext_ref[0:CONV_HALO, :] = ext_ref[tl:tl + CONV_HALO, :]

    ext_ref[CONV_HALO:CONV_HALO + tl, :] = c_ref[...].astype(F32)

    sh_rows = tl + CONV_HALO - 8
    for s in range(1, 8):
        sh_ref[s - 1] = ext_ref[s:s + sh_rows, :]

    base = CONV_HALO - (CONV_WIDTH - 1)

    n_groups = CONV_RC // 8
    taps_of_shift = [[(a, 8 * a + s - base) for a in range(5) if 0 <= 8 * a + s - base < CONV_WIDTH]
                     for s in range(8)]

    def conv_rows(rc, carry):
        r0 = pl.multiple_of(rc * CONV_RC, CONV_RC)
        for lo in range(0, D_CONV, LANES):
            cols = slice(lo, lo + LANES)
            w = [jnp.broadcast_to(cw_ref[k:k + 1, cols], (8, LANES)) for k in range(CONV_WIDTH)]
            acc = [jnp.broadcast_to(cb_ref[:, cols], (8, LANES))] * n_groups
            for s in range(8):
                taps = taps_of_shift[s]
                for j in range(n_groups + max(a for a, _ in taps)):
                    used = [(a, k) for a, k in taps if 0 <= j - a < n_groups]
                    if not used:
                        continue
                    rows = pl.ds(r0 + 8 * j, 8)
                    x = ext_ref[rows, cols] if s == 0 else sh_ref[s - 1, rows, cols]
                    for a, k in used:
                        acc[j - a] = acc[j - a] + w[k] * x
            for i in range(n_groups):
                conv_ref[pl.ds(r0 + 8 * i, 8), cols] = acc[i]
        return carry

    lax.fori_loop(0, tl // CONV_RC, conv_rows, 0)
    h = _layer_norm(conv_ref[...], lg_ref[...], lb_ref[...])
    h = h * _sigmoid(h)
    y = jnp.dot(h.astype(BF16), wbf_ref[...], preferred_element_type=F32)
    o_ref[...] = (y * gate_ref[...].astype(F32)).astype(o_ref.dtype)


def _convbranch(c, gates, conv_w, conv_b3, ln_g3, ln_b3, w_conv_out, layer, bsz, seqlen):
    t = c.shape[0]
    tl = min(CONV_TILE, seqlen)
    nl = seqlen // tl
    vec = pl.BlockSpec((None, 1, D_CONV), lambda b, i: (layer, 0, 0))
    return pl.pallas_call(
        functools.partial(_convbranch_kernel, tl=tl),
        grid=(bsz, nl),
        in_specs=[pl.BlockSpec((tl, D_CONV), lambda b, i: (b * nl + i, 0)),
                  pl.BlockSpec((None, CONV_WIDTH, D_CONV), lambda b, i: (layer, 0, 0)),
                  vec, vec, vec,
                  pl.BlockSpec((None, D_CONV, D_MODEL), lambda b, i: (layer, 0, 0)),
                  pl.BlockSpec((tl, D_MODEL), lambda b, i: (b * nl + i, 0))],
        out_specs=pl.BlockSpec((tl, D_MODEL), lambda b, i: (b * nl + i, 0)),
        out_shape=jax.ShapeDtypeStruct((t, D_MODEL), BF16),
        scratch_shapes=[pltpu.VMEM((tl + CONV_HALO, D_CONV), F32),
                        pltpu.VMEM((7, tl + CONV_HALO - 8, D_CONV), F32),
                        pltpu.VMEM((tl, D_CONV), F32),
                        pltpu.VMEM((D_CONV, D_MODEL), BF16)],
        compiler_params=_params(2),
        name="conv_module",
    )(c, conv_w, conv_b3, ln_g3, ln_b3, w_conv_out, gates)


def _expand_heads(v, g, lane_in_pair):
    rows = v.shape[0]
    b = [jnp.broadcast_to(v[:, g * HEADS_PER_GROUP + r:g * HEADS_PER_GROUP + r + 1], (rows, LANES))
         for r in range(HEADS_PER_GROUP)]
    first = lane_in_pair[:rows] < SSM_HEAD_DIM
    return jnp.concatenate([jnp.where(first, b[0], b[1]), jnp.where(first, b[2], b[3])], axis=1)


def _ssd_kernel(xbc_ref, u_ref, zs_ref, wdt_ref, dtb_ref, cw_ref, cb_ref, alog_ref, dskip_ref, ng_ref,
                o_ref, ext_ref, sh_ref, act_ref, state_ref, *, q):
    @pl.when(pl.program_id(1) == 0)
    def _():
        ext_ref[0:SSM_HALO, :] = jnp.zeros((SSM_HALO, D_XBC), F32)
        state_ref[...] = jnp.zeros(state_ref.shape, F32)

    @pl.when(pl.program_id(1) > 0)
    def _():
        ext_ref[0:SSM_HALO, :] = ext_ref[q:q + SSM_HALO, :]

    ext_ref[SSM_HALO:SSM_HALO + q, :] = xbc_ref[...].astype(F32)

    base = SSM_HALO - (SSM_CONV_WIDTH - 1)
    for k in range(SSM_CONV_WIDTH - 1):
        sh_ref[k] = ext_ref[base + k:base + k + q, :]
    for lo in range(0, D_XBC, CONV_CW):
        for r0 in range(0, q, CONV_RC):
            cols = slice(lo, lo + CONV_CW)
            acc = cb_ref[:, cols] + cw_ref[SSM_CONV_WIDTH - 1:SSM_CONV_WIDTH, cols] * \
                ext_ref[SSM_HALO + r0:SSM_HALO + r0 + CONV_RC, cols]
            for k in range(SSM_CONV_WIDTH - 1):
                acc = acc + cw_ref[k:k + 1, cols] * sh_ref[k, r0:r0 + CONV_RC, cols]
            act_ref[r0:r0 + CONV_RC, cols] = acc * _sigmoid(acc)

    dt_raw = jnp.dot(u_ref[...], wdt_ref[...], preferred_element_type=F32) + dtb_ref[...]
    dt = jnp.maximum(dt_raw, 0.0) + jnp.log1p(jnp.exp(-jnp.abs(dt_raw)))
    for r0 in range(0, q, SSD_Q):
        _scan_chunk(slice(r0, r0 + SSD_Q), dt[r0:r0 + SSD_Q, :], act_ref, zs_ref, alog_ref, dskip_ref,
                    ng_ref, o_ref, state_ref)


def _scan_chunk(rows, dt, act_ref, zs_ref, alog_ref, dskip_ref, ng_ref, o_ref, state_ref):
    q = SSD_Q
    adt = dt * (-jnp.exp(alog_ref[...]))
    row = lax.broadcasted_iota(I32, (q, q), 0)
    col = lax.broadcasted_iota(I32, (q, q), 1)
    causal = row >= col
    tril = jnp.where(causal, 1.0, 0.0).astype(F32)
    acs = jnp.dot(tril, adt, preferred_element_type=F32, precision=lax.Precision.HIGHEST)
    acs_t = acs.T
    dt_t = dt.T
    last = acs[q - 1:q, :]
    exp_acs = jnp.exp(acs)
    dt_decay = dt * jnp.exp(last - acs)
    chunk_decay = jnp.exp(last)
    lane_in_pair = lax.broadcasted_iota(I32, (q, LANES), 1)
    head_of_lane = lax.broadcasted_iota(I32, (1, GROUP_CH), 1) // SSM_HEAD_DIM

    b_off = D_INNER
    c_off = D_INNER + SSM_GROUPS * D_STATE
    for g in range(SSM_GROUPS):
        ch = slice(g * GROUP_CH, (g + 1) * GROUP_CH)
        xg = act_ref[rows, ch]
        xg_bf = xg.astype(BF16)
        bg = act_ref[rows, b_off + g * D_STATE:b_off + (g + 1) * D_STATE]
        cg = act_ref[rows, c_off + g * D_STATE:c_off + (g + 1) * D_STATE].astype(BF16)
        cb = lax.dot_general(cg, bg.astype(BF16), (((1,), (1,)), ((), ())),
                             preferred_element_type=F32)
        ms, xblocks = [], []
        for r in range(HEADS_PER_GROUP):
            h = g * HEADS_PER_GROUP + r
            lmat = jnp.exp(jnp.where(causal, acs[:, h:h + 1] - acs_t[h:h + 1, :], -jnp.inf))
            ms.append((cb * lmat * dt_t[h:h + 1, :]).astype(BF16))
            head_mask = jnp.where(head_of_lane == r, 1.0, 0.0).astype(BF16)
            xblocks.append(xg_bf * head_mask)
        y_diag = jnp.dot(jnp.concatenate(ms, axis=1), jnp.concatenate(xblocks, axis=0),
                         preferred_element_type=F32)
        s_prev = state_ref[g]
        y_off = jnp.dot(cg, s_prev.astype(BF16), preferred_element_type=F32)
        yg = y_diag + y_off * _expand_heads(exp_acs, g, lane_in_pair) + xg * dskip_ref[:, ch]
        xw = (xg * _expand_heads(dt_decay, g, lane_in_pair)).astype(BF16)
        dec = _expand_heads(chunk_decay, g, lane_in_pair)
        state_ref[g] = s_prev * dec + jnp.dot(bg.T.astype(BF16), xw, preferred_element_type=F32)

        yz = yg * zs_ref[rows, g * GROUP_CH:(g + 1) * GROUP_CH].astype(F32)
        ms = jnp.mean(yz * yz, axis=-1, keepdims=True)
        yn = yz * lax.rsqrt(ms + RMS_EPS) * ng_ref[:, g * GROUP_CH:(g + 1) * GROUP_CH]
        o_ref[rows, g * GROUP_CH:(g + 1) * GROUP_CH] = yn.astype(o_ref.dtype)


def _ssd(xbc, u, zs, w_dt, dtb3, ssm_conv_w, ssm_conv_b3, alog3, dskip3, ng3, layer, bsz, seqlen):
    t = xbc.shape[0]
    q = min(SSD_STEP, seqlen)
    nq = seqlen // q
    tile = lambda width: pl.BlockSpec((q, width), lambda b, i: (b * nq + i, 0))
    vec = lambda width: pl.BlockSpec((None, 1, width), lambda b, i: (layer, 0, 0))
    return pl.pallas_call(
        functools.partial(_ssd_kernel, q=q),
        grid=(bsz, nq),
        in_specs=[tile(D_XBC), tile(D_MODEL), tile(D_INNER),
                  pl.BlockSpec((None, D_MODEL, LANES), lambda b, i: (layer, 0, 0)), vec(LANES),
                  pl.BlockSpec((None, SSM_CONV_WIDTH, D_XBC), lambda b, i: (layer, 0, 0)),
                  vec(D_XBC), vec(LANES), vec(D_INNER), vec(D_INNER)],
        out_specs=tile(D_INNER),
        out_shape=jax.ShapeDtypeStruct((t, D_INNER), BF16),
        scratch_shapes=[pltpu.VMEM((q + SSM_HALO, D_XBC), F32),
                        pltpu.VMEM((SSM_CONV_WIDTH - 1, q, D_XBC), F32),
                        pltpu.VMEM((q, D_XBC), F32),
                        pltpu.VMEM((SSM_GROUPS, D_STATE, GROUP_CH), F32)],
        compiler_params=_params(2),
        name="ssd_mixer",
    )(xbc, u, zs, w_dt, dtb3, ssm_conv_w, ssm_conv_b3, alog3, dskip3, ng3)


def _first_argmax(vals):
    best, idx = vals[0], jnp.zeros(vals[0].shape, I32)
    for j in range(1, len(vals)):
        gt = vals[j] > best
        idx = jnp.where(gt, j, idx)
        best = jnp.where(gt, vals[j], best)
    return idx, best


def _select(idx, vals):
    out = vals[len(vals) - 1]
    for j in range(len(vals) - 2, -1, -1):
        out = jnp.where(idx == j, vals[j], out)
    return out


def _outproj_kernel(yn_ref, y1_ref, gs_ref, x_ref, wso_ref, wo_ref, lg_ref, lb_ref, wr_ref, rb_ref,
                    x1_ref, x1b_ref, pos_ref, wts_ref, cnt_ref, seg_ref,
                    wso_bf, wo_bf, base_ref, *, tm):
    @pl.when(pl.program_id(0) == 0)
    def _():
        wso_bf[...] = wso_ref[...].astype(BF16)
        wo_bf[...] = wo_ref[...].astype(BF16)
        base_ref[...] = jnp.zeros(base_ref.shape, F32)

    y_ssm = jnp.dot(yn_ref[...], wso_bf[...], preferred_element_type=F32)
    merged = y1_ref[...].astype(F32) + gs_ref[...].astype(F32) * y_ssm
    mix = jnp.dot(merged.astype(BF16), wo_bf[...], preferred_element_type=F32)
    x1 = _layer_norm(ALPHA * x_ref[...] + mix, lg_ref[...], lb_ref[...])
    x1_ref[...] = x1
    x1b_ref[...] = x1.astype(BF16)

    logits = lax.dot_general(wr_ref[...], x1, (((1,), (1,)), ((), ())),
                             preferred_element_type=F32, precision=lax.Precision.HIGHEST)
    scores = _sigmoid(logits)
    sel = scores + rb_ref[...]
    sel_rows = [sel[e:e + 1, :] for e in range(N_EXPERTS)]
    sc_rows = [scores[e:e + 1, :] for e in range(N_EXPERTS)]

    gscores = []
    for gi in range(N_EXPERT_GROUPS):
        v = sel_rows[gi * EXPERTS_PER_GROUP:(gi + 1) * EXPERTS_PER_GROUP]
        best = None
        for a in range(EXPERTS_PER_GROUP):
            for b in range(a + 1, EXPERTS_PER_GROUP):
                s = v[a] + v[b]
                best = s if best is None else jnp.maximum(best, s)
        gscores.append(best)
    grp, _ = _first_argmax(gscores)

    sel_in = [_select(grp, [sel_rows[gi * EXPERTS_PER_GROUP + j] for gi in range(N_EXPERT_GROUPS)])
              for j in range(EXPERTS_PER_GROUP)]
    sc_in = [_select(grp, [sc_rows[gi * EXPERTS_PER_GROUP + j] for gi in range(N_EXPERT_GROUPS)])
             for j in range(EXPERTS_PER_GROUP)]
    i1, _ = _first_argmax(sel_in)
    neg = jnp.full(sel_in[0].shape, -jnp.inf, F32)
    i2, _ = _first_argmax([jnp.where(i1 == j, neg, sel_in[j]) for j in range(EXPERTS_PER_GROUP)])
    s1 = _select(i1, sc_in)
    s2 = _select(i2, sc_in)
    tot = s1 + s2
    e1 = grp * EXPERTS_PER_GROUP + i1
    e2 = grp * EXPERTS_PER_GROUP + i2
    wts_ref[0:1, :] = s1 / tot
    wts_ref[1:2, :] = s2 / tot

    eio = lax.broadcasted_iota(I32, (N_EXPERTS, tm), 0)
    oh1 = jnp.where(eio == e1, 1.0, 0.0).astype(F32)
    oh2 = jnp.where(eio == e2, 1.0, 0.0).astype(F32)
    both = oh1 + oh2
    srow = lax.broadcasted_iota(I32, (tm, tm), 0)
    scol = lax.broadcasted_iota(I32, (tm, tm), 1)
    before = jnp.where(srow < scol, 1.0, 0.0).astype(BF16)
    cum = jnp.dot(both.astype(BF16), before, preferred_element_type=F32)
    cnt8 = jnp.floor((jnp.sum(both, axis=1, keepdims=True) + 7.0) * 0.125) * 8.0
    cnt8_l = jnp.broadcast_to(cnt8, (N_EXPERTS, LANES))
    erow = lax.broadcasted_iota(I32, (N_EXPERTS, N_EXPERTS), 0)
    ecol = lax.broadcasted_iota(I32, (N_EXPERTS, N_EXPERTS), 1)
    seg_off = jnp.dot(jnp.where(ecol < erow, 1.0, 0.0).astype(F32), cnt8_l,
                      preferred_element_type=F32, precision=lax.Precision.HIGHEST)
    base = base_ref[...]
    in_tile = seg_off[:, 0:1] + cum
    pos_ref[0:1, :] = jnp.sum(oh1 * in_tile, axis=0, keepdims=True).astype(I32)
    pos_ref[1:2, :] = jnp.sum(oh2 * in_tile, axis=0, keepdims=True).astype(I32)
    lane = lax.broadcasted_iota(I32, (N_EXPERTS, LANES), 1)
    seg_ref[...] = jnp.where(lane == 0, seg_off, jnp.where(lane == 1, cnt8_l, base)).astype(I32)
    base_ref[...] = base + cnt8_l
    cnt_ref[...] = base_ref[...].astype(I32)


def _outproj(yn, y1g, gates, x, w_ssm_out, w_out, ln_g3, ln_b3, wr_t, rbias, layer, tm):
    t = x.shape[0]
    tile = lambda width: pl.BlockSpec((tm, width), lambda m: (m, 0))
    vec = pl.BlockSpec((None, 1, D_MODEL), lambda m: (layer, 0, 0))
    pair = pl.BlockSpec((2, tm), lambda m: (0, m))
    return pl.pallas_call(
        functools.partial(_outproj_kernel, tm=tm),
        grid=(t // tm,),
        in_specs=[tile(D_INNER), tile(D_MODEL),
                  pl.BlockSpec((tm, D_MODEL), lambda m: (m, 1)),
                  tile(D_MODEL),
                  pl.BlockSpec((None, D_INNER, D_MODEL), lambda m: (layer, 0, 0)),
                  pl.BlockSpec((None, D_MODEL, D_MODEL), lambda m: (layer, 0, 0)),
                  vec, vec,
                  pl.BlockSpec((N_EXPERTS, D_MODEL), lambda m: (0, 0)),
                  pl.BlockSpec((N_EXPERTS, 1), lambda m: (0, 0))],
        out_specs=[tile(D_MODEL), tile(D_MODEL), pair, pair,
                   pl.BlockSpec((N_EXPERTS, LANES), lambda m: (0, 0)),
                   pl.BlockSpec((None, N_EXPERTS, LANES), lambda m: (m, 0, 0))],
        out_shape=[jax.ShapeDtypeStruct((t, D_MODEL), F32),
                   jax.ShapeDtypeStruct((t, D_MODEL), BF16),
                   jax.ShapeDtypeStruct((2, t), I32),
                   jax.ShapeDtypeStruct((2, t), F32),
                   jax.ShapeDtypeStruct((N_EXPERTS, LANES), I32),
                   jax.ShapeDtypeStruct((t // tm, N_EXPERTS, LANES), I32)],
        scratch_shapes=[pltpu.VMEM((D_INNER, D_MODEL), BF16),
                        pltpu.VMEM((D_MODEL, D_MODEL), BF16),
                        pltpu.VMEM((N_EXPERTS, LANES), F32)],
        compiler_params=_params(1),
        name="outproj_ln_router",
    )(yn, y1g, gates, x, w_ssm_out, w_out, ln_g3, ln_b3, wr_t, rbias)


def _segment_blocks(tile, segoff_ref, cnt8_ref, gdst_ref, make_copy, act):
    for e in range(N_EXPERTS):
        idx = tile * N_EXPERTS + e
        n8 = lax.shift_right_logical(cnt8_ref[idx], 3)
        local = segoff_ref[idx]
        glob = gdst_ref[idx]
        for bit in SEGMENT_BITS:
            rows = 8 * bit
            hit = (n8 & bit) != 0

            @pl.when(hit)
            def _(local=local, glob=glob, rows=rows):
                act(make_copy(pl.multiple_of(local, 8), pl.multiple_of(glob, 8), rows))

            step = jnp.where(hit, rows, 0)
            local = local + step
            glob = glob + step


def _dispatch_kernel(fill_ref, segoff_ref, cnt8_ref, gdst_ref, pos_ref, x_ref, xs_hbm,
                     sorted_ref, zeros_ref, sem, blk_sem, seg_sems, *, tm, n_blk, n_tiles):
    m = pl.program_id(0)
    slot = lax.rem(m, 2)

    def zero_rows(row):
        return pltpu.make_async_copy(zeros_ref.at[pl.ds(0, 8)],
                                     xs_hbm.at[pl.ds(pl.multiple_of(row, 8), 8)], sem)

    def zero_block(b):
        return pltpu.make_async_copy(
            zeros_ref, xs_hbm.at[pl.ds(pl.multiple_of(b * MOE_ROWS, MOE_ROWS), MOE_ROWS)], blk_sem)

    @pl.when(m == 0)
    def _():
        zeros_ref[...] = jnp.zeros(zeros_ref.shape, F32)
        n_active = fill_ref[2 * N_EXPERTS]
        for e in range(N_EXPERTS):
            pad_start = fill_ref[e]
            lax.fori_loop(0, fill_ref[N_EXPERTS + e],
                          lambda j, c: (zero_rows(pad_start + 8 * j).start(), c)[1], 0)
        lax.fori_loop(n_active, n_blk, lambda b, c: (zero_block(b).start(), c)[1], 0)
        lax.fori_loop(0, fill_ref[2 * N_EXPERTS + 1], lambda j, c: (zero_rows(0).wait(), c)[1], 0)
        lax.fori_loop(n_active, n_blk, lambda b, c: (zero_block(b).wait(), c)[1], 0)

    def for_each_segment_block(tile, s, act):
        copy = lambda local, glob, rows: pltpu.make_async_copy(
            sorted_ref.at[s, pl.ds(local, rows)], xs_hbm.at[pl.ds(glob, rows)], seg_sems.at[s])
        _segment_blocks(tile, segoff_ref, cnt8_ref, gdst_ref, copy, act)

    start = lambda copy: copy.start()
    wait = lambda copy: copy.wait()

    @pl.when(m >= 2)
    def _():
        for_each_segment_block(m - 2, slot, wait)

    pos = pos_ref[...]
    srow = lax.broadcasted_iota(I32, (sorted_ref.shape[1], tm), 0)
    onehot = jnp.where(srow == pos[0:1, :], 1.0, jnp.where(srow == pos[1:2, :], 1.0, 0.0)).astype(BF16)
    sorted_ref[slot] = jnp.dot(onehot, x_ref[...], preferred_element_type=F32)
    for_each_segment_block(m, slot, start)

    @pl.when(m == n_tiles - 1)
    def _():
        if n_tiles > 1:
            for_each_segment_block(m - 1, 1 - slot, wait)
        for_each_segment_block(m, slot, wait)


def _dispatch(fill, seg_off, seg_cnt8, seg_dst, pos, x1b, n_blk, tm):
    t = x1b.shape[0]
    n_tiles = t // tm
    sorted_rows = 2 * tm + 8 * N_EXPERTS
    grid_spec = pltpu.PrefetchScalarGridSpec(
        num_scalar_prefetch=4,
        grid=(n_tiles,),
        in_specs=[pl.BlockSpec((2, tm), lambda m, *_: (0, m)),
                  pl.BlockSpec((tm, D_MODEL), lambda m, *_: (m, 0))],
        out_specs=pl.BlockSpec(memory_space=pl.ANY),
        scratch_shapes=[pltpu.VMEM((2, sorted_rows, D_MODEL), F32),
                        pltpu.VMEM((MOE_ROWS, D_MODEL), F32),
                        pltpu.SemaphoreType.DMA(()), pltpu.SemaphoreType.DMA(()),
                        pltpu.SemaphoreType.DMA((2,))],
    )
    return pl.pallas_call(
        functools.partial(_dispatch_kernel, tm=tm, n_blk=n_blk, n_tiles=n_tiles),
        grid_spec=grid_spec,
        out_shape=jax.ShapeDtypeStruct((n_blk * MOE_ROWS, D_MODEL), F32),
        compiler_params=_params(1),
        name="moe_dispatch",
    )(fill, seg_off, seg_cnt8, seg_dst, pos, x1b)


def _expert_kernel(src_ref, exp_ref, nvalid_ref, xs_ref, wg_ref, wu_ref, wd_ref, y_ref,
                   wgu_bf, wd_bf):
    i = pl.program_id(0)
    nvalid = nvalid_ref[i]
    changed = (i == 0) | (exp_ref[i] != exp_ref[jnp.maximum(i - 1, 0)])

    @pl.when((nvalid > 0) & changed)
    def _():
        wgu_bf[:, :D_EXPERT] = wg_ref[...].astype(BF16)
        wgu_bf[:, D_EXPERT:] = wu_ref[...].astype(BF16)
        wd_bf[...] = wd_ref[...].astype(BF16)

    @pl.when(nvalid > 0)
    def _():
        rows = lax.broadcasted_iota(I32, (MOE_ROWS, 1), 0)
        x = jnp.where(rows < nvalid, xs_ref[...], 0.0).astype(BF16)
        gu = jnp.dot(x, wgu_bf[...], preferred_element_type=F32)
        hg = gu[:, :D_EXPERT]
        h = hg * _sigmoid(hg) * gu[:, D_EXPERT:]
        y_ref[...] = jnp.dot(h.astype(BF16), wd_bf[...], preferred_element_type=F32)

    @pl.when(nvalid == 0)
    def _():
        y_ref[...] = jnp.zeros(y_ref.shape, F32)


def _experts(blk_src, blk_exp, blk_nvalid, xs, wg, wu, wd, layer):
    n_rows = xs.shape[0]
    n_blk = n_rows // MOE_ROWS
    grid_spec = pltpu.PrefetchScalarGridSpec(
        num_scalar_prefetch=3,
        grid=(n_blk,),
        in_specs=[pl.BlockSpec((MOE_ROWS, D_MODEL), lambda i, src, exp, nv: (src[i], 0)),
                  pl.BlockSpec((None, None, D_MODEL, D_EXPERT),
                               lambda i, src, exp, nv: (layer, exp[i], 0, 0)),
                  pl.BlockSpec((None, None, D_MODEL, D_EXPERT),
                               lambda i, src, exp, nv: (layer, exp[i], 0, 0)),
                  pl.BlockSpec((None, None, D_EXPERT, D_MODEL),
                               lambda i, src, exp, nv: (layer, exp[i], 0, 0))],
        out_specs=pl.BlockSpec((MOE_ROWS, D_MODEL), lambda i, src, exp, nv: (i, 0)),
        scratch_shapes=[pltpu.VMEM((D_MODEL, 2 * D_EXPERT), BF16),
                        pltpu.VMEM((D_EXPERT, D_MODEL), BF16)],
    )
    return pl.pallas_call(
        _expert_kernel,
        grid_spec=grid_spec,
        out_shape=jax.ShapeDtypeStruct((n_rows, D_MODEL), F32),
        compiler_params=_params(1),
        name="moe_experts",
    )(blk_src, blk_exp, blk_nvalid, xs, wg, wu, wd)


def _moe_plan(counts, seg, t):
    cnt = counts[:, 0]
    nblk_e = (cnt + MOE_ROWS - 1) // MOE_ROWS
    blk_end = jnp.cumsum(nblk_e)
    blk_start = blk_end - nblk_e
    n_active = blk_end[N_EXPERTS - 1]
    experts = jnp.arange(N_EXPERTS, dtype=I32)
    row_start = blk_start * MOE_ROWS
    n_tiles = seg.shape[0]
    n_blk = -(-(2 * t + 8 * N_EXPERTS * n_tiles) // MOE_ROWS) + N_EXPERTS
    ids = jnp.arange(n_blk, dtype=I32)
    src = jnp.minimum(ids, n_active - 1)
    exp = jnp.minimum(jnp.sum((blk_end[None, :] <= src[:, None]).astype(I32), axis=1), N_EXPERTS - 1)
    onehot = (exp[:, None] == experts[None, :]).astype(I32)
    pick = lambda table: jnp.sum(onehot * table[None, :], axis=1)
    left = pick(cnt) - (src - pick(blk_start)) * MOE_ROWS
    nvalid = jnp.where(ids < n_active, jnp.clip(left, 0, MOE_ROWS), 0)
    pad8 = (nblk_e * MOE_ROWS - cnt) // 8
    fill = jnp.concatenate([row_start + cnt, pad8, jnp.stack([n_active, jnp.sum(pad8)])]).astype(I32)
    seg_off = seg[:, :, 0].reshape(-1)
    seg_cnt8 = seg[:, :, 1].reshape(-1)
    seg_dst = (row_start[None, :] + seg[:, :, 2]).reshape(-1)
    return src.astype(I32), exp, nvalid.astype(I32), fill, seg_off, seg_cnt8, seg_dst, n_blk


def _final_kernel(segoff_ref, cnt8_ref, gsrc_ref, x1_ref, x1b_ref, p_ref, wt_ref, pos_ref, wpu_ref,
                  wpg_ref, bpg_ref, lg_ref, lb_ref, y_hbm, x2_ref, x2b_ref, ys_ref, wpu_bf, wpg_bf, sems,
                  *, tm, n_tiles):
    m = pl.program_id(0)
    slot = lax.rem(m, 2)

    def for_each_segment_block(tile, s, act):
        copy = lambda local, glob, rows: pltpu.make_async_copy(
            y_hbm.at[pl.ds(glob, rows)], ys_ref.at[s, pl.ds(local, rows)], sems.at[s])
        _segment_blocks(tile, segoff_ref, cnt8_ref, gsrc_ref, copy, act)

    @pl.when(m == 0)
    def _():
        wpu_bf[...] = wpu_ref[...].astype(BF16)
        wpg_bf[...] = wpg_ref[...].astype(BF16)
        for_each_segment_block(0, 0, lambda copy: copy.start())

    @pl.when(m + 1 < n_tiles)
    def _():
        for_each_segment_block(m + 1, 1 - slot, lambda copy: copy.start())

    up = jnp.dot(p_ref[...].astype(BF16), wpu_bf[...], preferred_element_type=F32)
    gate = _sigmoid(jnp.dot(x1b_ref[...], wpg_bf[...], preferred_element_type=F32) + bpg_ref[...])
    resid = ALPHA * x1_ref[...] + up * gate

    for_each_segment_block(m, slot, lambda copy: copy.wait())

    n_sorted = ys_ref.shape[1]
    used = segoff_ref[m * N_EXPERTS + N_EXPERTS - 1] + cnt8_ref[m * N_EXPERTS + N_EXPERTS - 1]
    srow = lax.broadcasted_iota(I32, (n_sorted, 1), 0)
    ys = jnp.where(srow < used, ys_ref[slot], 0.0).astype(BF16)
    pos = pos_ref[...]
    scol = lax.broadcasted_iota(I32, (tm, n_sorted), 1)
    wt = wt_ref[...]
    moe = jnp.zeros((tm, D_MODEL), F32)
    for k in range(TOP_K):
        pick = jnp.where(scol == pos[:, k:k + 1], 1.0, 0.0).astype(BF16)
        moe = moe + wt[:, k:k + 1] * jnp.dot(pick, ys, preferred_element_type=F32)
    x2 = _layer_norm(resid + moe, lg_ref[...], lb_ref[...])
    x2_ref[...] = x2
    x2b_ref[...] = x2.astype(BF16)


def _final(seg_off, seg_cnt8, seg_src, x1, x1b, p, wt_tok, pos_tok, w_ple_up, w_ple_gate, b_pg3,
           ln_g3, ln_b3, y_rows, layer, tm):
    t = x1.shape[0]
    n_tiles = t // tm
    sorted_rows = 2 * tm + 8 * N_EXPERTS
    tile = lambda width: pl.BlockSpec((tm, width), lambda m, *_: (m, 0))
    vec = pl.BlockSpec((None, 1, D_MODEL), lambda m, *_: (layer, 0, 0))
    grid_spec = pltpu.PrefetchScalarGridSpec(
        num_scalar_prefetch=3,
        grid=(n_tiles,),
        in_specs=[tile(D_MODEL), tile(D_MODEL),
                  pl.BlockSpec((None, tm, PLE_DIM), lambda m, *_: (layer, m, 0)),
                  tile(2), tile(2),
                  pl.BlockSpec((None, PLE_DIM, D_MODEL), lambda m, *_: (layer, 0, 0)),
                  pl.BlockSpec((None, D_MODEL, D_MODEL), lambda m, *_: (layer, 0, 0)),
                  vec, vec, vec,
                  pl.BlockSpec(memory_space=pl.ANY)],
        out_specs=[tile(D_MODEL), tile(D_MODEL)],
        scratch_shapes=[pltpu.VMEM((2, sorted_rows, D_MODEL), F32),
                        pltpu.VMEM((PLE_DIM, D_MODEL), BF16), pltpu.VMEM((D_MODEL, D_MODEL), BF16),
                        pltpu.SemaphoreType.DMA((2,))],
    )
    return pl.pallas_call(
        functools.partial(_final_kernel, tm=tm, n_tiles=n_tiles),
        grid_spec=grid_spec,
        out_shape=[jax.ShapeDtypeStruct((t, D_MODEL), F32),
                   jax.ShapeDtypeStruct((t, D_MODEL), BF16)],
        compiler_params=_params(1),
        name="combine_ple_ln",
    )(seg_off, seg_cnt8, seg_src, x1, x1b, p, wt_tok, pos_tok, w_ple_up, w_ple_gate, b_pg3,
      ln_g3, ln_b3, y_rows)


def kernel(x, p, w_in, b_glu, b_branch_gate, conv_w, conv_b, conv_ln_g, conv_ln_b, w_conv_out,
           ssm_conv_w, ssm_conv_b, dt_bias, a_log, d_skip, ssm_norm_g, w_ssm_out, w_out,
           ln1_g, ln1_b, w_router, router_bias, w_exp_gate, w_exp_up, w_exp_down,
           w_ple_up, w_ple_gate, b_ple_gate, ln2_g, ln2_b):
    bsz, seqlen, d = x.shape
    depth = w_in.shape[0]
    t = bsz * seqlen
    tm_moe = min(MOE_TILE, t)

    row3 = lambda a: a.reshape(a.shape[0], 1, a.shape[1])
    pad_lanes = lambda a: jnp.pad(a, ((0, 0), (0, LANES - a.shape[1])))
    b_glu3, b_gate3 = row3(b_glu), row3(b_branch_gate)
    conv_b3, cln_g3, cln_b3 = row3(conv_b), row3(conv_ln_g), row3(conv_ln_b)
    ssm_conv_b3, ng3 = row3(ssm_conv_b), row3(ssm_norm_g)
    dtb3, alog3 = row3(pad_lanes(dt_bias)), row3(pad_lanes(a_log))
    dskip3 = row3(jnp.repeat(d_skip, SSM_HEAD_DIM, axis=1))
    ln1_g3, ln1_b3, ln2_g3, ln2_b3 = row3(ln1_g), row3(ln1_b), row3(ln2_g), row3(ln2_b)
    b_pg3 = row3(b_ple_gate)
    w_in_t = jnp.swapaxes(w_in, 1, 2)
    w_gate, w_dt = _tail_weights(w_in_t)
    wr_t = w_router.T
    rbias = router_bias.reshape(N_EXPERTS, 1)
    p2 = p.reshape(depth, t, PLE_DIM)

    xf = x.reshape(t, d)
    xb = xf.astype(BF16)
    for i in range(depth):
        c = _glu(xb, w_in_t, b_glu3, i)
        zs = _matmul(xb, w_in_t, lambda n: (i, OFF_Z // 1024 + n, 0), None, None,
                     D_INNER, 1024, "silu", BF16, w_rows_are_outputs=True)
        xbc = _matmul(xb, w_in_t, lambda n: (i, OFF_XBC // 1024 + n, 0), None, None,
                      D_XBC, 1024, "none", BF16, w_rows_are_outputs=True)
        gates = _matmul(xb, w_gate, lambda n: (i, 0, n), b_gate3, lambda n: (i, 0, n),
                        2 * D_MODEL, 1024, "sigmoid", BF16)
        y1g = _convbranch(c, gates, conv_w, conv_b3, cln_g3, cln_b3, w_conv_out, i, bsz, seqlen)
        yn = _ssd(xbc, xb, zs, w_dt, dtb3, ssm_conv_w, ssm_conv_b3, alog3, dskip3, ng3, i, bsz, seqlen)
        x1, x1b, pos, wts, counts, seg = _outproj(
            yn, y1g, gates, xf, w_ssm_out, w_out, ln1_g3, ln1_b3, wr_t, rbias, i, tm_moe)
        (blk_src, blk_exp, blk_nvalid, fill, seg_off, seg_cnt8, seg_row,
         n_blk) = _moe_plan(counts, seg, t)
        xs = _dispatch(fill, seg_off, seg_cnt8, seg_row, pos, x1b, n_blk, tm_moe)
        y_rows = _experts(blk_src, blk_exp, blk_nvalid, xs, w_exp_gate, w_exp_up, w_exp_down, i)
        xf, xb = _final(seg_off, seg_cnt8, seg_row, x1, x1b, p2, wts.T, pos.T,
                        w_ple_up, w_ple_gate, b_pg3, ln2_g3, ln2_b3, y_rows, i, tm_moe)
    return xf.reshape(bsz, seqlen, d)
```

```python
import functools

import jax
import jax.numpy as jnp
from jax import lax
from jax.experimental import pallas as pl
from jax.experimental.pallas import tpu as pltpu

F32 = jnp.float32
BF16 = jnp.bfloat16
I32 = jnp.int32

D_MODEL = 1024
D_CONV = 1024
CONV_WIDTH = 31
D_INNER = 2048
SSM_HEAD_DIM = 64
SSM_HEADS = 32
SSM_GROUPS = 8
HEADS_PER_GROUP = 4
D_STATE = 128
SSM_CONV_WIDTH = 4
D_XBC = D_INNER + 2 * SSM_GROUPS * D_STATE
GROUP_CH = HEADS_PER_GROUP * SSM_HEAD_DIM
N_EXPERTS = 16
N_EXPERT_GROUPS = 4
TOP_K = 2
EXPERTS_PER_GROUP = 4
D_EXPERT = 512
PLE_DIM = 256
DEPTH = 4
ALPHA = (2.0 * DEPTH) ** 0.25
LN_EPS = 1e-5
RMS_EPS = 1e-5

LANES = 128
CONV_HALO = 32
CONV_TILE = 512
CONV_RC = 64
CONV_CW = 256
SSM_HALO = 8
SSD_Q = 128
SSD_STEP = 512
SSD_CONV_ROWS = 256
MOE_ROWS = 512
MOE_TILE = 512
SEGMENT_BITS = (64, 32, 16, 8, 4, 2, 1)
VMEM_LIMIT = 48 * 1024 * 1024
INPROJ_TM = 2048
LARGE_VMEM_LIMIT = 56 * 1024 * 1024

OFF_Z = 2 * D_CONV
OFF_XBC = OFF_Z + D_INNER
OFF_DT = OFF_XBC + D_XBC


def _sigmoid(x):
    return 1.0 / (1.0 + jnp.exp(-x))


def _layer_norm(x, g, b):
    mu = jnp.mean(x, axis=-1, keepdims=True)
    xc = x - mu
    var = jnp.mean(xc * xc, axis=-1, keepdims=True)
    return xc * lax.rsqrt(var + LN_EPS) * g + b


def _params(n_axes, vmem_limit=VMEM_LIMIT):
    return pltpu.CompilerParams(dimension_semantics=("arbitrary",) * n_axes,
                                vmem_limit_bytes=vmem_limit)


def _mm_kernel(x_ref, w_ref, *rest, act, has_bias, w_rows_are_outputs):
    if has_bias:
        b_ref, o_ref, wbf_ref = rest
    else:
        o_ref, wbf_ref = rest

    @pl.when(pl.program_id(1) == 0)
    def _():
        w = w_ref[...]
        wbf_ref[...] = (w.T if w_rows_are_outputs else w).astype(BF16)

    acc = jnp.dot(x_ref[...], wbf_ref[...], preferred_element_type=F32)
    if has_bias:
        acc = acc + b_ref[...]
    if act == "sigmoid":
        acc = _sigmoid(acc)
    elif act == "silu":
        acc = acc * _sigmoid(acc)
    o_ref[...] = acc.astype(o_ref.dtype)


def _matmul(x, w, w_index, bias, b_index, n_out, tn, act, out_dtype, w_rows_are_outputs=False):
    t, k = x.shape
    tm = min(INPROJ_TM, t)
    w_block = (None,) * (w.ndim - 2) + ((tn, k) if w_rows_are_outputs else (k, tn))
    in_specs = [pl.BlockSpec((tm, k), lambda n, m: (m, 0)),
                pl.BlockSpec(w_block, lambda n, m: w_index(n))]
    args = [x, w]
    if bias is not None:
        b_block = (None,) * (bias.ndim - 2) + (1, tn)
        in_specs.append(pl.BlockSpec(b_block, lambda n, m: b_index(n)))
        args.append(bias)
    return pl.pallas_call(
        functools.partial(_mm_kernel, act=act, has_bias=bias is not None,
                          w_rows_are_outputs=w_rows_are_outputs),
        grid=(n_out // tn, t // tm),
        in_specs=in_specs,
        out_specs=pl.BlockSpec((tm, tn), lambda n, m: (m, n)),
        out_shape=jax.ShapeDtypeStruct((t, n_out), out_dtype),
        scratch_shapes=[pltpu.VMEM((k, tn), BF16)],
        compiler_params=_params(2, LARGE_VMEM_LIMIT),
        name="inproj_" + act,
    )(*args)


def _tail_weights_kernel(wt_hbm, wgate_ref, wdt_ref, buf_ref, sem):
    n_tail = buf_ref.shape[0]
    copy = pltpu.make_async_copy(wt_hbm.at[pl.program_id(0), pl.ds(OFF_DT, n_tail)], buf_ref, sem)
    copy.start()
    copy.wait()
    lane = lax.broadcasted_iota(I32, (D_MODEL, LANES), 1)
    wdt_ref[...] = jnp.where(lane < SSM_HEADS, buf_ref[0:LANES, :].T, 0.0).astype(BF16)
    step = 4 * LANES
    for lo in range(0, 2 * D_MODEL, step):
        rows = slice(SSM_HEADS + lo, SSM_HEADS + lo + step)
        wgate_ref[:, lo:lo + step] = buf_ref[rows, :].T.astype(BF16)


def _tail_weights(w_in_t):
    depth, n_all, k = w_in_t.shape
    n = 2 * D_MODEL
    return pl.pallas_call(
        _tail_weights_kernel,
        grid=(depth,),
        in_specs=[pl.BlockSpec(memory_space=pl.ANY)],
        out_specs=[pl.BlockSpec((None, k, n), lambda l: (l, 0, 0)),
                   pl.BlockSpec((None, k, LANES), lambda l: (l, 0, 0))],
        out_shape=[jax.ShapeDtypeStruct((depth, k, n), BF16),
                   jax.ShapeDtypeStruct((depth, k, LANES), BF16)],
        scratch_shapes=[pltpu.VMEM((n_all - OFF_DT, k), F32), pltpu.SemaphoreType.DMA(())],
        compiler_params=_params(1),
        name="tail_weights",
    )(w_in_t)


def _glu_kernel(x_ref, wa_ref, wg_ref, ba_ref, bg_ref, o_ref, wa_bf, wg_bf):
    @pl.when(pl.program_id(1) == 0)
    def _():
        wa_bf[...] = wa_ref[...].T.astype(BF16)
        wg_bf[...] = wg_ref[...].T.astype(BF16)

    x = x_ref[...]
    a = jnp.dot(x, wa_bf[...], preferred_element_type=F32) + ba_ref[...]
    g = jnp.dot(x, wg_bf[...], preferred_element_type=F32) + bg_ref[...]
    o_ref[...] = (a * _sigmoid(g)).astype(o_ref.dtype)


def _glu(x, w_in_t, b_glu3, layer, tn=512):
    t, k = x.shape
    tm = min(INPROJ_TM, t)
    half = D_CONV // tn
    return pl.pallas_call(
        _glu_kernel,
        grid=(half, t // tm),
        in_specs=[pl.BlockSpec((tm, k), lambda n, m: (m, 0)),
                  pl.BlockSpec((None, tn, k), lambda n, m: (layer, n, 0)),
                  pl.BlockSpec((None, tn, k), lambda n, m: (layer, n + half, 0)),
                  pl.BlockSpec((None, 1, tn), lambda n, m: (layer, 0, n)),
                  pl.BlockSpec((None, 1, tn), lambda n, m: (layer, 0, n + half))],
        out_specs=pl.BlockSpec((tm, tn), lambda n, m: (m, n)),
        out_shape=jax.ShapeDtypeStruct((t, D_CONV), BF16),
        scratch_shapes=[pltpu.VMEM((k, tn), BF16), pltpu.VMEM((k, tn), BF16)],
        compiler_params=_params(2, LARGE_VMEM_LIMIT),
        name="inproj_glu",
    )(x, w_in_t, w_in_t, b_glu3, b_glu3)


def _convbranch_kernel(c_ref, cw_ref, cb_ref, lg_ref, lb_ref, w_ref, gate_ref, o_ref,
                       ext_ref, sh_ref, conv_ref, wbf_ref, *, tl):
    first = (pl.program_id(0) == 0) & (pl.program_id(1) == 0)

    @pl.when(first)
    def _():
        wbf_ref[...] = w_ref[...].astype(BF16)

    @pl.when(pl.program_id(1) == 0)
    def _():
        ext_ref[0:CONV_HALO, :] = jnp.zeros((CONV_HALO, D_CONV), F32)

    @pl.when(pl.program_id(1) > 0)
    def _():
        ext_ref[0:CONV_HALO, :] = ext_ref[tl:tl + CONV_HALO, :]

    ext_ref[CONV_HALO:CONV_HALO + tl, :] = c_ref[...].astype(F32)

    sh_rows = tl + CONV_HALO - 8
    for s in range(1, 8):
        sh_ref[s - 1] = ext_ref[s:s + sh_rows, :]

    base = CONV_HALO - (CONV_WIDTH - 1)

    n_groups = CONV_RC // 8
    taps_of_shift = [[(a, 8 * a + s - base) for a in range(5) if 0 <= 8 * a + s - base < CONV_WIDTH]
                     for s in range(8)]

    def conv_rows(rc, carry):
        r0 = pl.multiple_of(rc * CONV_RC, CONV_RC)
        for lo in range(0, D_CONV, LANES):
            cols = slice(lo, lo + LANES)
            w = [jnp.broadcast_to(cw_ref[k:k + 1, cols], (8, LANES)) for k in range(CONV_WIDTH)]
            acc = [jnp.broadcast_to(cb_ref[:, cols], (8, LANES))] * n_groups
            for s in range(8):
                taps = taps_of_shift[s]
                for j in range(n_groups + max(a for a, _ in taps)):
                    used = [(a, k) for a, k in taps if 0 <= j - a < n_groups]
                    if not used:
                        continue
                    rows = pl.ds(r0 + 8 * j, 8)
                    x = ext_ref[rows, cols] if s == 0 else sh_ref[s - 1, rows, cols]
                    for a, k in used:
                        acc[j - a] = acc[j - a] + w[k] * x
            for i in range(n_groups):
                conv_ref[pl.ds(r0 + 8 * i, 8), cols] = acc[i]
        return carry

    lax.fori_loop(0, tl // CONV_RC, conv_rows, 0)
    h = _layer_norm(conv_ref[...], lg_ref[...], lb_ref[...])
    h = h * _sigmoid(h)
    y = jnp.dot(h.astype(BF16), wbf_ref[...], preferred_element_type=F32)
    o_ref[...] = (y * gate_ref[...].astype(F32)).astype(o_ref.dtype)


def _convbranch(c, gates, conv_w, conv_b3, ln_g3, ln_b3, w_conv_out, layer, bsz, seqlen):
    t = c.shape[0]
    tl = min(CONV_TILE, seqlen)
    nl = seqlen // tl
    vec = pl.BlockSpec((None, 1, D_CONV), lambda b, i: (layer, 0, 0))
    return pl.pallas_call(
        functools.partial(_convbranch_kernel, tl=tl),
        grid=(bsz, nl),
        in_specs=[pl.BlockSpec((tl, D_CONV), lambda b, i: (b * nl + i, 0)),
                  pl.BlockSpec((None, CONV_WIDTH, D_CONV), lambda b, i: (layer, 0, 0)),
                  vec, vec, vec,
                  pl.BlockSpec((None, D_CONV, D_MODEL), lambda b, i: (layer, 0, 0)),
                  pl.BlockSpec((tl, D_MODEL), lambda b, i: (b * nl + i, 0))],
        out_specs=pl.BlockSpec((tl, D_MODEL), lambda b, i: (b * nl + i, 0)),
        out_shape=jax.ShapeDtypeStruct((t, D_MODEL), BF16),
        scratch_shapes=[pltpu.VMEM((tl + CONV_HALO, D_CONV), F32),
                        pltpu.VMEM((7, tl + CONV_HALO - 8, D_CONV), F32),
                        pltpu.VMEM((tl, D_CONV), F32),
                        pltpu.VMEM((D_CONV, D_MODEL), BF16)],
        compiler_params=_params(2),
        name="conv_module",
    )(c, conv_w, conv_b3, ln_g3, ln_b3, w_conv_out, gates)


def _expand_heads(v, g, lane_in_pair):
    rows = v.shape[0]
    b = [jnp.broadcast_to(v[:, g * HEADS_PER_GROUP + r:g * HEADS_PER_GROUP + r + 1], (rows, LANES))
         for r in range(HEADS_PER_GROUP)]
    first = lane_in_pair[:rows] < SSM_HEAD_DIM
    return jnp.concatenate([jnp.where(first, b[0], b[1]), jnp.where(first, b[2], b[3])], axis=1)


def _ssd_kernel(xbc_ref, u_ref, zs_ref, wdt_ref, dtb_ref, cw_ref, cb_ref, alog_ref, dskip_ref, ng_ref,
                o_ref, ext_ref, sh_ref, act_ref, state_ref, *, q):
    @pl.when(pl.program_id(1) == 0)
    def _():
        ext_ref[0:SSM_HALO, :] = jnp.zeros((SSM_HALO, D_XBC), F32)
        state_ref[...] = jnp.zeros(state_ref.shape, F32)

    @pl.when(pl.program_id(1) > 0)
    def _():
        ext_ref[0:SSM_HALO, :] = ext_ref[q:q + SSM_HALO, :]

    ext_ref[SSM_HALO:SSM_HALO + q, :] = xbc_ref[...].astype(F32)

    base = SSM_HALO - (SSM_CONV_WIDTH - 1)
    piece = sh_ref.shape[1]
    for p0 in range(0, q, piece):
        for k in range(SSM_CONV_WIDTH - 1):
            sh_ref[k] = ext_ref[base + k + p0:base + k + p0 + piece, :]
        for lo in range(0, D_XBC, CONV_CW):
            for r0 in range(0, piece, CONV_RC):
                cols = slice(lo, lo + CONV_CW)
                top = SSM_HALO + p0 + r0
                acc = cb_ref[:, cols] + cw_ref[SSM_CONV_WIDTH - 1:SSM_CONV_WIDTH, cols] * \
                    ext_ref[top:top + CONV_RC, cols]
                for k in range(SSM_CONV_WIDTH - 1):
                    acc = acc + cw_ref[k:k + 1, cols] * sh_ref[k, r0:r0 + CONV_RC, cols]
                act_ref[p0 + r0:p0 + r0 + CONV_RC, cols] = acc * _sigmoid(acc)

    dt_raw = jnp.dot(u_ref[...], wdt_ref[...], preferred_element_type=F32) + dtb_ref[...]
    dt = jnp.maximum(dt_raw, 0.0) + jnp.log1p(jnp.exp(-jnp.abs(dt_raw)))
    for r0 in range(0, q, SSD_Q):
        _scan_chunk(slice(r0, r0 + SSD_Q), dt[r0:r0 + SSD_Q, :], act_ref, zs_ref, alog_ref, dskip_ref,
                    ng_ref, o_ref, state_ref)


def _scan_chunk(rows, dt, act_ref, zs_ref, alog_ref, dskip_ref, ng_ref, o_ref, state_ref):
    q = SSD_Q
    adt = dt * (-jnp.exp(alog_ref[...]))
    row = lax.broadcasted_iota(I32, (q, q), 0)
    col = lax.broadcasted_iota(I32, (q, q), 1)
    causal = row >= col
    tril = jnp.where(causal, 1.0, 0.0).astype(F32)
    acs = jnp.dot(tril, adt, preferred_element_type=F32, precision=lax.Precision.HIGHEST)
    acs_t = acs.T
    dt_t = dt.T
    last = acs[q - 1:q, :]
    exp_acs = jnp.exp(acs)
    dt_decay = dt * jnp.exp(last - acs)
    chunk_decay = jnp.exp(last)
    lane_in_pair = lax.broadcasted_iota(I32, (q, LANES), 1)
    head_of_lane = lax.broadcasted_iota(I32, (1, GROUP_CH), 1) // SSM_HEAD_DIM

    b_off = D_INNER
    c_off = D_INNER + SSM_GROUPS * D_STATE
    for g in range(SSM_GROUPS):
        ch = slice(g * GROUP_CH, (g + 1) * GROUP_CH)
        xg = act_ref[rows, ch]
        xg_bf = xg.astype(BF16)
        bg = act_ref[rows, b_off + g * D_STATE:b_off + (g + 1) * D_STATE]
        cg = act_ref[rows, c_off + g * D_STATE:c_off + (g + 1) * D_STATE].astype(BF16)
        cb = lax.dot_general(cg, bg.astype(BF16), (((1,), (1,)), ((), ())),
                             preferred_element_type=F32)
        ms, xblocks = [], []
        for r in range(HEADS_PER_GROUP):
            h = g * HEADS_PER_GROUP + r
            lmat = jnp.exp(jnp.where(causal, acs[:, h:h + 1] - acs_t[h:h + 1, :], -jnp.inf))
            ms.append((cb * lmat * dt_t[h:h + 1, :]).astype(BF16))
            head_mask = jnp.where(head_of_lane == r, 1.0, 0.0).astype(BF16)
            xblocks.append(xg_bf * head_mask)
        y_diag = jnp.dot(jnp.concatenate(ms, axis=1), jnp.concatenate(xblocks, axis=0),
                         preferred_element_type=F32)
        s_prev = state_ref[g]
        y_off = jnp.dot(cg, s_prev.astype(BF16), preferred_element_type=F32)
        yg = y_diag + y_off * _expand_heads(exp_acs, g, lane_in_pair) + xg * dskip_ref[:, ch]
        xw = (xg * _expand_heads(dt_decay, g, lane_in_pair)).astype(BF16)
        dec = _expand_heads(chunk_decay, g, lane_in_pair)
        state_ref[g] = s_prev * dec + jnp.dot(bg.T.astype(BF16), xw, preferred_element_type=F32)

        yz = yg * zs_ref[rows, g * GROUP_CH:(g + 1) * GROUP_CH].astype(F32)
        ms = jnp.mean(yz * yz, axis=-1, keepdims=True)
        yn = yz * lax.rsqrt(ms + RMS_EPS) * ng_ref[:, g * GROUP_CH:(g + 1) * GROUP_CH]
        o_ref[rows, g * GROUP_CH:(g + 1) * GROUP_CH] = yn.astype(o_ref.dtype)


def _ssd(xbc, u, zs, w_dt, dtb3, ssm_conv_w, ssm_conv_b3, alog3, dskip3, ng3, layer, bsz, seqlen):
    t = xbc.shape[0]
    q = min(SSD_STEP, seqlen)
    nq = seqlen // q
    tile = lambda width: pl.BlockSpec((q, width), lambda b, i: (b * nq + i, 0))
    vec = lambda width: pl.BlockSpec((None, 1, width), lambda b, i: (layer, 0, 0))
    return pl.pallas_call(
        functools.partial(_ssd_kernel, q=q),
        grid=(bsz, nq),
        in_specs=[tile(D_XBC), tile(D_MODEL), tile(D_INNER),
                  pl.BlockSpec((None, D_MODEL, LANES), lambda b, i: (layer, 0, 0)), vec(LANES),
                  pl.BlockSpec((None, SSM_CONV_WIDTH, D_XBC), lambda b, i: (layer, 0, 0)),
                  vec(D_XBC), vec(LANES), vec(D_INNER), vec(D_INNER)],
        out_specs=tile(D_INNER),
        out_shape=jax.ShapeDtypeStruct((t, D_INNER), BF16),
        scratch_shapes=[pltpu.VMEM((q + SSM_HALO, D_XBC), F32),
                        pltpu.VMEM((SSM_CONV_WIDTH - 1, min(SSD_CONV_ROWS, q), D_XBC), F32),
                        pltpu.VMEM((q, D_XBC), F32),
                        pltpu.VMEM((SSM_GROUPS, D_STATE, GROUP_CH), F32)],
        compiler_params=_params(2, LARGE_VMEM_LIMIT),
        name="ssd_mixer",
    )(xbc, u, zs, w_dt, dtb3, ssm_conv_w, ssm_conv_b3, alog3, dskip3, ng3)


def _first_argmax(vals):
    best, idx = vals[0], jnp.zeros(vals[0].shape, I32)
    for j in range(1, len(vals)):
        gt = vals[j] > best
        idx = jnp.where(gt, j, idx)
        best = jnp.where(gt, vals[j], best)
    return idx, best


def _select(idx, vals):
    out = vals[len(vals) - 1]
    for j in range(len(vals) - 2, -1, -1):
        out = jnp.where(idx == j, vals[j], out)
    return out


def _outproj_kernel(yn_ref, y1_ref, gs_ref, x_ref, wso_ref, wo_ref, lg_ref, lb_ref, wr_ref, rb_ref,
                    x1_ref, x1b_ref, pos_ref, wts_ref, cnt_ref, seg_ref,
                    wso_bf, wo_bf, base_ref, *, tm):
    @pl.when(pl.program_id(0) == 0)
    def _():
        wso_bf[...] = wso_ref[...].astype(BF16)
        wo_bf[...] = wo_ref[...].astype(BF16)
        base_ref[...] = jnp.zeros(base_ref.shape, F32)

    y_ssm = jnp.dot(yn_ref[...], wso_bf[...], preferred_element_type=F32)
    merged = y1_ref[...].astype(F32) + gs_ref[...].astype(F32) * y_ssm
    mix = jnp.dot(merged.astype(BF16), wo_bf[...], preferred_element_type=F32)
    x1 = _layer_norm(ALPHA * x_ref[...] + mix, lg_ref[...], lb_ref[...])
    x1_ref[...] = x1
    x1b_ref[...] = x1.astype(BF16)

    logits = lax.dot_general(wr_ref[...], x1, (((1,), (1,)), ((), ())),
                             preferred_element_type=F32, precision=lax.Precision.HIGHEST)
    scores = _sigmoid(logits)
    sel = scores + rb_ref[...]
    sel_rows = [sel[e:e + 1, :] for e in range(N_EXPERTS)]
    sc_rows = [scores[e:e + 1, :] for e in range(N_EXPERTS)]

    gscores = []
    for gi in range(N_EXPERT_GROUPS):
        v = sel_rows[gi * EXPERTS_PER_GROUP:(gi + 1) * EXPERTS_PER_GROUP]
        best = None
        for a in range(EXPERTS_PER_GROUP):
            for b in range(a + 1, EXPERTS_PER_GROUP):
                s = v[a] + v[b]
                best = s if best is None else jnp.maximum(best, s)
        gscores.append(best)
    grp, _ = _first_argmax(gscores)

    sel_in = [_select(grp, [sel_rows[gi * EXPERTS_PER_GROUP + j] for gi in range(N_EXPERT_GROUPS)])
              for j in range(EXPERTS_PER_GROUP)]
    sc_in = [_select(grp, [sc_rows[gi * EXPERTS_PER_GROUP + j] for gi in range(N_EXPERT_GROUPS)])
             for j in range(EXPERTS_PER_GROUP)]
    i1, _ = _first_argmax(sel_in)
    neg = jnp.full(sel_in[0].shape, -jnp.inf, F32)
    i2, _ = _first_argmax([jnp.where(i1 == j, neg, sel_in[j]) for j in range(EXPERTS_PER_GROUP)])
    s1 = _select(i1, sc_in)
    s2 = _select(i2, sc_in)
    tot = s1 + s2
    e1 = grp * EXPERTS_PER_GROUP + i1
    e2 = grp * EXPERTS_PER_GROUP + i2
    wts_ref[0:1, :] = s1 / tot
    wts_ref[1:2, :] = s2 / tot

    eio = lax.broadcasted_iota(I32, (N_EXPERTS, tm), 0)
    oh1 = jnp.where(eio == e1, 1.0, 0.0).astype(F32)
    oh2 = jnp.where(eio == e2, 1.0, 0.0).astype(F32)
    both = oh1 + oh2
    srow = lax.broadcasted_iota(I32, (tm, tm), 0)
    scol = lax.broadcasted_iota(I32, (tm, tm), 1)
    before = jnp.where(srow < scol, 1.0, 0.0).astype(BF16)
    cum = jnp.dot(both.astype(BF16), before, preferred_element_type=F32)
    cnt8 = jnp.floor((jnp.sum(both, axis=1, keepdims=True) + 7.0) * 0.125) * 8.0
    cnt8_l = jnp.broadcast_to(cnt8, (N_EXPERTS, LANES))
    erow = lax.broadcasted_iota(I32, (N_EXPERTS, N_EXPERTS), 0)
    ecol = lax.broadcasted_iota(I32, (N_EXPERTS, N_EXPERTS), 1)
    seg_off = jnp.dot(jnp.where(ecol < erow, 1.0, 0.0).astype(F32), cnt8_l,
                      preferred_element_type=F32, precision=lax.Precision.HIGHEST)
    base = base_ref[...]
    in_tile = seg_off[:, 0:1] + cum
    pos_ref[0:1, :] = jnp.sum(oh1 * in_tile, axis=0, keepdims=True).astype(I32)
    pos_ref[1:2, :] = jnp.sum(oh2 * in_tile, axis=0, keepdims=True).astype(I32)
    lane = lax.broadcasted_iota(I32, (N_EXPERTS, LANES), 1)
    seg_ref[...] = jnp.where(lane == 0, seg_off, jnp.where(lane == 1, cnt8_l, base)).astype(I32)
    base_ref[...] = base + cnt8_l
    cnt_ref[...] = base_ref[...].astype(I32)


def _outproj(yn, y1g, gates, x, w_ssm_out, w_out, ln_g3, ln_b3, wr_t, rbias, layer, tm):
    t = x.shape[0]
    tile = lambda width: pl.BlockSpec((tm, width), lambda m: (m, 0))
    vec = pl.BlockSpec((None, 1, D_MODEL), lambda m: (layer, 0, 0))
    pair = pl.BlockSpec((2, tm), lambda m: (0, m))
    return pl.pallas_call(
        functools.partial(_outproj_kernel, tm=tm),
        grid=(t // tm,),
        in_specs=[tile(D_INNER), tile(D_MODEL),
                  pl.BlockSpec((tm, D_MODEL), lambda m: (m, 1)),
                  tile(D_MODEL),
                  pl.BlockSpec((None, D_INNER, D_MODEL), lambda m: (layer, 0, 0)),
                  pl.BlockSpec((None, D_MODEL, D_MODEL), lambda m: (layer, 0, 0)),
                  vec, vec,
                  pl.BlockSpec((N_EXPERTS, D_MODEL), lambda m: (0, 0)),
                  pl.BlockSpec((N_EXPERTS, 1), lambda m: (0, 0))],
        out_specs=[tile(D_MODEL), tile(D_MODEL), pair, pair,
                   pl.BlockSpec((N_EXPERTS, LANES), lambda m: (0, 0)),
                   pl.BlockSpec((None, N_EXPERTS, LANES), lambda m: (m, 0, 0))],
        out_shape=[jax.ShapeDtypeStruct((t, D_MODEL), F32),
                   jax.ShapeDtypeStruct((t, D_MODEL), BF16),
                   jax.ShapeDtypeStruct((2, t), I32),
                   jax.ShapeDtypeStruct((2, t), F32),
                   jax.ShapeDtypeStruct((N_EXPERTS, LANES), I32),
                   jax.ShapeDtypeStruct((t // tm, N_EXPERTS, LANES), I32)],
        scratch_shapes=[pltpu.VMEM((D_INNER, D_MODEL), BF16),
                        pltpu.VMEM((D_MODEL, D_MODEL), BF16),
                        pltpu.VMEM((N_EXPERTS, LANES), F32)],
        compiler_params=_params(1),
        name="outproj_ln_router",
    )(yn, y1g, gates, x, w_ssm_out, w_out, ln_g3, ln_b3, wr_t, rbias)


def _segment_blocks(tile, segoff_ref, cnt8_ref, gdst_ref, make_copy, act):
    for e in range(N_EXPERTS):
        idx = tile * N_EXPERTS + e
        n8 = lax.shift_right_logical(cnt8_ref[idx], 3)
        local = segoff_ref[idx]
        glob = gdst_ref[idx]
        for bit in SEGMENT_BITS:
            rows = 8 * bit
            hit = (n8 & bit) != 0

            @pl.when(hit)
            def _(local=local, glob=glob, rows=rows):
                act(make_copy(pl.multiple_of(local, 8), pl.multiple_of(glob, 8), rows))

            step = jnp.where(hit, rows, 0)
            local = local + step
            glob = glob + step


def _dispatch_kernel(fill_ref, segoff_ref, cnt8_ref, gdst_ref, pos_ref, x_ref, xs_hbm,
                     sorted_ref, zeros_ref, sem, blk_sem, seg_sems, *, tm, n_blk, n_tiles):
    m = pl.program_id(0)
    slot = lax.rem(m, 2)

    def zero_rows(row):
        return pltpu.make_async_copy(zeros_ref.at[pl.ds(0, 8)],
                                     xs_hbm.at[pl.ds(pl.multiple_of(row, 8), 8)], sem)

    def zero_block(b):
        return pltpu.make_async_copy(
            zeros_ref, xs_hbm.at[pl.ds(pl.multiple_of(b * MOE_ROWS, MOE_ROWS), MOE_ROWS)], blk_sem)

    @pl.when(m == 0)
    def _():
        zeros_ref[...] = jnp.zeros(zeros_ref.shape, F32)
        n_active = fill_ref[2 * N_EXPERTS]
        for e in range(N_EXPERTS):
            pad_start = fill_ref[e]
            lax.fori_loop(0, fill_ref[N_EXPERTS + e],
                          lambda j, c: (zero_rows(pad_start + 8 * j).start(), c)[1], 0)
        lax.fori_loop(n_active, n_blk, lambda b, c: (zero_block(b).start(), c)[1], 0)
        lax.fori_loop(0, fill_ref[2 * N_EXPERTS + 1], lambda j, c: (zero_rows(0).wait(), c)[1], 0)
        lax.fori_loop(n_active, n_blk, lambda b, c: (zero_block(b).wait(), c)[1], 0)

    def for_each_segment_block(tile, s, act):
        copy = lambda local, glob, rows: pltpu.make_async_copy(
            sorted_ref.at[s, pl.ds(local, rows)], xs_hbm.at[pl.ds(glob, rows)], seg_sems.at[s])
        _segment_blocks(tile, segoff_ref, cnt8_ref, gdst_ref, copy, act)

    start = lambda copy: copy.start()
    wait = lambda copy: copy.wait()

    @pl.when(m >= 2)
    def _():
        for_each_segment_block(m - 2, slot, wait)

    pos = pos_ref[...]
    srow = lax.broadcasted_iota(I32, (sorted_ref.shape[1], tm), 0)
    onehot = jnp.where(srow == pos[0:1, :], 1.0, jnp.where(srow == pos[1:2, :], 1.0, 0.0)).astype(BF16)
    sorted_ref[slot] = jnp.dot(onehot, x_ref[...], preferred_element_type=F32)
    for_each_segment_block(m, slot, start)

    @pl.when(m == n_tiles - 1)
    def _():
        if n_tiles > 1:
            for_each_segment_block(m - 1, 1 - slot, wait)
        for_each_segment_block(m, slot, wait)


def _dispatch(fill, seg_off, seg_cnt8, seg_dst, pos, x1b, n_blk, tm):
    t = x1b.shape[0]
    n_tiles = t // tm
    sorted_rows = 2 * tm + 8 * N_EXPERTS
    grid_spec = pltpu.PrefetchScalarGridSpec(
        num_scalar_prefetch=4,
        grid=(n_tiles,),
        in_specs=[pl.BlockSpec((2, tm), lambda m, *_: (0, m)),
                  pl.BlockSpec((tm, D_MODEL), lambda m, *_: (m, 0))],
        out_specs=pl.BlockSpec(memory_space=pl.ANY),
        scratch_shapes=[pltpu.VMEM((2, sorted_rows, D_MODEL), F32),
                        pltpu.VMEM((MOE_ROWS, D_MODEL), F32),
                        pltpu.SemaphoreType.DMA(()), pltpu.SemaphoreType.DMA(()),
                        pltpu.SemaphoreType.DMA((2,))],
    )
    return pl.pallas_call(
        functools.partial(_dispatch_kernel, tm=tm, n_blk=n_blk, n_tiles=n_tiles),
        grid_spec=grid_spec,
        out_shape=jax.ShapeDtypeStruct((n_blk * MOE_ROWS, D_MODEL), F32),
        compiler_params=_params(1),
        name="moe_dispatch",
    )(fill, seg_off, seg_cnt8, seg_dst, pos, x1b)


def _expert_kernel(src_ref, exp_ref, nvalid_ref, xs_ref, wg_ref, wu_ref, wd_ref, y_ref,
                   wgu_bf, wd_bf):
    i = pl.program_id(0)
    nvalid = nvalid_ref[i]
    changed = (i == 0) | (exp_ref[i] != exp_ref[jnp.maximum(i - 1, 0)])

    @pl.when((nvalid > 0) & changed)
    def _():
        wgu_bf[:, :D_EXPERT] = wg_ref[...].astype(BF16)
        wgu_bf[:, D_EXPERT:] = wu_ref[...].astype(BF16)
        wd_bf[...] = wd_ref[...].astype(BF16)

    @pl.when(nvalid > 0)
    def _():
        rows = lax.broadcasted_iota(I32, (MOE_ROWS, 1), 0)
        x = jnp.where(rows < nvalid, xs_ref[...], 0.0).astype(BF16)
        gu = jnp.dot(x, wgu_bf[...], preferred_element_type=F32)
        hg = gu[:, :D_EXPERT]
        h = hg * _sigmoid(hg) * gu[:, D_EXPERT:]
        y_ref[...] = jnp.dot(h.astype(BF16), wd_bf[...], preferred_element_type=F32)

    @pl.when(nvalid == 0)
    def _():
        y_ref[...] = jnp.zeros(y_ref.shape, F32)


def _experts(blk_src, blk_exp, blk_nvalid, xs, wg, wu, wd, layer):
    n_rows = xs.shape[0]
    n_blk = n_rows // MOE_ROWS
    grid_spec = pltpu.PrefetchScalarGridSpec(
        num_scalar_prefetch=3,
        grid=(n_blk,),
        in_specs=[pl.BlockSpec((MOE_ROWS, D_MODEL), lambda i, src, exp, nv: (src[i], 0)),
                  pl.BlockSpec((None, None, D_MODEL, D_EXPERT),
                               lambda i, src, exp, nv: (layer, exp[i], 0, 0)),
                  pl.BlockSpec((None, None, D_MODEL, D_EXPERT),
                               lambda i, src, exp, nv: (layer, exp[i], 0, 0)),
                  pl.BlockSpec((None, None, D_EXPERT, D_MODEL),
                               lambda i, src, exp, nv: (layer, exp[i], 0, 0))],
        out_specs=pl.BlockSpec((MOE_ROWS, D_MODEL), lambda i, src, exp, nv: (i, 0)),
        scratch_shapes=[pltpu.VMEM((D_MODEL, 2 * D_EXPERT), BF16),
                        pltpu.VMEM((D_EXPERT, D_MODEL), BF16)],
    )
    return pl.pallas_call(
        _expert_kernel,
        grid_spec=grid_spec,
        out_shape=jax.ShapeDtypeStruct((n_rows, D_MODEL), F32),
        compiler_params=_params(1),
        name="moe_experts",
    )(blk_src, blk_exp, blk_nvalid, xs, wg, wu, wd)


def _moe_plan(counts, seg, t):
    cnt = counts[:, 0]
    nblk_e = (cnt + MOE_ROWS - 1) // MOE_ROWS
    blk_end = jnp.cumsum(nblk_e)
    blk_start = blk_end - nblk_e
    n_active = blk_end[N_EXPERTS - 1]
    experts = jnp.arange(N_EXPERTS, dtype=I32)
    row_start = blk_start * MOE_ROWS
    n_tiles = seg.shape[0]
    n_blk = -(-(2 * t + 8 * N_EXPERTS * n_tiles) // MOE_ROWS) + N_EXPERTS
    ids = jnp.arange(n_blk, dtype=I32)
    src = jnp.minimum(ids, n_active - 1)
    exp = jnp.minimum(jnp.sum((blk_end[None, :] <= src[:, None]).astype(I32), axis=1), N_EXPERTS - 1)
    onehot = (exp[:, None] == experts[None, :]).astype(I32)
    pick = lambda table: jnp.sum(onehot * table[None, :], axis=1)
    left = pick(cnt) - (src - pick(blk_start)) * MOE_ROWS
    nvalid = jnp.where(ids < n_active, jnp.clip(left, 0, MOE_ROWS), 0)
    pad8 = (nblk_e * MOE_ROWS - cnt) // 8
    fill = jnp.concatenate([row_start + cnt, pad8, jnp.stack([n_active, jnp.sum(pad8)])]).astype(I32)
    seg_off = seg[:, :, 0].reshape(-1)
    seg_cnt8 = seg[:, :, 1].reshape(-1)
    seg_dst = (row_start[None, :] + seg[:, :, 2]).reshape(-1)
    return src.astype(I32), exp, nvalid.astype(I32), fill, seg_off, seg_cnt8, seg_dst, n_blk


def _final_kernel(segoff_ref, cnt8_ref, gsrc_ref, x1_ref, x1b_ref, p_ref, wt_ref, pos_ref, wpu_ref,
                  wpg_ref, bpg_ref, lg_ref, lb_ref, y_hbm, x2_ref, x2b_ref, ys_ref, wpu_bf, wpg_bf, sems,
                  *, tm, n_tiles):
    m = pl.program_id(0)
    slot = lax.rem(m, 2)

    def for_each_segment_block(tile, s, act):
        copy = lambda local, glob, rows: pltpu.make_async_copy(
            y_hbm.at[pl.ds(glob, rows)], ys_ref.at[s, pl.ds(local, rows)], sems.at[s])
        _segment_blocks(tile, segoff_ref, cnt8_ref, gsrc_ref, copy, act)

    @pl.when(m == 0)
    def _():
        wpu_bf[...] = wpu_ref[...].astype(BF16)
        wpg_bf[...] = wpg_ref[...].astype(BF16)
        for_each_segment_block(0, 0, lambda copy: copy.start())

    @pl.when(m + 1 < n_tiles)
    def _():
        for_each_segment_block(m + 1, 1 - slot, lambda copy: copy.start())

    up = jnp.dot(p_ref[...].astype(BF16), wpu_bf[...], preferred_element_type=F32)
    gate = _sigmoid(jnp.dot(x1b_ref[...], wpg_bf[...], preferred_element_type=F32) + bpg_ref[...])
    resid = ALPHA * x1_ref[...] + up * gate

    for_each_segment_block(m, slot, lambda copy: copy.wait())

    n_sorted = ys_ref.shape[1]
    used = segoff_ref[m * N_EXPERTS + N_EXPERTS - 1] + cnt8_ref[m * N_EXPERTS + N_EXPERTS - 1]
    srow = lax.broadcasted_iota(I32, (n_sorted, 1), 0)
    ys = jnp.where(srow < used, ys_ref[slot], 0.0).astype(BF16)
    pos = pos_ref[...]
    scol = lax.broadcasted_iota(I32, (tm, n_sorted), 1)
    wt = wt_ref[...]
    moe = jnp.zeros((tm, D_MODEL), F32)
    for k in range(TOP_K):
        pick = jnp.where(scol == pos[:, k:k + 1], 1.0, 0.0).astype(BF16)
        moe = moe + wt[:, k:k + 1] * jnp.dot(pick, ys, preferred_element_type=F32)
    x2 = _layer_norm(resid + moe, lg_ref[...], lb_ref[...])
    x2_ref[...] = x2
    x2b_ref[...] = x2.astype(BF16)


def _final(seg_off, seg_cnt8, seg_src, x1, x1b, p, wt_tok, pos_tok, w_ple_up, w_ple_gate, b_pg3,
           ln_g3, ln_b3, y_rows, layer, tm):
    t = x1.shape[0]
    n_tiles = t // tm
    sorted_rows = 2 * tm + 8 * N_EXPERTS
    tile = lambda width: pl.BlockSpec((tm, width), lambda m, *_: (m, 0))
    vec = pl.BlockSpec((None, 1, D_MODEL), lambda m, *_: (layer, 0, 0))
    grid_spec = pltpu.PrefetchScalarGridSpec(
        num_scalar_prefetch=3,
        grid=(n_tiles,),
        in_specs=[tile(D_MODEL), tile(D_MODEL),
                  pl.BlockSpec((None, tm, PLE_DIM), lambda m, *_: (layer, m, 0)),
                  tile(2), tile(2),
                  pl.BlockSpec((None, PLE_DIM, D_MODEL), lambda m, *_: (layer, 0, 0)),
                  pl.BlockSpec((None, D_MODEL, D_MODEL), lambda m, *_: (layer, 0, 0)),
                  vec, vec, vec,
                  pl.BlockSpec(memory_space=pl.ANY)],
        out_specs=[tile(D_MODEL), tile(D_MODEL)],
        scratch_shapes=[pltpu.VMEM((2, sorted_rows, D_MODEL), F32),
                        pltpu.VMEM((PLE_DIM, D_MODEL), BF16), pltpu.VMEM((D_MODEL, D_MODEL), BF16),
                        pltpu.SemaphoreType.DMA((2,))],
    )
    return pl.pallas_call(
        functools.partial(_final_kernel, tm=tm, n_tiles=n_tiles),
        grid_spec=grid_spec,
        out_shape=[jax.ShapeDtypeStruct((t, D_MODEL), F32),
                   jax.ShapeDtypeStruct((t, D_MODEL), BF16)],
        compiler_params=_params(1),
        name="combine_ple_ln",
    )(seg_off, seg_cnt8, seg_src, x1, x1b, p, wt_tok, pos_tok, w_ple_up, w_ple_gate, b_pg3,
      ln_g3, ln_b3, y_rows)


def kernel(x, p, w_in, b_glu, b_branch_gate, conv_w, conv_b, conv_ln_g, conv_ln_b, w_conv_out,
           ssm_conv_w, ssm_conv_b, dt_bias, a_log, d_skip, ssm_norm_g, w_ssm_out, w_out,
           ln1_g, ln1_b, w_router, router_bias, w_exp_gate, w_exp_up, w_exp_down,
           w_ple_up, w_ple_gate, b_ple_gate, ln2_g, ln2_b):
    bsz, seqlen, d = x.shape
    depth = w_in.shape[0]
    t = bsz * seqlen
    tm_moe = min(MOE_TILE, t)

    row3 = lambda a: a.reshape(a.shape[0], 1, a.shape[1])
    pad_lanes = lambda a: jnp.pad(a, ((0, 0), (0, LANES - a.shape[1])))
    b_glu3, b_gate3 = row3(b_glu), row3(b_branch_gate)
    conv_b3, cln_g3, cln_b3 = row3(conv_b), row3(conv_ln_g), row3(conv_ln_b)
    ssm_conv_b3, ng3 = row3(ssm_conv_b), row3(ssm_norm_g)
    dtb3, alog3 = row3(pad_lanes(dt_bias)), row3(pad_lanes(a_log))
    dskip3 = row3(jnp.repeat(d_skip, SSM_HEAD_DIM, axis=1))
    ln1_g3, ln1_b3, ln2_g3, ln2_b3 = row3(ln1_g), row3(ln1_b), row3(ln2_g), row3(ln2_b)
    b_pg3 = row3(b_ple_gate)
    w_in_t = jnp.swapaxes(w_in, 1, 2)
    w_gate, w_dt = _tail_weights(w_in_t)
    wr_t = w_router.T
    rbias = router_bias.reshape(N_EXPERTS, 1)
    p2 = p.reshape(depth, t, PLE_DIM)

    xf = x.reshape(t, d)
    xb = xf.astype(BF16)
    for i in range(depth):
        c = _glu(xb, w_in_t, b_glu3, i)
        zs = _matmul(xb, w_in_t, lambda n: (i, OFF_Z // 1024 + n, 0), None, None,
                     D_INNER, 1024, "silu", BF16, w_rows_are_outputs=True)
        xbc = _matmul(xb, w_in_t, lambda n: (i, OFF_XBC // 1024 + n, 0), None, None,
                      D_XBC, 1024, "none", BF16, w_rows_are_outputs=True)
        gates = _matmul(xb, w_gate, lambda n: (i, 0, n), b_gate3, lambda n: (i, 0, n),
                        2 * D_MODEL, 1024, "sigmoid", BF16)
        y1g = _convbranch(c, gates, conv_w, conv_b3, cln_g3, cln_b3, w_conv_out, i, bsz, seqlen)
        yn = _ssd(xbc, xb, zs, w_dt, dtb3, ssm_conv_w, ssm_conv_b3, alog3, dskip3, ng3, i, bsz, seqlen)
        x1, x1b, pos, wts, counts, seg = _outproj(
            yn, y1g, gates, xf, w_ssm_out, w_out, ln1_g3, ln1_b3, wr_t, rbias, i, tm_moe)
        (blk_src, blk_exp, blk_nvalid, fill, seg_off, seg_cnt8, seg_row,
         n_blk) = _moe_plan(counts, seg, t)
        xs = _dispatch(fill, seg_off, seg_cnt8, seg_row, pos, x1b, n_blk, tm_moe)
        y_rows = _experts(blk_src, blk_exp, blk_nvalid, xs, w_exp_gate, w_exp_up, w_exp_down, i)
        xf, xb = _final(seg_off, seg_cnt8, seg_row, x1, x1b, p2, wts.T, pos.T,
                        w_ple_up, w_ple_gate, b_pg3, ln2_g3, ln2_b3, y_rows, i, tm_moe)
    return xf.reshape(bsz, seqlen, d)
```

```python
import functools

import jax
import jax.numpy as jnp
from jax import lax
from jax.experimental import pallas as pl
from jax.experimental.pallas import tpu as pltpu

F32 = jnp.float32
BF16 = jnp.bfloat16
I32 = jnp.int32

D_MODEL = 1024
D_CONV = 1024
CONV_WIDTH = 31
D_INNER = 2048
SSM_HEAD_DIM = 64
SSM_HEADS = 32
SSM_GROUPS = 8
HEADS_PER_GROUP = 4
D_STATE = 128
SSM_CONV_WIDTH = 4
D_XBC = D_INNER + 2 * SSM_GROUPS * D_STATE
GROUP_CH = HEADS_PER_GROUP * SSM_HEAD_DIM
N_EXPERTS = 16
N_EXPERT_GROUPS = 4
TOP_K = 2
EXPERTS_PER_GROUP = 4
D_EXPERT = 512
PLE_DIM = 256
DEPTH = 4
ALPHA = (2.0 * DEPTH) ** 0.25
LN_EPS = 1e-5
RMS_EPS = 1e-5

LANES = 128
CONV_HALO = 32
CONV_TILE = 512
CONV_RC = 64
CONV_CW = 256
SSM_HALO = 8
SSD_Q = 128
SSD_STEP = 512
SSD_CONV_ROWS = 256
MOE_ROWS = 512
MOE_TILE = 512
SEGMENT_BITS = (64, 32, 16, 8, 4, 2, 1)
VMEM_LIMIT = 48 * 1024 * 1024
INPROJ_TM = 2048
LARGE_VMEM_LIMIT = 56 * 1024 * 1024

OFF_Z = 2 * D_CONV
OFF_XBC = OFF_Z + D_INNER
OFF_DT = OFF_XBC + D_XBC


def _sigmoid(x):
    return 1.0 / (1.0 + jnp.exp(-x))


def _layer_norm(x, g, b):
    mu = jnp.mean(x, axis=-1, keepdims=True)
    xc = x - mu
    var = jnp.mean(xc * xc, axis=-1, keepdims=True)
    return xc * lax.rsqrt(var + LN_EPS) * g + b


def _params(n_axes, vmem_limit=VMEM_LIMIT):
    return pltpu.CompilerParams(dimension_semantics=("arbitrary",) * n_axes,
                                vmem_limit_bytes=vmem_limit)


def _mm_kernel(x_ref, w_ref, *rest, act, has_bias, w_rows_are_outputs):
    if has_bias:
        b_ref, o_ref, wbf_ref = rest
    else:
        o_ref, wbf_ref = rest

    @pl.when(pl.program_id(1) == 0)
    def _():
        w = w_ref[...]
        wbf_ref[...] = (w.T if w_rows_are_outputs else w).astype(BF16)

    acc = jnp.dot(x_ref[...], wbf_ref[...], preferred_element_type=F32)
    if has_bias:
        acc = acc + b_ref[...]
    if act == "sigmoid":
        acc = _sigmoid(acc)
    elif act == "silu":
        acc = acc * _sigmoid(acc)
    o_ref[...] = acc.astype(o_ref.dtype)


def _matmul(x, w, w_index, bias, b_index, n_out, tn, act, out_dtype, w_rows_are_outputs=False):
    t, k = x.shape
    tm = min(INPROJ_TM, t)
    w_block = (None,) * (w.ndim - 2) + ((tn, k) if w_rows_are_outputs else (k, tn))
    in_specs = [pl.BlockSpec((tm, k), lambda n, m: (m, 0)),
                pl.BlockSpec(w_block, lambda n, m: w_index(n))]
    args = [x, w]
    if bias is not None:
        b_block = (None,) * (bias.ndim - 2) + (1, tn)
        in_specs.append(pl.BlockSpec(b_block, lambda n, m: b_index(n)))
        args.append(bias)
    return pl.pallas_call(
        functools.partial(_mm_kernel, act=act, has_bias=bias is not None,
                          w_rows_are_outputs=w_rows_are_outputs),
        grid=(n_out // tn, t // tm),
        in_specs=in_specs,
        out_specs=pl.BlockSpec((tm, tn), lambda n, m: (m, n)),
        out_shape=jax.ShapeDtypeStruct((t, n_out), out_dtype),
        scratch_shapes=[pltpu.VMEM((k, tn), BF16)],
        compiler_params=_params(2, LARGE_VMEM_LIMIT),
        name="inproj_" + act,
    )(*args)


def _tail_weights_kernel(wt_hbm, wgate_ref, wdt_ref, buf_ref, sem):
    n_tail = buf_ref.shape[0]
    copy = pltpu.make_async_copy(wt_hbm.at[pl.program_id(0), pl.ds(OFF_DT, n_tail)], buf_ref, sem)
    copy.start()
    copy.wait()
    lane = lax.broadcasted_iota(I32, (D_MODEL, LANES), 1)
    wdt_ref[...] = jnp.where(lane < SSM_HEADS, buf_ref[0:LANES, :].T, 0.0).astype(BF16)
    step = 4 * LANES
    for lo in range(0, 2 * D_MODEL, step):
        rows = slice(SSM_HEADS + lo, SSM_HEADS + lo + step)
        wgate_ref[:, lo:lo + step] = buf_ref[rows, :].T.astype(BF16)


def _tail_weights(w_in_t):
    depth, n_all, k = w_in_t.shape
    n = 2 * D_MODEL
    return pl.pallas_call(
        _tail_weights_kernel,
        grid=(depth,),
        in_specs=[pl.BlockSpec(memory_space=pl.ANY)],
        out_specs=[pl.BlockSpec((None, k, n), lambda l: (l, 0, 0)),
                   pl.BlockSpec((None, k, LANES), lambda l: (l, 0, 0))],
        out_shape=[jax.ShapeDtypeStruct((depth, k, n), BF16),
                   jax.ShapeDtypeStruct((depth, k, LANES), BF16)],
        scratch_shapes=[pltpu.VMEM((n_all - OFF_DT, k), F32), pltpu.SemaphoreType.DMA(())],
        compiler_params=_params(1),
        name="tail_weights",
    )(w_in_t)


def _glu_kernel(x_ref, wa_ref, wg_ref, ba_ref, bg_ref, o_ref, wa_bf, wg_bf):
    @pl.when(pl.program_id(1) == 0)
    def _():
        wa_bf[...] = wa_ref[...].T.astype(BF16)
        wg_bf[...] = wg_ref[...].T.astype(BF16)

    x = x_ref[...]
    a = jnp.dot(x, wa_bf[...], preferred_element_type=F32) + ba_ref[...]
    g = jnp.dot(x, wg_bf[...], preferred_element_type=F32) + bg_ref[...]
    o_ref[...] = (a * _sigmoid(g)).astype(o_ref.dtype)


def _glu(x, w_in_t, b_glu3, layer, tn=512):
    t, k = x.shape
    tm = min(INPROJ_TM, t)
    half = D_CONV // tn
    return pl.pallas_call(
        _glu_kernel,
        grid=(half, t // tm),
        in_specs=[pl.BlockSpec((tm, k), lambda n, m: (m, 0)),
                  pl.BlockSpec((None, tn, k), lambda n, m: (layer, n, 0)),
                  pl.BlockSpec((None, tn, k), lambda n, m: (layer, n + half, 0)),
                  pl.BlockSpec((None, 1, tn), lambda n, m: (layer, 0, n)),
                  pl.BlockSpec((None, 1, tn), lambda n, m: (layer, 0, n + half))],
        out_specs=pl.BlockSpec((tm, tn), lambda n, m: (m, n)),
        out_shape=jax.ShapeDtypeStruct((t, D_CONV), BF16),
        scratch_shapes=[pltpu.VMEM((k, tn), BF16), pltpu.VMEM((k, tn), BF16)],
        compiler_params=_params(2, LARGE_VMEM_LIMIT),
        name="inproj_glu",
    )(x, w_in_t, w_in_t, b_glu3, b_glu3)


def _convbranch_kernel(c_ref, cw_ref, cb_ref, lg_ref, lb_ref, w_ref, gate_ref, o_ref,
                       ext_ref, sh_ref, conv_ref, wbf_ref, *, tl):
    first = (pl.program_id(0) == 0) & (pl.program_id(1) == 0)

    @pl.when(first)
    def _():
        wbf_ref[...] = w_ref[...].astype(BF16)

    @pl.when(pl.program_id(1) == 0)
    def _():
        ext_ref[0:CONV_HALO, :] = jnp.zeros((CONV_HALO, D_CONV), F32)

    @pl.when(pl.program_id(1) > 0)
    def _():
        ext_ref[0:CONV_HALO, :] = ext_ref[tl:tl + CONV_HALO, :]

    ext_ref[CONV_HALO:CONV_HALO + tl, :] = c_ref[...].astype(F32)

    sh_rows = tl + CONV_HALO - 8
    for s in range(1, 8):
        sh_ref[s - 1] = ext_ref[s:s + sh_rows, :]

    base = CONV_HALO - (CONV_WIDTH - 1)

    n_groups = CONV_RC // 8
    taps_of_shift = [[(a, 8 * a + s - base) for a in range(5) if 0 <= 8 * a + s - base < CONV_WIDTH]
                     for s in range(8)]

    def conv_rows(rc, carry):
        r0 = pl.multiple_of(rc * CONV_RC, CONV_RC)
        for lo in range(0, D_CONV, LANES):
            cols = slice(lo, lo + LANES)
            w = [jnp.broadcast_to(cw_ref[k:k + 1, cols], (8, LANES)) for k in range(CONV_WIDTH)]
            acc = [jnp.broadcast_to(cb_ref[:, cols], (8, LANES))] * n_groups
            for s in range(8):
                taps = taps_of_shift[s]
                for j in range(n_groups + max(a for a, _ in taps)):
                    used = [(a, k) for a, k in taps if 0 <= j - a < n_groups]
                    if not used:
                        continue
                    rows = pl.ds(r0 + 8 * j, 8)
                    x = ext_ref[rows, cols] if s == 0 else sh_ref[s - 1, rows, cols]
                    for a, k in used:
                        acc[j - a] = acc[j - a] + w[k] * x
            for i in range(n_groups):
                conv_ref[pl.ds(r0 + 8 * i, 8), cols] = acc[i]
        return carry

    lax.fori_loop(0, tl // CONV_RC, conv_rows, 0)
    h = _layer_norm(conv_ref[...], lg_ref[...], lb_ref[...])
    h = h * _sigmoid(h)
    y = jnp.dot(h.astype(BF16), wbf_ref[...], preferred_element_type=F32)
    o_ref[...] = (y * gate_ref[...].astype(F32)).astype(o_ref.dtype)


def _convbranch(c, gates, conv_w, conv_b3, ln_g3, ln_b3, w_conv_out, layer, bsz, seqlen):
    t = c.shape[0]
    tl = min(CONV_TILE, seqlen)
    nl = seqlen // tl
    vec = pl.BlockSpec((None, 1, D_CONV), lambda b, i: (layer, 0, 0))
    return pl.pallas_call(
        functools.partial(_convbranch_kernel, tl=tl),
        grid=(bsz, nl),
        in_specs=[pl.BlockSpec((tl, D_CONV), lambda b, i: (b * nl + i, 0)),
                  pl.BlockSpec((None, CONV_WIDTH, D_CONV), lambda b, i: (layer, 0, 0)),
                  vec, vec, vec,
                  pl.BlockSpec((None, D_CONV, D_MODEL), lambda b, i: (layer, 0, 0)),
                  pl.BlockSpec((tl, D_MODEL), lambda b, i: (b * nl + i, 0))],
        out_specs=pl.BlockSpec((tl, D_MODEL), lambda b, i: (b * nl + i, 0)),
        out_shape=jax.ShapeDtypeStruct((t, D_MODEL), BF16),
        scratch_shapes=[pltpu.VMEM((tl + CONV_HALO, D_CONV), F32),
                        pltpu.VMEM((7, tl + CONV_HALO - 8, D_CONV), F32),
                        pltpu.VMEM((tl, D_CONV), F32),
                        pltpu.VMEM((D_CONV, D_MODEL), BF16)],
        compiler_params=_params(2),
        name="conv_module",
    )(c, conv_w, conv_b3, ln_g3, ln_b3, w_conv_out, gates)


def _expand_heads(v, g, lane_in_pair):
    rows = v.shape[0]
    b = [jnp.broadcast_to(v[:, g * HEADS_PER_GROUP + r:g * HEADS_PER_GROUP + r + 1], (rows, LANES))
         for r in range(HEADS_PER_GROUP)]
    first = lane_in_pair[:rows] < SSM_HEAD_DIM
    return jnp.concatenate([jnp.where(first, b[0], b[1]), jnp.where(first, b[2], b[3])], axis=1)


def _ssd_kernel(xbc_ref, u_ref, zs_ref, wdt_ref, dtb_ref, cw_ref, cb_ref, alog_ref, dskip_ref, ng_ref,
                o_ref, ext_ref, sh_ref, act_ref, state_ref, *, q):
    @pl.when(pl.program_id(1) == 0)
    def _():
        ext_ref[0:SSM_HALO, :] = jnp.zeros((SSM_HALO, D_XBC), F32)
        state_ref[...] = jnp.zeros(state_ref.shape, F32)

    @pl.when(pl.program_id(1) > 0)
    def _():
        ext_ref[0:SSM_HALO, :] = ext_ref[q:q + SSM_HALO, :]

    ext_ref[SSM_HALO:SSM_HALO + q, :] = xbc_ref[...].astype(F32)

    base = SSM_HALO - (SSM_CONV_WIDTH - 1)
    piece = sh_ref.shape[1]
    for p0 in range(0, q, piece):
        for k in range(SSM_CONV_WIDTH - 1):
            sh_ref[k] = ext_ref[base + k + p0:base + k + p0 + piece, :]
        for lo in range(0, D_XBC, CONV_CW):
            for r0 in range(0, piece, CONV_RC):
                cols = slice(lo, lo + CONV_CW)
                top = SSM_HALO + p0 + r0
                acc = cb_ref[:, cols] + cw_ref[SSM_CONV_WIDTH - 1:SSM_CONV_WIDTH, cols] * \
                    ext_ref[top:top + CONV_RC, cols]
                for k in range(SSM_CONV_WIDTH - 1):
                    acc = acc + cw_ref[k:k + 1, cols] * sh_ref[k, r0:r0 + CONV_RC, cols]
                act_ref[p0 + r0:p0 + r0 + CONV_RC, cols] = acc * _sigmoid(acc)

    dt_raw = jnp.dot(u_ref[...], wdt_ref[...], preferred_element_type=F32) + dtb_ref[...]
    dt = jnp.maximum(dt_raw, 0.0) + jnp.log1p(jnp.exp(-jnp.abs(dt_raw)))
    for r0 in range(0, q, SSD_Q):
        _scan_chunk(slice(r0, r0 + SSD_Q), dt[r0:r0 + SSD_Q, :], act_ref, zs_ref, alog_ref, dskip_ref,
                    ng_ref, o_ref, state_ref)


def _scan_chunk(rows, dt, act_ref, zs_ref, alog_ref, dskip_ref, ng_ref, o_ref, state_ref):
    q = SSD_Q
    adt = dt * (-jnp.exp(alog_ref[...]))
    row = lax.broadcasted_iota(I32, (q, q), 0)
    col = lax.broadcasted_iota(I32, (q, q), 1)
    causal = row >= col
    tril = jnp.where(causal, 1.0, 0.0).astype(F32)
    acs = jnp.dot(tril, adt, preferred_element_type=F32, precision=lax.Precision.HIGHEST)
    acs_t = acs.T
    dt_t = dt.T
    last = acs[q - 1:q, :]
    exp_acs = jnp.exp(acs)
    dt_decay = dt * jnp.exp(last - acs)
    chunk_decay = jnp.exp(last)
    lane_in_pair = lax.broadcasted_iota(I32, (q, LANES), 1)
    head_of_lane = lax.broadcasted_iota(I32, (1, GROUP_CH), 1) // SSM_HEAD_DIM

    b_off = D_INNER
    c_off = D_INNER + SSM_GROUPS * D_STATE
    for g in range(SSM_GROUPS):
        ch = slice(g * GROUP_CH, (g + 1) * GROUP_CH)
        xg = act_ref[rows, ch]
        xg_bf = xg.astype(BF16)
        bg = act_ref[rows, b_off + g * D_STATE:b_off + (g + 1) * D_STATE]
        cg = act_ref[rows, c_off + g * D_STATE:c_off + (g + 1) * D_STATE].astype(BF16)
        cb = lax.dot_general(cg, bg.astype(BF16), (((1,), (1,)), ((), ())),
                             preferred_element_type=F32)
        ms, xblocks = [], []
        for r in range(HEADS_PER_GROUP):
            h = g * HEADS_PER_GROUP + r
            lmat = jnp.exp(jnp.where(causal, acs[:, h:h + 1] - acs_t[h:h + 1, :], -jnp.inf))
            ms.append((cb * lmat * dt_t[h:h + 1, :]).astype(BF16))
            head_mask = jnp.where(head_of_lane == r, 1.0, 0.0).astype(BF16)
            xblocks.append(xg_bf * head_mask)
        y_diag = jnp.dot(jnp.concatenate(ms, axis=1), jnp.concatenate(xblocks, axis=0),
                         preferred_element_type=F32)
        s_prev = state_ref[g]
        y_off = jnp.dot(cg, s_prev.astype(BF16), preferred_element_type=F32)
        yg = y_diag + y_off * _expand_heads(exp_acs, g, lane_in_pair) + xg * dskip_ref[:, ch]
        xw = (xg * _expand_heads(dt_decay, g, lane_in_pair)).astype(BF16)
        dec = _expand_heads(chunk_decay, g, lane_in_pair)
        state_ref[g] = s_prev * dec + jnp.dot(bg.T.astype(BF16), xw, preferred_element_type=F32)

        yz = yg * zs_ref[rows, g * GROUP_CH:(g + 1) * GROUP_CH].astype(F32)
        ms = jnp.mean(yz * yz, axis=-1, keepdims=True)
        yn = yz * lax.rsqrt(ms + RMS_EPS) * ng_ref[:, g * GROUP_CH:(g + 1) * GROUP_CH]
        o_ref[rows, g * GROUP_CH:(g + 1) * GROUP_CH] = yn.astype(o_ref.dtype)


def _ssd(xbc, u, zs, w_dt, dtb3, ssm_conv_w, ssm_conv_b3, alog3, dskip3, ng3, layer, bsz, seqlen):
    t = xbc.shape[0]
    q = min(SSD_STEP, seqlen)
    nq = seqlen // q
    tile = lambda width: pl.BlockSpec((q, width), lambda b, i: (b * nq + i, 0))
    vec = lambda width: pl.BlockSpec((None, 1, width), lambda b, i: (layer, 0, 0))
    return pl.pallas_call(
        functools.partial(_ssd_kernel, q=q),
        grid=(bsz, nq),
        in_specs=[tile(D_XBC), tile(D_MODEL), tile(D_INNER),
                  pl.BlockSpec((None, D_MODEL, LANES), lambda b, i: (layer, 0, 0)), vec(LANES),
                  pl.BlockSpec((None, SSM_CONV_WIDTH, D_XBC), lambda b, i: (layer, 0, 0)),
                  vec(D_XBC), vec(LANES), vec(D_INNER), vec(D_INNER)],
        out_specs=tile(D_INNER),
        out_shape=jax.ShapeDtypeStruct((t, D_INNER), BF16),
        scratch_shapes=[pltpu.VMEM((q + SSM_HALO, D_XBC), F32),
                        pltpu.VMEM((SSM_CONV_WIDTH - 1, min(SSD_CONV_ROWS, q), D_XBC), F32),
                        pltpu.VMEM((q, D_XBC), F32),
                        pltpu.VMEM((SSM_GROUPS, D_STATE, GROUP_CH), F32)],
        compiler_params=_params(2, LARGE_VMEM_LIMIT),
        name="ssd_mixer",
    )(xbc, u, zs, w_dt, dtb3, ssm_conv_w, ssm_conv_b3, alog3, dskip3, ng3)


def _first_argmax(vals):
    best, idx = vals[0], jnp.zeros(vals[0].shape, I32)
    for j in range(1, len(vals)):
        gt = vals[j] > best
        idx = jnp.where(gt, j, idx)
        best = jnp.where(gt, vals[j], best)
    return idx, best


def _select(idx, vals):
    out = vals[len(vals) - 1]
    for j in range(len(vals) - 2, -1, -1):
        out = jnp.where(idx == j, vals[j], out)
    return out


def _outproj_kernel(yn_ref, y1_ref, gs_ref, x_ref, wso_ref, wo_ref, lg_ref, lb_ref, wr_ref, rb_ref,
                    x1_ref, x1b_ref, pos_ref, wts_ref, cnt_ref, seg_ref,
                    wso_bf, wo_bf, base_ref, *, tm):
    @pl.when(pl.program_id(0) == 0)
    def _():
        wso_bf[...] = wso_ref[...].astype(BF16)
        wo_bf[...] = wo_ref[...].astype(BF16)
        base_ref[...] = jnp.zeros(base_ref.shape, F32)

    y_ssm = jnp.dot(yn_ref[...], wso_bf[...], preferred_element_type=F32)
    merged = y1_ref[...].astype(F32) + gs_ref[...].astype(F32) * y_ssm
    mix = jnp.dot(merged.astype(BF16), wo_bf[...], preferred_element_type=F32)
    x1 = _layer_norm(ALPHA * x_ref[...] + mix, lg_ref[...], lb_ref[...])
    x1_ref[...] = x1
    x1b_ref[...] = x1.astype(BF16)

    logits = lax.dot_general(wr_ref[...], x1, (((1,), (1,)), ((), ())),
                             preferred_element_type=F32, precision=lax.Precision.HIGHEST)
    scores = _sigmoid(logits)
    sel = scores + rb_ref[...]
    sel_rows = [sel[e:e + 1, :] for e in range(N_EXPERTS)]
    sc_rows = [scores[e:e + 1, :] for e in range(N_EXPERTS)]

    gscores = []
    for gi in range(N_EXPERT_GROUPS):
        v = sel_rows[gi * EXPERTS_PER_GROUP:(gi + 1) * EXPERTS_PER_GROUP]
        best = None
        for a in range(EXPERTS_PER_GROUP):
            for b in range(a + 1, EXPERTS_PER_GROUP):
                s = v[a] + v[b]
                best = s if best is None else jnp.maximum(best, s)
        gscores.append(best)
    grp, _ = _first_argmax(gscores)

    sel_in = [_select(grp, [sel_rows[gi * EXPERTS_PER_GROUP + j] for gi in range(N_EXPERT_GROUPS)])
              for j in range(EXPERTS_PER_GROUP)]
    sc_in = [_select(grp, [sc_rows[gi * EXPERTS_PER_GROUP + j] for gi in range(N_EXPERT_GROUPS)])
             for j in range(EXPERTS_PER_GROUP)]
    i1, _ = _first_argmax(sel_in)
    neg = jnp.full(sel_in[0].shape, -jnp.inf, F32)
    i2, _ = _first_argmax([jnp.where(i1 == j, neg, sel_in[j]) for j in range(EXPERTS_PER_GROUP)])
    s1 = _select(i1, sc_in)
    s2 = _select(i2, sc_in)
    tot = s1 + s2
    e1 = grp * EXPERTS_PER_GROUP + i1
    e2 = grp * EXPERTS_PER_GROUP + i2
    wts_ref[0:1, :] = s1 / tot
    wts_ref[1:2, :] = s2 / tot

    eio = lax.broadcasted_iota(I32, (N_EXPERTS, tm), 0)
    oh1 = jnp.where(eio == e1, 1.0, 0.0).astype(F32)
    oh2 = jnp.where(eio == e2, 1.0, 0.0).astype(F32)
    both = oh1 + oh2
    srow = lax.broadcasted_iota(I32, (tm, tm), 0)
    scol = lax.broadcasted_iota(I32, (tm, tm), 1)
    before = jnp.where(srow < scol, 1.0, 0.0).astype(BF16)
    cum = jnp.dot(both.astype(BF16), before, preferred_element_type=F32)
    cnt8 = jnp.floor((jnp.sum(both, axis=1, keepdims=True) + 7.0) * 0.125) * 8.0
    cnt8_l = jnp.broadcast_to(cnt8, (N_EXPERTS, LANES))
    erow = lax.broadcasted_iota(I32, (N_EXPERTS, N_EXPERTS), 0)
    ecol = lax.broadcasted_iota(I32, (N_EXPERTS, N_EXPERTS), 1)
    seg_off = jnp.dot(jnp.where(ecol < erow, 1.0, 0.0).astype(F32), cnt8_l,
                      preferred_element_type=F32, precision=lax.Precision.HIGHEST)
    base = base_ref[...]
    in_tile = seg_off[:, 0:1] + cum
    pos_ref[0:1, :] = jnp.sum(oh1 * in_tile, axis=0, keepdims=True).astype(I32)
    pos_ref[1:2, :] = jnp.sum(oh2 * in_tile, axis=0, keepdims=True).astype(I32)
    lane = lax.broadcasted_iota(I32, (N_EXPERTS, LANES), 1)
    seg_ref[...] = jnp.where(lane == 0, seg_off, jnp.where(lane == 1, cnt8_l, base)).astype(I32)
    base_ref[...] = base + cnt8_l
    cnt_ref[...] = base_ref[...].astype(I32)


def _outproj(yn, y1g, gates, x, w_ssm_out, w_out, ln_g3, ln_b3, wr_t, rbias, layer, tm):
    t = x.shape[0]
    tile = lambda width: pl.BlockSpec((tm, width), lambda m: (m, 0))
    vec = pl.BlockSpec((None, 1, D_MODEL), lambda m: (layer, 0, 0))
    pair = pl.BlockSpec((2, tm), lambda m: (0, m))
    return pl.pallas_call(
        functools.partial(_outproj_kernel, tm=tm),
        grid=(t // tm,),
        in_specs=[tile(D_INNER), tile(D_MODEL),
                  pl.BlockSpec((tm, D_MODEL), lambda m: (m, 1)),
                  tile(D_MODEL),
                  pl.BlockSpec((None, D_INNER, D_MODEL), lambda m: (layer, 0, 0)),
                  pl.BlockSpec((None, D_MODEL, D_MODEL), lambda m: (layer, 0, 0)),
                  vec, vec,
                  pl.BlockSpec((N_EXPERTS, D_MODEL), lambda m: (0, 0)),
                  pl.BlockSpec((N_EXPERTS, 1), lambda m: (0, 0))],
        out_specs=[tile(D_MODEL), tile(D_MODEL), pair, pair,
                   pl.BlockSpec((N_EXPERTS, LANES), lambda m: (0, 0)),
                   pl.BlockSpec((None, N_EXPERTS, LANES), lambda m: (m, 0, 0))],
        out_shape=[jax.ShapeDtypeStruct((t, D_MODEL), F32),
                   jax.ShapeDtypeStruct((t, D_MODEL), BF16),
                   jax.ShapeDtypeStruct((2, t), I32),
                   jax.ShapeDtypeStruct((2, t), F32),
                   jax.ShapeDtypeStruct((N_EXPERTS, LANES), I32),
                   jax.ShapeDtypeStruct((t // tm, N_EXPERTS, LANES), I32)],
        scratch_shapes=[pltpu.VMEM((D_INNER, D_MODEL), BF16),
                        pltpu.VMEM((D_MODEL, D_MODEL), BF16),
                        pltpu.VMEM((N_EXPERTS, LANES), F32)],
        compiler_params=_params(1),
        name="outproj_ln_router",
    )(yn, y1g, gates, x, w_ssm_out, w_out, ln_g3, ln_b3, wr_t, rbias)


def _segment_blocks(tile, segoff_ref, cnt8_ref, gdst_ref, make_copy, act):
    for e in range(N_EXPERTS):
        idx = tile * N_EXPERTS + e
        n8 = lax.shift_right_logical(cnt8_ref[idx], 3)
        local = segoff_ref[idx]
        glob = gdst_ref[idx]
        for bit in SEGMENT_BITS:
            rows = 8 * bit
            hit = (n8 & bit) != 0

            @pl.when(hit)
            def _(local=local, glob=glob, rows=rows):
                act(make_copy(pl.multiple_of(local, 8), pl.multiple_of(glob, 8), rows), e % 2)

            step = jnp.where(hit, rows, 0)
            local = local + step
            glob = glob + step


def _dispatch_kernel(fill_ref, segoff_ref, cnt8_ref, gdst_ref, pos_ref, x_ref, xs_hbm,
                     sorted_ref, zeros_ref, sem, blk_sem, seg_sems, *, tm, n_blk, n_tiles):
    m = pl.program_id(0)
    slot = lax.rem(m, 2)

    def zero_rows(row):
        return pltpu.make_async_copy(zeros_ref.at[pl.ds(0, 8)],
                                     xs_hbm.at[pl.ds(pl.multiple_of(row, 8), 8)], sem)

    def zero_block(b):
        return pltpu.make_async_copy(
            zeros_ref, xs_hbm.at[pl.ds(pl.multiple_of(b * MOE_ROWS, MOE_ROWS), MOE_ROWS)], blk_sem)

    @pl.when(m == 0)
    def _():
        zeros_ref[...] = jnp.zeros(zeros_ref.shape, F32)
        n_active = fill_ref[2 * N_EXPERTS]
        for e in range(N_EXPERTS):
            pad_start = fill_ref[e]
            lax.fori_loop(0, fill_ref[N_EXPERTS + e],
                          lambda j, c: (zero_rows(pad_start + 8 * j).start(), c)[1], 0)
        lax.fori_loop(n_active, n_blk, lambda b, c: (zero_block(b).start(), c)[1], 0)
        lax.fori_loop(0, fill_ref[2 * N_EXPERTS + 1], lambda j, c: (zero_rows(0).wait(), c)[1], 0)
        lax.fori_loop(n_active, n_blk, lambda b, c: (zero_block(b).wait(), c)[1], 0)

    def for_each_segment_block(tile, s, act):
        copy = lambda local, glob, rows: pltpu.make_async_copy(
            sorted_ref.at[s, pl.ds(local, rows)], xs_hbm.at[pl.ds(glob, rows)], seg_sems.at[s])
        _segment_blocks(tile, segoff_ref, cnt8_ref, gdst_ref, copy, act)

    start = lambda copy, priority: copy.start(priority=priority)
    wait = lambda copy, priority: copy.wait()

    @pl.when(m >= 2)
    def _():
        for_each_segment_block(m - 2, slot, wait)

    pos = pos_ref[...]
    srow = lax.broadcasted_iota(I32, (sorted_ref.shape[1], tm), 0)
    onehot = jnp.where(srow == pos[0:1, :], 1.0, jnp.where(srow == pos[1:2, :], 1.0, 0.0)).astype(BF16)
    sorted_ref[slot] = jnp.dot(onehot, x_ref[...], preferred_element_type=F32)
    for_each_segment_block(m, slot, start)

    @pl.when(m == n_tiles - 1)
    def _():
        if n_tiles > 1:
            for_each_segment_block(m - 1, 1 - slot, wait)
        for_each_segment_block(m, slot, wait)


def _dispatch(fill, seg_off, seg_cnt8, seg_dst, pos, x1b, n_blk, tm):
    t = x1b.shape[0]
    n_tiles = t // tm
    sorted_rows = 2 * tm + 8 * N_EXPERTS
    grid_spec = pltpu.PrefetchScalarGridSpec(
        num_scalar_prefetch=4,
        grid=(n_tiles,),
        in_specs=[pl.BlockSpec((2, tm), lambda m, *_: (0, m)),
                  pl.BlockSpec((tm, D_MODEL), lambda m, *_: (m, 0))],
        out_specs=pl.BlockSpec(memory_space=pl.ANY),
        scratch_shapes=[pltpu.VMEM((2, sorted_rows, D_MODEL), F32),
                        pltpu.VMEM((MOE_ROWS, D_MODEL), F32),
                        pltpu.SemaphoreType.DMA(()), pltpu.SemaphoreType.DMA(()),
                        pltpu.SemaphoreType.DMA((2,))],
    )
    return pl.pallas_call(
        functools.partial(_dispatch_kernel, tm=tm, n_blk=n_blk, n_tiles=n_tiles),
        grid_spec=grid_spec,
        out_shape=jax.ShapeDtypeStruct((n_blk * MOE_ROWS, D_MODEL), F32),
        compiler_params=_params(1),
        name="moe_dispatch",
    )(fill, seg_off, seg_cnt8, seg_dst, pos, x1b)


def _expert_kernel(src_ref, exp_ref, nvalid_ref, xs_ref, wg_ref, wu_ref, wd_ref, y_ref,
                   wgu_bf, wd_bf):
    i = pl.program_id(0)
    nvalid = nvalid_ref[i]
    changed = (i == 0) | (exp_ref[i] != exp_ref[jnp.maximum(i - 1, 0)])

    @pl.when((nvalid > 0) & changed)
    def _():
        wgu_bf[:, :D_EXPERT] = wg_ref[...].astype(BF16)
        wgu_bf[:, D_EXPERT:] = wu_ref[...].astype(BF16)
        wd_bf[...] = wd_ref[...].astype(BF16)

    @pl.when(nvalid > 0)
    def _():
        rows = lax.broadcasted_iota(I32, (MOE_ROWS, 1), 0)
        x = jnp.where(rows < nvalid, xs_ref[...], 0.0).astype(BF16)
        gu = jnp.dot(x, wgu_bf[...], preferred_element_type=F32)
        hg = gu[:, :D_EXPERT]
        h = hg * _sigmoid(hg) * gu[:, D_EXPERT:]
        y_ref[...] = jnp.dot(h.astype(BF16), wd_bf[...], preferred_element_type=F32)

    @pl.when(nvalid == 0)
    def _():
        y_ref[...] = jnp.zeros(y_ref.shape, F32)


def _experts(blk_src, blk_exp, blk_nvalid, xs, wg, wu, wd, layer):
    n_rows = xs.shape[0]
    n_blk = n_rows // MOE_ROWS
    grid_spec = pltpu.PrefetchScalarGridSpec(
        num_scalar_prefetch=3,
        grid=(n_blk,),
        in_specs=[pl.BlockSpec((MOE_ROWS, D_MODEL), lambda i, src, exp, nv: (src[i], 0)),
                  pl.BlockSpec((None, None, D_MODEL, D_EXPERT),
                               lambda i, src, exp, nv: (layer, exp[i], 0, 0)),
                  pl.BlockSpec((None, None, D_MODEL, D_EXPERT),
                               lambda i, src, exp, nv: (layer, exp[i], 0, 0)),
                  pl.BlockSpec((None, None, D_EXPERT, D_MODEL),
                               lambda i, src, exp, nv: (layer, exp[i], 0, 0))],
        out_specs=pl.BlockSpec((MOE_ROWS, D_MODEL), lambda i, src, exp, nv: (i, 0)),
        scratch_shapes=[pltpu.VMEM((D_MODEL, 2 * D_EXPERT), BF16),
                        pltpu.VMEM((D_EXPERT, D_MODEL), BF16)],
    )
    return pl.pallas_call(
        _expert_kernel,
        grid_spec=grid_spec,
        out_shape=jax.ShapeDtypeStruct((n_rows, D_MODEL), F32),
        compiler_params=_params(1),
        name="moe_experts",
    )(blk_src, blk_exp, blk_nvalid, xs, wg, wu, wd)


def _moe_plan(counts, seg, t):
    cnt = counts[:, 0]
    nblk_e = (cnt + MOE_ROWS - 1) // MOE_ROWS
    blk_end = jnp.cumsum(nblk_e)
    blk_start = blk_end - nblk_e
    n_active = blk_end[N_EXPERTS - 1]
    experts = jnp.arange(N_EXPERTS, dtype=I32)
    row_start = blk_start * MOE_ROWS
    n_tiles = seg.shape[0]
    n_blk = -(-(2 * t + 8 * N_EXPERTS * n_tiles) // MOE_ROWS) + N_EXPERTS
    ids = jnp.arange(n_blk, dtype=I32)
    src = jnp.minimum(ids, n_active - 1)
    exp = jnp.minimum(jnp.sum((blk_end[None, :] <= src[:, None]).astype(I32), axis=1), N_EXPERTS - 1)
    onehot = (exp[:, None] == experts[None, :]).astype(I32)
    pick = lambda table: jnp.sum(onehot * table[None, :], axis=1)
    left = pick(cnt) - (src - pick(blk_start)) * MOE_ROWS
    nvalid = jnp.where(ids < n_active, jnp.clip(left, 0, MOE_ROWS), 0)
    pad8 = (nblk_e * MOE_ROWS - cnt) // 8
    fill = jnp.concatenate([row_start + cnt, pad8, jnp.stack([n_active, jnp.sum(pad8)])]).astype(I32)
    seg_off = seg[:, :, 0].reshape(-1)
    seg_cnt8 = seg[:, :, 1].reshape(-1)
    seg_dst = (row_start[None, :] + seg[:, :, 2]).reshape(-1)
    return src.astype(I32), exp, nvalid.astype(I32), fill, seg_off, seg_cnt8, seg_dst, n_blk


def _final_kernel(segoff_ref, cnt8_ref, gsrc_ref, x1_ref, x1b_ref, p_ref, wt_ref, pos_ref, wpu_ref,
                  wpg_ref, bpg_ref, lg_ref, lb_ref, y_hbm, x2_ref, x2b_ref, ys_ref, wpu_bf, wpg_bf, sems,
                  *, tm, n_tiles):
    m = pl.program_id(0)
    slot = lax.rem(m, 2)

    def for_each_segment_block(tile, s, act):
        copy = lambda local, glob, rows: pltpu.make_async_copy(
            y_hbm.at[pl.ds(glob, rows)], ys_ref.at[s, pl.ds(local, rows)], sems.at[s])
        _segment_blocks(tile, segoff_ref, cnt8_ref, gsrc_ref, copy, act)

    @pl.when(m == 0)
    def _():
        wpu_bf[...] = wpu_ref[...].astype(BF16)
        wpg_bf[...] = wpg_ref[...].astype(BF16)
        for_each_segment_block(0, 0, lambda copy, priority: copy.start(priority=priority))

    @pl.when(m + 1 < n_tiles)
    def _():
        for_each_segment_block(m + 1, 1 - slot, lambda copy, priority: copy.start(priority=priority))

    up = jnp.dot(p_ref[...].astype(BF16), wpu_bf[...], preferred_element_type=F32)
    gate = _sigmoid(jnp.dot(x1b_ref[...], wpg_bf[...], preferred_element_type=F32) + bpg_ref[...])
    resid = ALPHA * x1_ref[...] + up * gate

    for_each_segment_block(m, slot, lambda copy, priority: copy.wait())

    n_sorted = ys_ref.shape[1]
    used = segoff_ref[m * N_EXPERTS + N_EXPERTS - 1] + cnt8_ref[m * N_EXPERTS + N_EXPERTS - 1]
    srow = lax.broadcasted_iota(I32, (n_sorted, 1), 0)
    ys = jnp.where(srow < used, ys_ref[slot], 0.0).astype(BF16)
    pos = pos_ref[...]
    scol = lax.broadcasted_iota(I32, (tm, n_sorted), 1)
    wt = wt_ref[...]
    moe = jnp.zeros((tm, D_MODEL), F32)
    for k in range(TOP_K):
        pick = jnp.where(scol == pos[:, k:k + 1], 1.0, 0.0).astype(BF16)
        moe = moe + wt[:, k:k + 1] * jnp.dot(pick, ys, preferred_element_type=F32)
    x2 = _layer_norm(resid + moe, lg_ref[...], lb_ref[...])
    x2_ref[...] = x2
    x2b_ref[...] = x2.astype(BF16)


def _final(seg_off, seg_cnt8, seg_src, x1, x1b, p, wt_tok, pos_tok, w_ple_up, w_ple_gate, b_pg3,
           ln_g3, ln_b3, y_rows, layer, tm):
    t = x1.shape[0]
    n_tiles = t // tm
    sorted_rows = 2 * tm + 8 * N_EXPERTS
    tile = lambda width: pl.BlockSpec((tm, width), lambda m, *_: (m, 0))
    vec = pl.BlockSpec((None, 1, D_MODEL), lambda m, *_: (layer, 0, 0))
    grid_spec = pltpu.PrefetchScalarGridSpec(
        num_scalar_prefetch=3,
        grid=(n_tiles,),
        in_specs=[tile(D_MODEL), tile(D_MODEL),
                  pl.BlockSpec((None, tm, PLE_DIM), lambda m, *_: (layer, m, 0)),
                  tile(2), tile(2),
                  pl.BlockSpec((None, PLE_DIM, D_MODEL), lambda m, *_: (layer, 0, 0)),
                  pl.BlockSpec((None, D_MODEL, D_MODEL), lambda m, *_: (layer, 0, 0)),
                  vec, vec, vec,
                  pl.BlockSpec(memory_space=pl.ANY)],
        out_specs=[tile(D_MODEL), tile(D_MODEL)],
        scratch_shapes=[pltpu.VMEM((2, sorted_rows, D_MODEL), F32),
                        pltpu.VMEM((PLE_DIM, D_MODEL), BF16), pltpu.VMEM((D_MODEL, D_MODEL), BF16),
                        pltpu.SemaphoreType.DMA((2,))],
    )
    return pl.pallas_call(
        functools.partial(_final_kernel, tm=tm, n_tiles=n_tiles),
        grid_spec=grid_spec,
        out_shape=[jax.ShapeDtypeStruct((t, D_MODEL), F32),
                   jax.ShapeDtypeStruct((t, D_MODEL), BF16)],
        compiler_params=_params(1),
        name="combine_ple_ln",
    )(seg_off, seg_cnt8, seg_src, x1, x1b, p, wt_tok, pos_tok, w_ple_up, w_ple_gate, b_pg3,
      ln_g3, ln_b3, y_rows)


def kernel(x, p, w_in, b_glu, b_branch_gate, conv_w, conv_b, conv_ln_g, conv_ln_b, w_conv_out,
           ssm_conv_w, ssm_conv_b, dt_bias, a_log, d_skip, ssm_norm_g, w_ssm_out, w_out,
           ln1_g, ln1_b, w_router, router_bias, w_exp_gate, w_exp_up, w_exp_down,
           w_ple_up, w_ple_gate, b_ple_gate, ln2_g, ln2_b):
    bsz, seqlen, d = x.shape
    depth = w_in.shape[0]
    t = bsz * seqlen
    tm_moe = min(MOE_TILE, t)

    row3 = lambda a: a.reshape(a.shape[0], 1, a.shape[1])
    pad_lanes = lambda a: jnp.pad(a, ((0, 0), (0, LANES - a.shape[1])))
    b_glu3, b_gate3 = row3(b_glu), row3(b_branch_gate)
    conv_b3, cln_g3, cln_b3 = row3(conv_b), row3(conv_ln_g), row3(conv_ln_b)
    ssm_conv_b3, ng3 = row3(ssm_conv_b), row3(ssm_norm_g)
    dtb3, alog3 = row3(pad_lanes(dt_bias)), row3(pad_lanes(a_log))
    dskip3 = row3(jnp.repeat(d_skip, SSM_HEAD_DIM, axis=1))
    ln1_g3, ln1_b3, ln2_g3, ln2_b3 = row3(ln1_g), row3(ln1_b), row3(ln2_g), row3(ln2_b)
    b_pg3 = row3(b_ple_gate)
    w_in_t = jnp.swapaxes(w_in, 1, 2)
    w_gate, w_dt = _tail_weights(w_in_t)
    wr_t = w_router.T
    rbias = router_bias.reshape(N_EXPERTS, 1)
    p2 = p.reshape(depth, t, PLE_DIM)

    xf = x.reshape(t, d)
    xb = xf.astype(BF16)
    for i in range(depth):
        c = _glu(xb, w_in_t, b_glu3, i)
        zs = _matmul(xb, w_in_t, lambda n: (i, OFF_Z // 1024 + n, 0), None, None,
                     D_INNER, 1024, "silu", BF16, w_rows_are_outputs=True)
        xbc = _matmul(xb, w_in_t, lambda n: (i, OFF_XBC // 1024 + n, 0), None, None,
                      D_XBC, 1024, "none", BF16, w_rows_are_outputs=True)
        gates = _matmul(xb, w_gate, lambda n: (i, 0, n), b_gate3, lambda n: (i, 0, n),
                        2 * D_MODEL, 1024, "sigmoid", BF16)
        y1g = _convbranch(c, gates, conv_w, conv_b3, cln_g3, cln_b3, w_conv_out, i, bsz, seqlen)
        yn = _ssd(xbc, xb, zs, w_dt, dtb3, ssm_conv_w, ssm_conv_b3, alog3, dskip3, ng3, i, bsz, seqlen)
        x1, x1b, pos, wts, counts, seg = _outproj(
            yn, y1g, gates, xf, w_ssm_out, w_out, ln1_g3, ln1_b3, wr_t, rbias, i, tm_moe)
        (blk_src, blk_exp, blk_nvalid, fill, seg_off, seg_cnt8, seg_row,
         n_blk) = _moe_plan(counts, seg, t)
        xs = _dispatch(fill, seg_off, seg_cnt8, seg_row, pos, x1b, n_blk, tm_moe)
        y_rows = _experts(blk_src, blk_exp, blk_nvalid, xs, w_exp_gate, w_exp_up, w_exp_down, i)
        xf, xb = _final(seg_off, seg_cnt8, seg_row, x1, x1b, p2, wts.T, pos.T,
                        w_ple_up, w_ple_gate, b_pg3, ln2_g3, ln2_b3, y_rows, i, tm_moe)
    return xf.reshape(bsz, seqlen, d)
```
